```python
import jax, jax.numpy as jnp
from jax import lax
import numpy as np

D_MODEL = 1024
BATCH = 2
SEQ = 8192
DEPTH = 4
DEC_BATCH = 32
DEC_SEQ = 16
PAST_LEN = 2048

CHUNK = 64
N_MIXERS = 2
N_FOX = (DEPTH + 1) // 2
N_LRU = DEPTH // 2
FOX_HEADS = 16
FOX_HEAD_DIM = 64
D_ATTN = FOX_HEADS * FOX_HEAD_DIM
QBLOCK = 128
D_RNN = D_MODEL
LRU_BLOCKS = 16
LRU_BW = D_RNN // LRU_BLOCKS
CONV_W = 4
LRU_C = 8.0
D_FF = 3584
N_EXPERTS = 8
TOP_K = 2
D_FF_E = 3584
N_MOD = 6
EPS = 1e-6

kernel_name = 'hybrid_fox_rglru_streaming_step'


def _rmsnorm(x, g):
    x32 = x.astype(jnp.float32)
    y = x32 * lax.rsqrt(jnp.mean(x32 * x32, axis=-1, keepdims=True) + EPS)
    return y.astype(x.dtype) * g


def _fox_block(q, cq, pq, k, v, ck, pk):
    s = jnp.einsum('bqhd,bkhd->bhqk', q, k).astype(jnp.float32) * (FOX_HEAD_DIM ** -0.5)
    bias = jnp.swapaxes(cq, 1, 2)[:, :, :, None] - jnp.swapaxes(ck, 1, 2)[:, :, None, :]
    mask = pk[None, :] <= pq[:, None]
    s = jnp.where(mask, s + bias, -jnp.inf)
    p = jax.nn.softmax(s, axis=-1).astype(v.dtype)
    return jnp.einsum('bhqk,bkhd->bqhd', p, v)


def _fox_mixer(h, k_past, v_past, logf_past, w_in, b_f, w_out):
    B, S, _ = h.shape
    P = k_past.shape[1]
    proj = h @ w_in
    q = proj[..., :D_ATTN].reshape(B, S, FOX_HEADS, FOX_HEAD_DIM)
    k = proj[..., D_ATTN:2 * D_ATTN].reshape(B, S, FOX_HEADS, FOX_HEAD_DIM)
    v = proj[..., 2 * D_ATTN:3 * D_ATTN].reshape(B, S, FOX_HEADS, FOX_HEAD_DIM)
    logf = jax.nn.log_sigmoid((proj[..., 3 * D_ATTN:] + b_f).astype(jnp.float32))
    k_all = jnp.concatenate([k_past, k], axis=1)
    v_all = jnp.concatenate([v_past, v], axis=1)
    c_all = jnp.cumsum(jnp.concatenate([logf_past.astype(jnp.float32), logf], axis=1), axis=1)
    cq = c_all[:, P:]
    pk = jnp.arange(P + S)
    pq = P + jnp.arange(S)
    qb = min(QBLOCK, S)
    nb = S // qb
    qs = jnp.moveaxis(q.reshape(B, nb, qb, FOX_HEADS, FOX_HEAD_DIM), 1, 0)
    cqs = jnp.moveaxis(cq.reshape(B, nb, qb, FOX_HEADS), 1, 0)
    pqs = pq.reshape(nb, qb)
    out = lax.map(lambda a: _fox_block(a[0], a[1], a[2], k_all, v_all, c_all, pk), (qs, cqs, pqs))
    out = jnp.moveaxis(out, 0, 1).reshape(B, S, D_ATTN)
    return out @ w_out, k, v, logf.astype(h.dtype)


def _lru_combine(left, right):
    a_l, b_l = left
    a_r, b_r = right
    return a_l * a_r, a_r * b_l + b_r


def _rglru_mixer(h, h0, conv_buf, w_in, conv_w, conv_b, w_a, b_a, w_i, b_i, lam, w_out):
    B, S, _ = h.shape
    proj = h @ w_in
    xb, gb = proj[..., :D_RNN], proj[..., D_RNN:]
    xpad = jnp.concatenate([conv_buf, xb], axis=1)
    xc = conv_b + sum(xpad[:, j:j + S] * conv_w[j] for j in range(CONV_W))
    new_buf = xpad[:, S:]
    xg = xc.reshape(B, S, LRU_BLOCKS, LRU_BW)
    r = jax.nn.sigmoid((jnp.einsum('bsnk,nkj->bsnj', xg, w_a).reshape(B, S, D_RNN) + b_a).astype(jnp.float32))
    gi = jax.nn.sigmoid((jnp.einsum('bsnk,nkj->bsnj', xg, w_i).reshape(B, S, D_RNN) + b_i).astype(jnp.float32))
    log_a = LRU_C * r * jax.nn.log_sigmoid(lam.astype(jnp.float32))
    a = jnp.exp(log_a)
    u = jnp.sqrt(-jnp.expm1(2.0 * log_a)) * gi * xc.astype(jnp.float32)
    u = u.at[:, 0].add(a[:, 0] * h0.astype(jnp.float32))
    _, hs = lax.associative_scan(_lru_combine, (a, u), axis=1)
    y = hs.astype(h.dtype) * jax.nn.gelu(gb)
    return y @ w_out, hs[:, -1].astype(h.dtype), new_buf


def _swiglu(h, w_gu, w_down):
    gu = h @ w_gu
    d = w_down.shape[0]
    return (jax.nn.silu(gu[..., :d]) * gu[..., d:]) @ w_down


def _moe(h, w_router, w_gu, w_down):
    logits = (h @ w_router).astype(jnp.float32)
    top_v, top_i = lax.top_k(logits, TOP_K)
    wts = jax.nn.softmax(top_v, axis=-1)
    gate = jnp.sum(jax.nn.one_hot(top_i, N_EXPERTS, dtype=jnp.float32) * wts[..., None], axis=-2)
    gate = gate.astype(h.dtype)
    y = jnp.zeros_like(h)
    for e in range(N_EXPERTS):
        y = y + gate[..., e:e + 1] * _swiglu(h, w_gu[e], w_down[e])
    return y


def _trunk(x, c, fox_past, lru_past, p):
    fox_new = []
    lru_new = []
    for l in range(DEPTH):
        j = l // N_MIXERS
        mod = jax.nn.silu(c) @ p['w_ada'][l] + p['b_ada'][l]
        sh_a, sc_a, g_a, sh_m, sc_m, g_m = jnp.split(mod[:, None, :], N_MOD, axis=-1)
        h = _rmsnorm(x, p['norm_mix_g'][l]) * (1 + sc_a) + sh_a
        if l % N_MIXERS == 0:
            y, k, v, lf = _fox_mixer(h, fox_past[j][0], fox_past[j][1], fox_past[j][2],
                                     p['fox_w_in'][j], p['fox_b_f'][j], p['fox_w_out'][j])
            fox_new.append((k, v, lf))
        else:
            y, hl, cb = _rglru_mixer(h, lru_past[j][0], lru_past[j][1],
                                     p['lru_w_in'][j], p['lru_conv_w'][j], p['lru_conv_b'][j],
                                     p['lru_w_a'][j], p['lru_b_a'][j], p['lru_w_i'][j], p['lru_b_i'][j],
                                     p['lru_lam'][j], p['lru_w_out'][j])
            lru_new.append((hl, cb))
        x = x + g_a * y
        h = _rmsnorm(x, p['norm_ffn_g'][l]) * (1 + sc_m) + sh_m
        if l % 2 == 0:
            y = _swiglu(h, p['ffn_w_gu'][j], p['ffn_w_down'][j])
        else:
            y = _moe(h, p['moe_router'][j], p['moe_w_gu'][j], p['moe_w_down'][j])
        x = x + g_m * y
    return _rmsnorm(x, p['final_g']), fox_new, lru_new


def _nrm(k, shape, scale):
    return scale * jax.random.normal(k, shape, jnp.float32)


def setup_inputs(seed: int = 0) -> dict:
    key = jax.random.key(seed)
    ks = jax.random.split(key, 32)
    D = D_MODEL
    u = jax.random.uniform(ks[24], (N_LRU, D_RNN), jnp.float32, minval=0.9, maxval=0.999)
    a_base = u ** (1.0 / LRU_C)
    lam = jnp.log(a_base) - jnp.log1p(-a_base)
    return {
        'x_prompt': _nrm(ks[0], (BATCH, SEQ, D), 1.0),
        'x_sample': _nrm(ks[1], (DEC_BATCH, DEC_SEQ, D), 1.0),
        'c_prompt': _nrm(ks[2], (BATCH, D), 1.0),
        'c_sample': _nrm(ks[3], (DEC_BATCH, D), 1.0),
        'cache_k': _nrm(ks[4], (N_FOX, DEC_BATCH, PAST_LEN, FOX_HEADS, FOX_HEAD_DIM), 1.0),
        'cache_v': _nrm(ks[5], (N_FOX, DEC_BATCH, PAST_LEN, FOX_HEADS, FOX_HEAD_DIM), 1.0),
        'cache_logf': jax.nn.log_sigmoid(2.5 + _nrm(ks[6], (N_FOX, DEC_BATCH, PAST_LEN, FOX_HEADS), 1.0)),
        'state_h': _nrm(ks[7], (N_LRU, DEC_BATCH, D_RNN), 0.5),
        'state_conv': _nrm(ks[8], (N_LRU, DEC_BATCH, CONV_W - 1, D_RNN), 1.0),
        'norm_mix_g': 1.0 + _nrm(ks[9], (DEPTH, D), 0.05),
        'norm_ffn_g': 1.0 + _nrm(ks[10], (DEPTH, D), 0.05),
        'final_g': 1.0 + _nrm(ks[11], (D,), 0.05),
        'w_ada': _nrm(ks[12], (DEPTH, D, N_MOD * D), 0.3 * D ** -0.5),
        'b_ada': _nrm(ks[13], (DEPTH, N_MOD * D), 0.02),
        'fox_w_in': _nrm(ks[14], (N_FOX, D, 3 * D_ATTN + FOX_HEADS), D ** -0.5),
        'fox_b_f': jax.random.uniform(ks[15], (N_FOX, FOX_HEADS), jnp.float32, minval=1.0, maxval=4.0),
        'fox_w_out': _nrm(ks[16], (N_FOX, D_ATTN, D), D_ATTN ** -0.5),
        'lru_w_in': _nrm(ks[17], (N_LRU, D, 2 * D_RNN), D ** -0.5),
        'lru_conv_w': _nrm(ks[18], (N_LRU, CONV_W, D_RNN), CONV_W ** -0.5),
        'lru_conv_b': _nrm(ks[19], (N_LRU, D_RNN), 0.02),
        'lru_w_a': _nrm(ks[20], (N_LRU, LRU_BLOCKS, LRU_BW, LRU_BW), LRU_BW ** -0.5),
        'lru_b_a': _nrm(ks[21], (N_LRU, D_RNN), 0.02),
        'lru_w_i': _nrm(ks[22], (N_LRU, LRU_BLOCKS, LRU_BW, LRU_BW), LRU_BW ** -0.5),
        'lru_b_i': _nrm(ks[23], (N_LRU, D_RNN), 0.02),
        'lru_lam': lam,
        'lru_w_out': _nrm(ks[25], (N_LRU, D_RNN, D), D_RNN ** -0.5),
        'ffn_w_gu': _nrm(ks[26], (N_FOX, D, 2 * D_FF), D ** -0.5),
        'ffn_w_down': _nrm(ks[27], (N_FOX, D_FF, D), D_FF ** -0.5),
        'moe_router': _nrm(ks[28], (N_LRU, D, N_EXPERTS), D ** -0.5),
        'moe_w_gu': _nrm(ks[29], (N_LRU, N_EXPERTS, D, 2 * D_FF_E), D ** -0.5),
        'moe_w_down': _nrm(ks[30], (N_LRU, N_EXPERTS, D_FF_E, D), D_FF_E ** -0.5),
    }


def reference(x_prompt, x_sample, c_prompt, c_sample, cache_k, cache_v, cache_logf, state_h, state_conv,
              norm_mix_g, norm_ffn_g, final_g, w_ada, b_ada, fox_w_in, fox_b_f, fox_w_out,
              lru_w_in, lru_conv_w, lru_conv_b, lru_w_a, lru_b_a, lru_w_i, lru_b_i, lru_lam, lru_w_out,
              ffn_w_gu, ffn_w_down, moe_router, moe_w_gu, moe_w_down):
    p = {
        'norm_mix_g': norm_mix_g, 'norm_ffn_g': norm_ffn_g, 'final_g': final_g,
        'w_ada': w_ada, 'b_ada': b_ada,
        'fox_w_in': fox_w_in, 'fox_b_f': fox_b_f, 'fox_w_out': fox_w_out,
        'lru_w_in': lru_w_in, 'lru_conv_w': lru_conv_w, 'lru_conv_b': lru_conv_b,
        'lru_w_a': lru_w_a, 'lru_b_a': lru_b_a, 'lru_w_i': lru_w_i, 'lru_b_i': lru_b_i,
        'lru_lam': lru_lam, 'lru_w_out': lru_w_out,
        'ffn_w_gu': ffn_w_gu, 'ffn_w_down': ffn_w_down,
        'moe_router': moe_router, 'moe_w_gu': moe_w_gu, 'moe_w_down': moe_w_down,
    }
    dt = x_prompt.dtype
    bp = x_prompt.shape[0]
    fox_zero = [(jnp.zeros((bp, 0, FOX_HEADS, FOX_HEAD_DIM), dt),
                 jnp.zeros((bp, 0, FOX_HEADS, FOX_HEAD_DIM), dt),
                 jnp.zeros((bp, 0, FOX_HEADS), dt)) for _ in range(N_FOX)]
    lru_zero = [(jnp.zeros((bp, D_RNN), dt), jnp.zeros((bp, CONV_W - 1, D_RNN), dt)) for _ in range(N_LRU)]
    y_prompt, fox_p, lru_p = _trunk(x_prompt, c_prompt, fox_zero, lru_zero, p)
    fox_past = [(cache_k[j], cache_v[j], cache_logf[j]) for j in range(N_FOX)]
    lru_past = [(state_h[j], state_conv[j]) for j in range(N_LRU)]
    y_sample, fox_s, lru_s = _trunk(x_sample, c_sample, fox_past, lru_past, p)
    k_prompt = jnp.stack([s[0] for s in fox_p])
    v_prompt = jnp.stack([s[1] for s in fox_p])
    logf_prompt = jnp.stack([s[2] for s in fox_p])
    h_prompt = jnp.stack([s[0] for s in lru_p])
    conv_prompt = jnp.stack([s[1] for s in lru_p])
    k_sample = jnp.stack([s[0] for s in fox_s])
    v_sample = jnp.stack([s[1] for s in fox_s])
    logf_sample = jnp.stack([s[2] for s in fox_s])
    h_sample = jnp.stack([s[0] for s in lru_s])
    conv_sample = jnp.stack([s[1] for s in lru_s])
    return (y_prompt, y_sample, k_prompt, v_prompt, logf_prompt, h_prompt, conv_prompt,
            k_sample, v_sample, logf_sample, h_sample, conv_sample)
```

```python
import functools
import math

import jax
import jax.numpy as jnp
from jax import lax
from jax.experimental import pallas as pl
from jax.experimental.pallas import tpu as pltpu

F32 = jnp.float32
BF16 = jnp.bfloat16

D = 1024
H = 16
HD = 64
NPAIR = H // 2
LANES = 128
DFF = 3584
NEXP = 8
CONV_W = 4
LRU_C = 8.0
EPS = 1e-6
QK_SCALE = HD ** -0.5
NEG_INF = float("-inf")
VMEM_LIMIT = 56 * 1024 * 1024


def _cparams(n_axes):
    return pltpu.CompilerParams(dimension_semantics=("arbitrary",) * n_axes,
                                vmem_limit_bytes=VMEM_LIMIT)


def _dot(a, b):
    return jnp.dot(a, b, preferred_element_type=F32)


def _dot_nt(a, b):
    return lax.dot_general(a, b, (((1,), (1,)), ((), ())), preferred_element_type=F32)


def _norm_mod(x, g, sc, sh):
    ms = jnp.mean(x * x, axis=-1, keepdims=True)
    return (x * lax.rsqrt(ms + EPS) * g) * (1.0 + sc) + sh


def _log_sigmoid(x):
    return jnp.minimum(x, 0.0) - jnp.log1p(jnp.exp(-jnp.abs(x)))


def _sigmoid(x):
    return 1.0 / (1.0 + jnp.exp(-x))


def _silu(x):
    return x * _sigmoid(x)


def _gelu_tanh(x):
    return 0.5 * x * (1.0 + jnp.tanh(math.sqrt(2.0 / math.pi) * (x + 0.044715 * (x * x * x))))


def _split3(x):
    hi = x.astype(BF16)
    r = x - hi.astype(F32)
    mid = r.astype(BF16)
    lo = (r - mid.astype(F32)).astype(BF16)
    return hi, mid, lo


def _cumsum_rows(x, n):
    r = lax.broadcasted_iota(jnp.int32, (n, n), 0)
    c = lax.broadcasted_iota(jnp.int32, (n, n), 1)
    tri = (c <= r).astype(BF16)
    hi, mid, lo = _split3(x)
    return _dot(tri, hi) + _dot(tri, mid) + _dot(tri, lo)


def _cumsum_lanes(x, n):
    r = lax.broadcasted_iota(jnp.int32, (n, n), 0)
    c = lax.broadcasted_iota(jnp.int32, (n, n), 1)
    tri = (r <= c).astype(BF16)
    hi, mid, lo = _split3(x)
    return _dot(hi, tri) + _dot(mid, tri) + _dot(lo, tri)


def _mod_kernel(c_ref, w_ref, b_ref, o_ref):
    c = c_ref[...]
    s = _silu(c).astype(BF16)
    o_ref[0] = _dot(s, w_ref[0].astype(BF16)) + b_ref[0]


def _ada_mod(c_all, w_ada, b_ada):
    nl, _, nmod = w_ada.shape
    rows = c_all.shape[0]
    tn = 1536
    return pl.pallas_call(
        _mod_kernel,
        grid=(nl, nmod // tn),
        in_specs=[
            pl.BlockSpec((rows, D), lambda l, j: (0, 0)),
            pl.BlockSpec((1, D, tn), lambda l, j: (l, 0, j)),
            pl.BlockSpec((1, 1, tn), lambda l, j: (l, 0, j)),
        ],
        out_specs=pl.BlockSpec((1, rows, tn), lambda l, j: (l, 0, j)),
        out_shape=jax.ShapeDtypeStruct((nl, rows, nmod), F32),
        compiler_params=_cparams(2),
    )(c_all, w_ada, b_ada.reshape(nl, 1, nmod))


class _Rows:
    def __init__(self, rows, tm, groups, per_row_mod):
        self.rows, self.tm, self.groups = rows, tm, groups
        self.n_tiles = rows // tm
        self.tpg = self.n_tiles // groups
        self.per_row_mod = per_row_mod

    def mod_spec(self, which, n_axes=1):
        tpg, tm = self.tpg, self.tm
        if self.per_row_mod:
            if n_axes == 1:
                return pl.BlockSpec((None, None, tm, D), lambda i: (which, 0, i, 0))
            if n_axes == 2:
                return pl.BlockSpec((None, None, tm, D), lambda i, f: (which, 0, i, 0))
            return pl.BlockSpec((None, None, tm, D), lambda i, e, f: (which, 0, i, 0))
        if n_axes == 1:
            return pl.BlockSpec((None, None, 1, D), lambda i: (which, i // tpg, 0, 0))
        if n_axes == 2:
            return pl.BlockSpec((None, None, 1, D), lambda i, f: (which, i // tpg, 0, 0))
        return pl.BlockSpec((None, None, 1, D), lambda i, e, f: (which, i // tpg, 0, 0))


def _fox_in_kernel(x_ref, g_ref, sc_ref, sh_ref, wq_ref, wk_ref, wv_ref, wf_ref, bf_ref, *rest,
                   tm, tpg, aug):
    if aug:
        k_ref, v_ref, lf_ref, qa_ref, ka_ref, vp_ref, carry_ref = rest
    else:
        k_ref, v_ref, lf_ref, q_ref = rest
    h = _norm_mod(x_ref[...], g_ref[...], sc_ref[...], sh_ref[...]).astype(BF16)
    q = _dot(h, wq_ref[...]) * QK_SCALE
    k = _dot(h, wk_ref[...])
    v = _dot(h, wv_ref[...])
    k_ref[...] = k
    v_ref[...] = v
    lf = _log_sigmoid(_dot(h, wf_ref[...]) + bf_ref[...])
    lf_ref[...] = lf[:, :H]
    if not aug:
        q_ref[...] = q
        return

    i = pl.program_id(0)

    @pl.when(i % tpg == 0)
    def _():
        carry_ref[...] = jnp.zeros_like(carry_ref)

    cum = _cumsum_rows(lf, tm) + carry_ref[...]
    carry_ref[...] = cum[tm - 1:tm, :]
    hi, mid, lo = [p.astype(F32) for p in _split3(cum)]
    lane = lax.broadcasted_iota(jnp.int32, (tm, HD), 1)
    one = jnp.ones((tm, HD), F32)
    zero = jnp.zeros((tm, HD), F32)
    for hh in range(H):
        chi, cmid, clo = hi[:, hh:hh + 1], mid[:, hh:hh + 1], lo[:, hh:hh + 1]
        q_tail = jnp.where(lane == 0, chi, jnp.where(lane == 1, cmid, jnp.where(
            lane == 2, clo, jnp.where(lane < 6, one, zero))))
        k_tail = jnp.where(lane < 3, one, jnp.where(lane == 3, -chi, jnp.where(
            lane == 4, -cmid, jnp.where(lane == 5, -clo, zero))))
        qa_ref[0, hh, :, 0:HD] = q[:, hh * HD:(hh + 1) * HD].astype(BF16)
        qa_ref[0, hh, :, HD:2 * HD] = q_tail.astype(BF16)
        ka_ref[0, hh, :, 0:HD] = k[:, hh * HD:(hh + 1) * HD].astype(BF16)
        ka_ref[0, hh, :, HD:2 * HD] = k_tail.astype(BF16)
    vb = v.astype(BF16)
    for hp in range(NPAIR):
        vp_ref[0, hp] = vb[:, hp * LANES:(hp + 1) * LANES]


def _fox_in(x, g, mod, wq, wk, wv, wf, bf, rt, aug, seq):
    tm, tpg = rt.tm, rt.tpg
    full = lambda shape: pl.BlockSpec(shape, lambda i: (0,) * len(shape))
    row = pl.BlockSpec((tm, D), lambda i: (i, 0))
    in_specs = [row, full((1, D)), rt.mod_spec(1), rt.mod_spec(0),
                full((D, D)), full((D, D)), full((D, D)), full((D, LANES)), full((1, LANES))]
    out_specs = [row, row, pl.BlockSpec((tm, H), lambda i: (i, 0))]
    out_shape = [jax.ShapeDtypeStruct((rt.rows, D), F32), jax.ShapeDtypeStruct((rt.rows, D), F32),
                 jax.ShapeDtypeStruct((rt.rows, H), F32)]
    scratch = []
    if aug:
        b = rt.groups
        head_spec = pl.BlockSpec((1, H, tm, LANES), lambda i: (i // tpg, 0, i % tpg, 0))
        out_specs += [head_spec, head_spec,
                      pl.BlockSpec((1, NPAIR, tm, LANES), lambda i: (i // tpg, 0, i % tpg, 0))]
        out_shape += [jax.ShapeDtypeStruct((b, H, seq, LANES), BF16),
                      jax.ShapeDtypeStruct((b, H, seq, LANES), BF16),
                      jax.ShapeDtypeStruct((b, NPAIR, seq, LANES), BF16)]
        scratch = [pltpu.VMEM((1, LANES), F32)]
    else:
        out_specs += [row]
        out_shape += [jax.ShapeDtypeStruct((rt.rows, D), F32)]
    return pl.pallas_call(
        functools.partial(_fox_in_kernel, tm=tm, tpg=tpg, aug=aug),
        grid=(rt.n_tiles,),
        in_specs=in_specs, out_specs=out_specs, out_shape=out_shape,
        scratch_shapes=scratch,
        compiler_params=_cparams(1),
    )(x, g, mod, mod, wq, wk, wv, wf, bf)


def _attn_kernel(qa_ref, ka_ref, vp_ref, o_ref, m_s, l_s, acc_s, *, tq):
    qi = pl.program_id(2)
    outs = []
    for j in range(2):
        q = qa_ref[0, j]
        m_s[...] = jnp.full_like(m_s, NEG_INF)
        l_s[...] = jnp.zeros_like(l_s)
        acc_s[...] = jnp.zeros_like(acc_s)

        def step(kb, masked, j=j, q=q):
            off = pl.multiple_of(kb * tq, tq)
            k = ka_ref[0, j, pl.ds(off, tq), :]
            v = vp_ref[0, 0, pl.ds(off, tq), :]
            s = _dot_nt(q, k)
            if masked:
                r = lax.broadcasted_iota(jnp.int32, (tq, tq), 0)
                c = lax.broadcasted_iota(jnp.int32, (tq, tq), 1)
                s = jnp.where(c <= r, s, NEG_INF)
            m_old = m_s[...]
            m_new = jnp.maximum(m_old, jnp.max(s, axis=-1, keepdims=True))
            alpha = jnp.exp(m_old - m_new)
            p = jnp.exp(s - m_new)
            l_s[...] = alpha * l_s[...] + jnp.sum(p, axis=-1, keepdims=True)
            acc_s[...] = alpha * acc_s[...] + _dot(p.astype(BF16), v)
            m_s[...] = m_new

        def body(kb, carry):
            step(kb, False)
            return carry

        lax.fori_loop(0, qi, body, 0)
        step(qi, True)
        outs.append(acc_s[...] / l_s[...])
    lane = lax.broadcasted_iota(jnp.int32, (tq, LANES), 1)
    o_ref[0, 0] = jnp.where(lane < HD, outs[0], outs[1]).astype(BF16)


def _attn_prompt(qa, ka, vp, tq):
    b, _, seq, _ = qa.shape
    return pl.pallas_call(
        functools.partial(_attn_kernel, tq=tq),
        grid=(b, NPAIR, seq // tq),
        in_specs=[
            pl.BlockSpec((1, 2, tq, LANES), lambda bb, hp, qi: (bb, hp, qi, 0)),
            pl.BlockSpec((1, 2, seq, LANES), lambda bb, hp, qi: (bb, hp, 0, 0)),
            pl.BlockSpec((1, 1, seq, LANES), lambda bb, hp, qi: (bb, hp, 0, 0)),
        ],
        out_specs=pl.BlockSpec((1, 1, tq, LANES), lambda bb, hp, qi: (bb, hp, qi, 0)),
        out_shape=jax.ShapeDtypeStruct((b, NPAIR, seq, LANES), BF16),
        scratch_shapes=[pltpu.VMEM((tq, 1), F32), pltpu.VMEM((tq, 1), F32),
                        pltpu.VMEM((tq, LANES), F32)],
        compiler_params=_cparams(3),
    )(qa, ka, vp)


def _attn_sample_kernel(q_ref, kn_ref, vn_ref, lfn_ref, ck_ref, cv_ref, clf_ref, o_ref,
                        m_s, l_s, acc_s, carry_s, *, sq, tk, nkc):
    kc = pl.program_id(1)
    nrow = H * sq

    @pl.when(kc == 0)
    def _():
        m_s[...] = jnp.full_like(m_s, NEG_INF)
        l_s[...] = jnp.zeros_like(l_s)
        acc_s[...] = jnp.zeros_like(acc_s)
        carry_s[...] = jnp.zeros_like(carry_s)

    rowh = lax.broadcasted_iota(jnp.int32, (nrow, D), 0) // sq
    colh = lax.broadcasted_iota(jnp.int32, (nrow, D), 1) // HD
    q = q_ref[0]
    qbd = jnp.where(rowh == colh, jnp.concatenate([q] * H, axis=0), 0.0).astype(BF16)

    def expand_heads(ct):
        n = ct.shape[1]
        return jnp.concatenate([jnp.broadcast_to(ct[hh:hh + 1, :], (sq, n)) for hh in range(H)], axis=0)

    def update(s, vb):
        m_old = m_s[...]
        m_new = jnp.maximum(m_old, jnp.max(s, axis=-1, keepdims=True))
        alpha = jnp.exp(m_old - m_new)
        p = jnp.exp(s - m_new)
        l_s[...] = alpha * l_s[...] + jnp.sum(p, axis=-1, keepdims=True)
        acc_s[...] = alpha * acc_s[...] + _dot(p.astype(BF16), vb)
        m_s[...] = m_new

    sub = 256
    cums = []
    carry = carry_s[...]
    for t in range(tk // sub):
        cpart = _cumsum_lanes(clf_ref[0, :, t * sub:(t + 1) * sub], sub) + carry
        carry = cpart[:, sub - 1:sub]
        cums.append(cpart)
    carry_s[...] = carry
    c_keys = jnp.concatenate(cums, axis=1)
    s = _dot_nt(qbd, ck_ref[0].astype(BF16)) - expand_heads(c_keys)
    update(s, cv_ref[0].astype(BF16))

    @pl.when(kc == nkc - 1)
    def _():
        c_new = _cumsum_lanes(lfn_ref[0], LANES) + carry
        pad = jnp.zeros((LANES - sq, D), F32)
        kn = jnp.concatenate([kn_ref[0], pad], axis=0).astype(BF16)
        vn = jnp.concatenate([vn_ref[0], pad], axis=0).astype(BF16)
        c_rows = expand_heads(c_new)
        rq = lax.broadcasted_iota(jnp.int32, (nrow, LANES), 0) % sq
        kidx = lax.broadcasted_iota(jnp.int32, (nrow, LANES), 1)
        cq = jnp.sum(jnp.where(kidx == rq, c_rows, 0.0), axis=-1, keepdims=True)
        s_new = _dot_nt(qbd, kn) + (cq - c_rows)
        s_new = jnp.where(kidx <= rq, s_new, NEG_INF)
        m_s[...] = m_s[...] + cq
        update(s_new, vn)
        r = acc_s[...] / l_s[...]
        r = jnp.where(rowh == colh, r, 0.0)
        out = r[0:sq, :]
        for hh in range(1, H):
            out = out + r[hh * sq:(hh + 1) * sq, :]
        o_ref[0] = out


def _attn_sample(q, k_new, v_new, lf_new, cache_k, cache_v, cache_lf):
    nb, sq, _ = q.shape
    past = cache_k.shape[1]
    tk = 1024
    nkc = past // tk
    cur = pl.BlockSpec((1, sq, D), lambda b, kc: (b, 0, 0))
    return pl.pallas_call(
        functools.partial(_attn_sample_kernel, sq=sq, tk=tk, nkc=nkc),
        grid=(nb, nkc),
        in_specs=[cur, cur, cur,
                  pl.BlockSpec((1, H, LANES), lambda b, kc: (b, 0, 0)),
                  pl.BlockSpec((1, tk, D), lambda b, kc: (b, kc, 0)),
                  pl.BlockSpec((1, tk, D), lambda b, kc: (b, kc, 0)),
                  pl.BlockSpec((1, H, tk), lambda b, kc: (b, 0, kc))],
        out_specs=cur,
        out_shape=jax.ShapeDtypeStruct((nb, sq, D), F32),
        scratch_shapes=[pltpu.VMEM((H * sq, 1), F32), pltpu.VMEM((H * sq, 1), F32),
                        pltpu.VMEM((H * sq, D), F32), pltpu.VMEM((H, 1), F32)],
        compiler_params=_cparams(2),
    )(q, k_new, v_new, lf_new, cache_k, cache_v, cache_lf)


def _lru_kernel(x_ref, g_ref, sc_ref, sh_ref, wx_ref, wg_ref, cw_ref, cb_ref, wa_ref, ba_ref,
                wi_ref, bi_ref, lam_ref, h0_ref, c0_ref, y_ref, hl_ref, cn_ref, xp_s, hc_s,
                *, tm, tpg, stride, hdr):
    i = pl.program_id(0)
    nst = (CONV_W - 1) * stride

    @pl.when(i % tpg == 0)
    def _():
        xp_s[hdr - nst:hdr, :] = c0_ref[0]
        hc_s[...] = h0_ref[0]

    h = _norm_mod(x_ref[...], g_ref[...], sc_ref[...], sh_ref[...]).astype(BF16)
    xb = _dot(h, wx_ref[...])
    gb = _dot(h, wg_ref[...])
    xp_s[hdr:hdr + tm, :] = xb
    xc = cb_ref[...] + xb * cw_ref[CONV_W - 1:CONV_W, :]
    for j in range(CONV_W - 1):
        d = (CONV_W - 1 - j) * stride
        xc = xc + xp_s[hdr - d:hdr - d + tm, :] * cw_ref[j:j + 1, :]
    tail = xp_s[hdr + tm - nst:hdr + tm, :]
    cn_ref[0] = tail
    xp_s[hdr - nst:hdr, :] = tail

    xcb = xc.astype(BF16)
    r = _sigmoid(_dot(xcb, wa_ref[...]) + ba_ref[...])
    gi = _sigmoid(_dot(xcb, wi_ref[...]) + bi_ref[...])
    log_a = LRU_C * r * _log_sigmoid(lam_ref[...])
    a = jnp.exp(log_a)
    t = jnp.tanh(log_a)
    u = jnp.sqrt((-2.0 * t) / (1.0 - t)) * gi * xc

    row = lax.broadcasted_iota(jnp.int32, (tm, 1), 0)
    sft = stride
    while sft < tm:
        keep = row >= sft
        a_sh = pltpu.roll(a, sft, 0)
        u_sh = pltpu.roll(u, sft, 0)
        u = jnp.where(keep, a * u_sh + u, u)
        a = jnp.where(keep, a * a_sh, a)
        sft *= 2
    hc = hc_s[...]
    if stride == 1:
        hs = a * hc + u
    else:
        hs = a * jnp.concatenate([hc] * (tm // stride), axis=0) + u
    last = hs[tm - stride:tm, :]
    hc_s[...] = last
    hl_ref[0] = last
    y_ref[...] = (hs * _gelu_tanh(gb)).astype(BF16)


def _lru_mix(x, g, mod, wx, wg, cw, cb, wa, ba, wi, bi, lam, h0, c0, rt, stride):
    tm, tpg = rt.tm, rt.tpg
    hdr = -(-(CONV_W - 1) * stride // 8) * 8
    nst = (CONV_W - 1) * stride
    full = lambda shape: pl.BlockSpec(shape, lambda i: (0,) * len(shape))
    row = pl.BlockSpec((tm, D), lambda i: (i, 0))
    grp = rt.groups
    return pl.pallas_call(
        functools.partial(_lru_kernel, tm=tm, tpg=tpg, stride=stride, hdr=hdr),
        grid=(rt.n_tiles,),
        in_specs=[row, full((1, D)), rt.mod_spec(1), rt.mod_spec(0),
                  full((D, D)), full((D, D)), full((CONV_W, D)), full((1, D)),
                  full((D, D)), full((1, D)), full((D, D)), full((1, D)), full((1, D)),
                  pl.BlockSpec((1, stride, D), lambda i: (i // tpg, 0, 0)),
                  pl.BlockSpec((1, nst, D), lambda i: (i // tpg, 0, 0))],
        out_specs=[row,
                   pl.BlockSpec((1, stride, D), lambda i: (i // tpg, 0, 0)),
                   pl.BlockSpec((1, nst, D), lambda i: (i // tpg, 0, 0))],
        out_shape=[jax.ShapeDtypeStruct((rt.rows, D), BF16),
                   jax.ShapeDtypeStruct((grp, stride, D), F32),
                   jax.ShapeDtypeStruct((grp, nst, D), F32)],
        scratch_shapes=[pltpu.VMEM((hdr + tm, D), F32), pltpu.VMEM((stride, D), F32)],
        compiler_params=_cparams(1),
    )(x, g, mod, mod, wx, wg, cw, cb, wa, ba, wi, bi, lam, h0, c0)


def _mix_in(a_ref, pair_major):
    if pair_major:
        return jnp.concatenate([a_ref[0, hp] for hp in range(NPAIR)], axis=1)
    return a_ref[...]


def _ffn_kernel(x_ref, a_ref, ga_ref, wo_ref, g2_ref, sc_ref, sh_ref, gm_ref, wg_ref, wu_ref, wd_ref,
                o_ref, x1_s, h_s, acc_s, *, nf, pair_major):
    f = pl.program_id(1)

    @pl.when(f == 0)
    def _():
        a = _mix_in(a_ref, pair_major).astype(BF16)
        x1 = x_ref[...] + ga_ref[...] * _dot(a, wo_ref[...])
        x1_s[...] = x1
        h_s[...] = _norm_mod(x1, g2_ref[...], sc_ref[...], sh_ref[...]).astype(BF16)
        acc_s[...] = jnp.zeros_like(acc_s)

    h = h_s[...]
    act = (_silu(_dot(h, wg_ref[...])) * _dot(h, wu_ref[...])).astype(BF16)
    acc_s[...] += _dot(act, wd_ref[...])

    @pl.when(f == nf - 1)
    def _():
        o_ref[...] = x1_s[...] + gm_ref[...] * acc_s[...]


def _moe_kernel(x_ref, a_ref, ga_ref, wo_ref, g2_ref, sc_ref, sh_ref, gm_ref, wr_ref, wg_ref, wu_ref,
                wd_ref, o_ref, x1_s, h_s, acc_s, gate_s, *, nf, tm):
    e = pl.program_id(1)
    f = pl.program_id(2)
    lane = lax.broadcasted_iota(jnp.int32, (tm, LANES), 1)

    @pl.when((e == 0) & (f == 0))
    def _():
        x1 = x_ref[...] + ga_ref[...] * _dot(a_ref[...].astype(BF16), wo_ref[...])
        x1_s[...] = x1
        hf = _norm_mod(x1, g2_ref[...], sc_ref[...], sh_ref[...])
        h_s[...] = hf.astype(BF16)
        acc_s[...] = jnp.zeros_like(acc_s)
        logits = jnp.dot(hf, wr_ref[...], preferred_element_type=F32, precision=lax.Precision.HIGHEST)
        logits = jnp.where(lane < NEXP, logits, NEG_INF)
        m1 = jnp.max(logits, axis=-1, keepdims=True)
        i1 = jnp.min(jnp.where(logits == m1, lane, LANES), axis=-1, keepdims=True)
        rest = jnp.where(lane == i1, NEG_INF, logits)
        m2 = jnp.max(rest, axis=-1, keepdims=True)
        i2 = jnp.min(jnp.where(rest == m2, lane, LANES), axis=-1, keepdims=True)
        t = jnp.exp(m2 - m1)
        gate_s[...] = (jnp.where(lane == i1, 1.0 / (1.0 + t), 0.0)
                       + jnp.where(lane == i2, t / (1.0 + t), 0.0))

    h = h_s[...]
    act = (_silu(_dot(h, wg_ref[0])) * _dot(h, wu_ref[0])).astype(BF16)
    ge = jnp.sum(jnp.where(lane == e, gate_s[...], 0.0), axis=-1, keepdims=True)
    acc_s[...] += ge * _dot(act, wd_ref[0])

    @pl.when((e == NEXP - 1) & (f == nf - 1))
    def _():
        o_ref[...] = x1_s[...] + gm_ref[...] * acc_s[...]


def _post_ffn(x, a, mod, wo, g2, w_gu, w_down, rt, pair_major, seq):
    tm, tpg = rt.tm, rt.tpg
    tf = 512
    nf = DFF // tf
    row = pl.BlockSpec((tm, D), lambda i, f: (i, 0))
    if pair_major:
        a_spec = pl.BlockSpec((1, NPAIR, tm, LANES), lambda i, f: (i // tpg, 0, i % tpg, 0))
    else:
        a_spec = row
    c2 = lambda shape: pl.BlockSpec(shape, lambda i, f: (0,) * len(shape))
    return pl.pallas_call(
        functools.partial(_ffn_kernel, nf=nf, pair_major=pair_major),
        grid=(rt.n_tiles, nf),
        in_specs=[row, a_spec, rt.mod_spec(2, 2), c2((D, D)), c2((1, D)),
                  rt.mod_spec(4, 2), rt.mod_spec(3, 2), rt.mod_spec(5, 2),
                  pl.BlockSpec((D, tf), lambda i, f: (0, f)),
                  pl.BlockSpec((D, tf), lambda i, f: (0, nf + f)),
                  pl.BlockSpec((tf, D), lambda i, f: (f, 0))],
        out_specs=row,
        out_shape=jax.ShapeDtypeStruct((rt.rows, D), F32),
        scratch_shapes=[pltpu.VMEM((tm, D), F32), pltpu.VMEM((tm, D), BF16), pltpu.VMEM((tm, D), F32)],
        compiler_params=_cparams(2),
    )(x, a, mod, wo, g2, mod, mod, mod, w_gu, w_gu, w_down)


def _post_moe(x, a, mod, wo, g2, w_router, w_gu, w_down, rt):
    tm = rt.tm
    tf = 512
    nf = DFF // tf
    row = pl.BlockSpec((tm, D), lambda i, e, f: (i, 0))
    c3 = lambda shape: pl.BlockSpec(shape, lambda i, e, f: (0,) * len(shape))
    return pl.pallas_call(
        functools.partial(_moe_kernel, nf=nf, tm=tm),
        grid=(rt.n_tiles, NEXP, nf),
        in_specs=[row, row, rt.mod_spec(2, 3), c3((D, D)), c3((1, D)),
                  rt.mod_spec(4, 3), rt.mod_spec(3, 3), rt.mod_spec(5, 3),
                  c3((D, LANES)),
                  pl.BlockSpec((1, D, tf), lambda i, e, f: (e, 0, f)),
                  pl.BlockSpec((1, D, tf), lambda i, e, f: (e, 0, nf + f)),
                  pl.BlockSpec((1, tf, D), lambda i, e, f: (e, f, 0))],
        out_specs=row,
        out_shape=jax.ShapeDtypeStruct((rt.rows, D), F32),
        scratch_shapes=[pltpu.VMEM((tm, D), F32), pltpu.VMEM((tm, D), BF16), pltpu.VMEM((tm, D), F32),
                        pltpu.VMEM((tm, LANES), F32)],
        compiler_params=_cparams(3),
    )(x, a, mod, wo, g2, mod, mod, mod, w_router, w_gu, w_gu, w_down)


def _final_norm_kernel(x_ref, g_ref, o_ref):
    x = x_ref[...]
    ms = jnp.mean(x * x, axis=-1, keepdims=True)
    o_ref[...] = x * lax.rsqrt(ms + EPS) * g_ref[...]


def _final_norm(x, g, tm):
    rows = x.shape[0]
    return pl.pallas_call(
        _final_norm_kernel,
        grid=(rows // tm,),
        in_specs=[pl.BlockSpec((tm, D), lambda i: (i, 0)), pl.BlockSpec((1, D), lambda i: (0, 0))],
        out_specs=pl.BlockSpec((tm, D), lambda i: (i, 0)),
        out_shape=jax.ShapeDtypeStruct((rows, D), F32),
        compiler_params=_cparams(1),
    )(x, g)


def _prep_weights(p):
    w = {}
    fw = p['fox_w_in']
    w['wq'] = fw[:, :, 0 * D:1 * D].astype(BF16)
    w['wk'] = fw[:, :, 1 * D:2 * D].astype(BF16)
    w['wv'] = fw[:, :, 2 * D:3 * D].astype(BF16)
    w['wf'] = jnp.pad(fw[:, :, 3 * D:], ((0, 0), (0, 0), (0, LANES - H))).astype(BF16)
    w['bf'] = jnp.pad(p['fox_b_f'], ((0, 0), (0, LANES - H)))[:, None, :]
    w['fox_wo'] = p['fox_w_out'].astype(BF16)
    w['lru_wx'] = p['lru_w_in'][:, :, :D].astype(BF16)
    w['lru_wg'] = p['lru_w_in'][:, :, D:].astype(BF16)
    nblk = p['lru_w_a'].shape[1]
    eye = jnp.eye(nblk, dtype=F32)
    bd = lambda m: jnp.einsum('lnkj,nm->lnkmj', m, eye).reshape(m.shape[0], D, D).astype(BF16)
    w['lru_wa'] = bd(p['lru_w_a'])
    w['lru_wi'] = bd(p['lru_w_i'])
    w['lru_wo'] = p['lru_w_out'].astype(BF16)
    w['ffn_gu'] = p['ffn_w_gu'].astype(BF16)
    w['ffn_down'] = p['ffn_w_down'].astype(BF16)
    w['router'] = jnp.pad(p['moe_router'], ((0, 0), (0, 0), (0, LANES - NEXP)))
    w['moe_gu'] = p['moe_w_gu'].astype(BF16)
    w['moe_down'] = p['moe_w_down'].astype(BF16)
    return w


def _trunk(x, mods, p, w, rt, rt_ffn, prompt, seq, fox_past, lru_past):
    depth = p['norm_mix_g'].shape[0]
    fox_new, lru_new = [], []
    for l in range(depth):
        j = l // 2
        mod = mods[l]
        g1 = p['norm_mix_g'][l][None, :]
        g2 = p['norm_ffn_g'][l][None, :]
        if l % 2 == 0:
            outs = _fox_in(x, g1, mod, w['wq'][j], w['wk'][j], w['wv'][j], w['wf'][j], w['bf'][j],
                           rt, aug=prompt, seq=seq)
            if prompt:
                k, v, lf, qa, ka, vp = outs
                a = _attn_prompt(qa, ka, vp, tq=512)
                fox_new.append((k, v, lf))
                x = _post_ffn(x, a, mod, w['fox_wo'][j], g2, w['ffn_gu'][j], w['ffn_down'][j], rt_ffn,
                              pair_major=True, seq=seq)
            else:
                k, v, lf, q = outs
                nb = rt.rows // seq
                to_b = lambda t: t.reshape(seq, nb, -1).transpose(1, 0, 2)
                kb, vb, lfb = to_b(k), to_b(v), to_b(lf)
                ck, cv, clf = fox_past[j]
                lfn_t = jnp.pad(lfb.transpose(0, 2, 1), ((0, 0), (0, 0), (0, LANES - seq)))
                ab = _attn_sample(to_b(q), kb, vb, lfn_t, ck, cv, clf.transpose(0, 2, 1))
                a = ab.transpose(1, 0, 2).reshape(rt.rows, D)
                fox_new.append((kb, vb, lfb))
                x = _post_ffn(x, a, mod, w['fox_wo'][j], g2, w['ffn_gu'][j], w['ffn_down'][j], rt_ffn,
                              pair_major=False, seq=seq)
        else:
            h0, c0, stride = lru_past[j]
            y, hl, cn = _lru_mix(x, g1, mod, w['lru_wx'][j], w['lru_wg'][j], p['lru_conv_w'][j],
                                 p['lru_conv_b'][j][None, :], w['lru_wa'][j], p['lru_b_a'][j][None, :],
                                 w['lru_wi'][j], p['lru_b_i'][j][None, :], p['lru_lam'][j][None, :],
                                 h0, c0, rt, stride)
            lru_new.append((hl, cn))
            x = _post_moe(x, y, mod, w['lru_wo'][j], g2, w['router'][j], w['moe_gu'][j],
                          w['moe_down'][j], rt_ffn)
    return _final_norm(x, p['final_g'][None, :], rt.tm), fox_new, lru_new


def kernel(x_prompt, x_sample, c_prompt, c_sample, cache_k, cache_v, cache_logf, state_h, state_conv,
           norm_mix_g, norm_ffn_g, final_g, w_ada, b_ada, fox_w_in, fox_b_f, fox_w_out,
           lru_w_in, lru_conv_w, lru_conv_b, lru_w_a, lru_b_a, lru_w_i, lru_b_i, lru_lam, lru_w_out,
           ffn_w_gu, ffn_w_down, moe_router, moe_w_gu, moe_w_down):
    p = {
        'norm_mix_g': norm_mix_g, 'norm_ffn_g': norm_ffn_g, 'final_g': final_g,
        'fox_w_in': fox_w_in, 'fox_b_f': fox_b_f, 'fox_w_out': fox_w_out,
        'lru_w_in': lru_w_in, 'lru_conv_w': lru_conv_w, 'lru_conv_b': lru_conv_b,
        'lru_w_a': lru_w_a, 'lru_b_a': lru_b_a, 'lru_w_i': lru_w_i, 'lru_b_i': lru_b_i,
        'lru_lam': lru_lam, 'lru_w_out': lru_w_out,
        'ffn_w_gu': ffn_w_gu, 'ffn_w_down': ffn_w_down,
        'moe_router': moe_router, 'moe_w_gu': moe_w_gu, 'moe_w_down': moe_w_down,
    }
    bp, sp, _ = x_prompt.shape
    bs, ss, _ = x_sample.shape
    depth = w_ada.shape[0]
    n_fox, n_lru = cache_k.shape[0], state_h.shape[0]
    past = cache_k.shape[2]
    w = _prep_weights(p)

    nc = bp + bs
    ncp = -(-nc // 8) * 8
    c_all = jnp.pad(jnp.concatenate([c_prompt, c_sample], axis=0), ((0, ncp - nc), (0, 0)))
    mod_all = _ada_mod(c_all, w_ada, b_ada)
    mods_p = [mod_all[l, :bp].reshape(bp, 6, 1, D).transpose(1, 0, 2, 3) for l in range(depth)]
    mods_s = [jnp.tile(mod_all[l, bp:nc].reshape(bs, 6, D).transpose(1, 0, 2), (1, ss, 1))[:, None]
              for l in range(depth)]

    rt_p = _Rows(bp * sp, 256, bp, per_row_mod=False)
    rt_p_ffn = _Rows(bp * sp, 1024, bp, per_row_mod=False)
    zeros_h = jnp.zeros((bp, 1, D), F32)
    zeros_c = jnp.zeros((bp, CONV_W - 1, D), F32)
    y_p, fox_p, lru_p = _trunk(x_prompt.reshape(bp * sp, D), mods_p, p, w, rt_p, rt_p_ffn, True, sp,
                               None, [(zeros_h, zeros_c, 1)] * n_lru)

    rt_s = _Rows(bs * ss, bs * ss, 1, per_row_mod=True)
    xs = x_sample.transpose(1, 0, 2).reshape(ss * bs, D)
    fox_past = [(cache_k[j].reshape(bs, past, D), cache_v[j].reshape(bs, past, D), cache_logf[j])
                for j in range(n_fox)]
    lru_past = [(state_h[j][None], state_conv[j].transpose(1, 0, 2).reshape(1, (CONV_W - 1) * bs, D), bs)
                for j in range(n_lru)]
    y_s, fox_s, lru_s = _trunk(xs, mods_s, p, w, rt_s, rt_s, False, ss, fox_past, lru_past)

    y_prompt = y_p.reshape(bp, sp, D)
    y_sample = y_s.reshape(ss, bs, D).transpose(1, 0, 2)
    k_prompt = jnp.stack([f[0] for f in fox_p]).reshape(n_fox, bp, sp, H, HD)
    v_prompt = jnp.stack([f[1] for f in fox_p]).reshape(n_fox, bp, sp, H, HD)
    logf_prompt = jnp.stack([f[2] for f in fox_p]).reshape(n_fox, bp, sp, H)
    h_prompt = jnp.stack([s[0][:, 0, :] for s in lru_p])
    conv_prompt = jnp.stack([s[1] for s in lru_p])
    k_sample = jnp.stack([f[0] for f in fox_s]).reshape(n_fox, bs, ss, H, HD)
    v_sample = jnp.stack([f[1] for f in fox_s]).reshape(n_fox, bs, ss, H, HD)
    logf_sample = jnp.stack([f[2] for f in fox_s])
    h_sample = jnp.stack([s[0][0] for s in lru_s])
    conv_sample = jnp.stack([s[1][0].reshape(CONV_W - 1, bs, D).transpose(1, 0, 2) for s in lru_s])
    return (y_prompt, y_sample, k_prompt, v_prompt, logf_prompt, h_prompt, conv_prompt,
            k_sample, v_sample, logf_sample, h_sample, conv_sample)
```

```python
import functools
import math

import jax
import jax.numpy as jnp
from jax import lax
from jax.experimental import pallas as pl
from jax.experimental.pallas import tpu as pltpu

F32 = jnp.float32
BF16 = jnp.bfloat16

D = 1024
H = 16
HD = 64
NPAIR = H // 2
LANES = 128
DFF = 3584
NEXP = 8
CONV_W = 4
LRU_C = 8.0
EPS = 1e-6
QK_SCALE = HD ** -0.5
NEG_INF = float("-inf")
VMEM_LIMIT = 56 * 1024 * 1024
TM_MIX = 256
TM_FFN = 1024
TM_MOE = 512
TF = 512
TG = 512


def _cparams(n_axes):
    return pltpu.CompilerParams(dimension_semantics=("arbitrary",) * n_axes,
                                vmem_limit_bytes=VMEM_LIMIT)


def _dot(a, b):
    return jnp.dot(a, b, preferred_element_type=F32)


def _dot_nt(a, b):
    return lax.dot_general(a, b, (((1,), (1,)), ((), ())), preferred_element_type=F32)


def _norm_mod(x, g, sc, sh):
    ms = jnp.mean(x * x, axis=-1, keepdims=True)
    return (x * lax.rsqrt(ms + EPS) * g) * (1.0 + sc) + sh


def _log_sigmoid(x):
    return jnp.minimum(x, 0.0) - jnp.log1p(jnp.exp(-jnp.abs(x)))


def _sigmoid(x):
    return 1.0 / (1.0 + jnp.exp(-x))


def _silu(x):
    return x * _sigmoid(x)


def _gelu_tanh(x):
    return 0.5 * x * (1.0 + jnp.tanh(math.sqrt(2.0 / math.pi) * (x + 0.044715 * (x * x * x))))


def _split3(x):
    hi = x.astype(BF16)
    r = x - hi.astype(F32)
    mid = r.astype(BF16)
    lo = (r - mid.astype(F32)).astype(BF16)
    return hi, mid, lo


def _cumsum_rows(x, n):
    r = lax.broadcasted_iota(jnp.int32, (n, n), 0)
    c = lax.broadcasted_iota(jnp.int32, (n, n), 1)
    tri = (c <= r).astype(BF16)
    hi, mid, lo = _split3(x)
    return _dot(tri, hi) + _dot(tri, mid) + _dot(tri, lo)


def _cumsum_lanes(x, n):
    r = lax.broadcasted_iota(jnp.int32, (n, n), 0)
    c = lax.broadcasted_iota(jnp.int32, (n, n), 1)
    tri = (r <= c).astype(BF16)
    hi, mid, lo = _split3(x)
    return _dot(hi, tri) + _dot(mid, tri) + _dot(lo, tri)


def _mod_kernel(c_ref, w_ref, b_ref, o_ref):
    c = c_ref[...]
    s = _silu(c).astype(BF16)
    o_ref[0] = _dot(s, w_ref[0].astype(BF16)) + b_ref[0]


def _ada_mod(c_all, w_ada, b_ada):
    nl, _, nmod = w_ada.shape
    rows = c_all.shape[0]
    tn = 1536
    return pl.pallas_call(
        _mod_kernel,
        grid=(nl, nmod // tn),
        in_specs=[
            pl.BlockSpec((rows, D), lambda l, j: (0, 0)),
            pl.BlockSpec((1, D, tn), lambda l, j: (l, 0, j)),
            pl.BlockSpec((1, 1, tn), lambda l, j: (l, 0, j)),
        ],
        out_specs=pl.BlockSpec((1, rows, tn), lambda l, j: (l, 0, j)),
        out_shape=jax.ShapeDtypeStruct((nl, rows, nmod), F32),
        compiler_params=_cparams(2),
    )(c_all, w_ada, b_ada.reshape(nl, 1, nmod))


class _Rows:
    def __init__(self, rows, tm, groups, per_row_mod):
        self.rows, self.tm, self.groups = rows, tm, groups
        self.n_tiles = rows // tm
        self.tpg = self.n_tiles // groups
        self.per_row_mod = per_row_mod

    def mod_spec(self, which):
        tpg, tm = self.tpg, self.tm
        if self.per_row_mod:
            return pl.BlockSpec((None, None, tm, D), lambda i, *_: (which, 0, i, 0))
        return pl.BlockSpec((None, None, 1, D), lambda i, *_: (which, i // tpg, 0, 0))


def _fox_in_sample_kernel(x_ref, g_ref, sc_ref, sh_ref, wq_ref, wk_ref, wv_ref, wf_ref, bf_ref,
                          k_ref, v_ref, lf_ref, q_ref):
    h = _norm_mod(x_ref[...], g_ref[...], sc_ref[...], sh_ref[...]).astype(BF16)
    q_ref[...] = _dot(h, wq_ref[...]) * QK_SCALE
    k_ref[...] = _dot(h, wk_ref[...])
    v_ref[...] = _dot(h, wv_ref[...])
    lf = _log_sigmoid(_dot(h, wf_ref[...]) + bf_ref[...])
    lf_ref[...] = lf[:, :H]


def _fox_in_sample(x, g, mod, wq, wk, wv, wf, bf, rt):
    tm = rt.tm
    full = lambda shape: pl.BlockSpec(shape, lambda i: (0,) * len(shape))
    row = pl.BlockSpec((tm, D), lambda i: (i, 0))
    rows_f32 = jax.ShapeDtypeStruct((rt.rows, D), F32)
    return pl.pallas_call(
        _fox_in_sample_kernel,
        grid=(rt.n_tiles,),
        in_specs=[row, full((1, D)), rt.mod_spec(1), rt.mod_spec(0),
                  full((D, D)), full((D, D)), full((D, D)), full((D, LANES)), full((1, LANES))],
        out_specs=[row, row, pl.BlockSpec((tm, H), lambda i: (i, 0)), row],
        out_shape=[rows_f32, rows_f32, jax.ShapeDtypeStruct((rt.rows, H), F32), rows_f32],
        compiler_params=_cparams(1),
    )(x, g, mod, mod, wq, wk, wv, wf, bf)


def _fox_in_prompt_kernel(x_ref, g_ref, sc_ref, sh_ref, wq_ref, wkt_ref, wvt_ref, wv_ref, wf_ref, bf_ref,
                          wft_ref, bft_ref, *rest, tm, tpg, n_prev):
    kt_ref, vt_ref, lft_ref, qa_ref, kat_ref, vp_ref, carry_r, carry_c = rest[n_prev:]
    i = pl.program_id(0)

    @pl.when(i % tpg == 0)
    def _():
        carry_r[...] = jnp.zeros_like(carry_r)
        carry_c[...] = jnp.zeros_like(carry_c)

    h = _norm_mod(x_ref[...], g_ref[...], sc_ref[...], sh_ref[...]).astype(BF16)
    q = _dot(h, wq_ref[...]) * QK_SCALE
    kt = _dot_nt(wkt_ref[...], h)
    kt_ref[0, 0] = kt
    vt_ref[0, 0] = _dot_nt(wvt_ref[...], h)
    vb = _dot(h, wv_ref[...]).astype(BF16)
    for hp in range(NPAIR):
        vp_ref[0, hp] = vb[:, hp * LANES:(hp + 1) * LANES]

    lf = _log_sigmoid(_dot(h, wf_ref[...]) + bf_ref[...])
    lft = _log_sigmoid(_dot_nt(wft_ref[...], h) + bft_ref[...])
    lft_ref[0, 0] = lft
    cum = _cumsum_rows(lf, tm) + carry_r[...]
    carry_r[...] = cum[tm - 1:tm, :]
    cumt = _cumsum_lanes(lft, tm) + carry_c[...]
    carry_c[...] = cumt[:, tm - 1:tm]
    hi, mid, lo = [p.astype(F32) for p in _split3(cum)]
    hit, midt, lot = [p.astype(F32) for p in _split3(cumt)]
    lane = lax.broadcasted_iota(jnp.int32, (tm, HD), 1)
    srow = lax.broadcasted_iota(jnp.int32, (HD, tm), 0)
    for hh in range(H):
        chi, cmid, clo = hi[:, hh:hh + 1], mid[:, hh:hh + 1], lo[:, hh:hh + 1]
        q_tail = jnp.where(lane == 0, chi, jnp.where(lane == 1, cmid, jnp.where(
            lane == 2, clo, jnp.where(lane < 6, 1.0, 0.0))))
        khi, kmid, klo = hit[hh:hh + 1, :], midt[hh:hh + 1, :], lot[hh:hh + 1, :]
        k_tail = jnp.where(srow < 3, 1.0, jnp.where(srow == 3, -khi, jnp.where(
            srow == 4, -kmid, jnp.where(srow == 5, -klo, 0.0))))
        qa_ref[0, hh, :, 0:HD] = q[:, hh * HD:(hh + 1) * HD].astype(BF16)
        qa_ref[0, hh, :, HD:2 * HD] = q_tail.astype(BF16)
        kat_ref[0, hh, 0, 0:HD, :] = kt[hh * HD:(hh + 1) * HD, :].astype(BF16)
        kat_ref[0, hh, 0, HD:2 * HD, :] = k_tail.astype(BF16)


def _fox_in_prompt(x, g, mod, wq, wkt, wvt, wv, wf, bf, wft, bft, rt, seq, layer, n_layers, prev):
    tm, tpg, b = rt.tm, rt.tpg, rt.groups
    full = lambda shape: pl.BlockSpec(shape, lambda i: (0,) * len(shape))
    row = pl.BlockSpec((tm, D), lambda i: (i, 0))
    if prev is None:
        prev = (jnp.zeros((n_layers, b, D, seq), F32), jnp.zeros((n_layers, b, D, seq), F32),
                jnp.zeros((n_layers, b, H, seq), F32))
    prev = tuple(prev)
    n_in = 12
    in_specs = [row, full((1, D)), rt.mod_spec(1), rt.mod_spec(0),
                full((D, D)), full((D, D)), full((D, D)), full((D, D)), full((D, LANES)),
                full((1, LANES)), full((H, D)), full((H, 1))]
    in_specs += [pl.BlockSpec(memory_space=pl.ANY)] * len(prev)
    state_spec = lambda rows: pl.BlockSpec((1, 1, rows, tm), lambda i: (layer, i // tpg, 0, i % tpg))
    out_specs = [state_spec(D), state_spec(D), state_spec(H),
                 pl.BlockSpec((1, H, tm, LANES), lambda i: (i // tpg, 0, i % tpg, 0)),
                 pl.BlockSpec((1, H, 1, LANES, tm), lambda i: (i // tpg, 0, i % tpg, 0, 0)),
                 pl.BlockSpec((1, NPAIR, tm, LANES), lambda i: (i // tpg, 0, i % tpg, 0))]
    out_shape = [jax.ShapeDtypeStruct((n_layers, b, D, seq), F32),
                 jax.ShapeDtypeStruct((n_layers, b, D, seq), F32),
                 jax.ShapeDtypeStruct((n_layers, b, H, seq), F32),
                 jax.ShapeDtypeStruct((b, H, seq, LANES), BF16),
                 jax.ShapeDtypeStruct((b, H, seq // tm, LANES, tm), BF16),
                 jax.ShapeDtypeStruct((b, NPAIR, seq, LANES), BF16)]
    return pl.pallas_call(
        functools.partial(_fox_in_prompt_kernel, tm=tm, tpg=tpg, n_prev=len(prev)),
        grid=(rt.n_tiles,),
        in_specs=in_specs, out_specs=out_specs, out_shape=out_shape,
        scratch_shapes=[pltpu.VMEM((1, LANES), F32), pltpu.VMEM((H, 1), F32)],
        input_output_aliases={n_in + t: t for t in range(len(prev))},
        compiler_params=_cparams(1),
    )(x, g, mod, mod, wq, wkt, wvt, wv, wf, bf, wft, bft, *prev)


def _attn_kernel(qa_ref, kat_ref, vp_ref, o_ref, m_s, l_s, acc_s, *, tq, tk):
    qi = pl.program_id(2)
    nsub = tq // tk
    m_s[...] = jnp.full_like(m_s, NEG_INF)
    l_s[...] = jnp.zeros_like(l_s)
    acc_s[...] = jnp.zeros_like(acc_s)
    q = [qa_ref[0, 0], qa_ref[0, 1]]

    def step(kblk, j, mask_off):
        kt = kat_ref[0, j, kblk]
        v = vp_ref[0, 0, pl.ds(pl.multiple_of(kblk * tk, tk), tk), :]
        s = _dot(q[j], kt)
        if mask_off is not None:
            r = lax.broadcasted_iota(jnp.int32, (tq, tk), 0)
            c = lax.broadcasted_iota(jnp.int32, (tq, tk), 1) + mask_off
            s = jnp.where(c <= r, s, NEG_INF)
        m_old = m_s[j]
        m_new = jnp.maximum(m_old, jnp.max(s, axis=-1, keepdims=True))
        alpha = jnp.exp(m_old - m_new)
        p = jnp.exp(s - jnp.concatenate([m_new] * (tk // LANES), axis=1))
        psum = p[:, 0:LANES]
        for t in range(1, tk // LANES):
            psum = psum + p[:, t * LANES:(t + 1) * LANES]
        l_s[j] = alpha * l_s[j] + psum
        acc_s[j] = alpha * acc_s[j] + _dot(p.astype(BF16), v)
        m_s[j] = m_new

    def body(kb, carry):
        for sub in range(nsub):
            for j in range(2):
                step(kb * nsub + sub, j, None)
        return carry

    lax.fori_loop(0, qi, body, 0)
    for sub in range(nsub):
        for j in range(2):
            step(qi * nsub + sub, j, sub * tk)
    outs = [acc_s[j] / jnp.sum(l_s[j], axis=-1, keepdims=True) for j in range(2)]
    lane = lax.broadcasted_iota(jnp.int32, (tq, LANES), 1)
    o_ref[0, 0] = jnp.where(lane < HD, outs[0], outs[1]).astype(BF16)


def _attn_prompt(qa, kat, vp, tq):
    b, _, seq, _ = qa.shape
    nkb, tk = kat.shape[2], kat.shape[4]
    return pl.pallas_call(
        functools.partial(_attn_kernel, tq=tq, tk=tk),
        grid=(b, NPAIR, seq // tq),
        in_specs=[
            pl.BlockSpec((1, 2, tq, LANES), lambda bb, hp, qi: (bb, hp, qi, 0)),
            pl.BlockSpec((1, 2, nkb, LANES, tk), lambda bb, hp, qi: (bb, hp, 0, 0, 0)),
            pl.BlockSpec((1, 1, seq, LANES), lambda bb, hp, qi: (bb, hp, 0, 0)),
        ],
        out_specs=pl.BlockSpec((1, 1, tq, LANES), lambda bb, hp, qi: (bb, hp, qi, 0)),
        out_shape=jax.ShapeDtypeStruct((b, NPAIR, seq, LANES), BF16),
        scratch_shapes=[pltpu.VMEM((2, tq, LANES), F32), pltpu.VMEM((2, tq, LANES), F32),
                        pltpu.VMEM((2, tq, LANES), F32)],
        compiler_params=_cparams(3),
    )(qa, kat, vp)


def _attn_sample_kernel(q_ref, kn_ref, vn_ref, lfn_ref, ck_ref, cv_ref, clf_ref, o_ref,
                        m_s, l_s, acc_s, carry_s, *, sq, tk, nkc):
    kc = pl.program_id(1)
    nrow = H * sq

    @pl.when(kc == 0)
    def _():
        m_s[...] = jnp.full_like(m_s, NEG_INF)
        l_s[...] = jnp.zeros_like(l_s)
        acc_s[...] = jnp.zeros_like(acc_s)
        carry_s[...] = jnp.zeros_like(carry_s)

    rowh = lax.broadcasted_iota(jnp.int32, (nrow, D), 0) // sq
    colh = lax.broadcasted_iota(jnp.int32, (nrow, D), 1) // HD
    q = q_ref[0]
    qbd = jnp.where(rowh == colh, jnp.concatenate([q] * H, axis=0), 0.0).astype(BF16)

    def expand_heads(ct):
        n = ct.shape[1]
        return jnp.concatenate([jnp.broadcast_to(ct[hh:hh + 1, :], (sq, n)) for hh in range(H)], axis=0)

    def update(s, vb, v_transposed):
        m_old = m_s[...]
        m_new = jnp.maximum(m_old, jnp.max(s, axis=-1, keepdims=True))
        alpha = jnp.exp(m_old - m_new)
        p = jnp.exp(s - m_new)
        l_s[...] = alpha * l_s[...] + jnp.sum(p, axis=-1, keepdims=True)
        pb = p.astype(BF16)
        pv = _dot_nt(pb, vb) if v_transposed else _dot(pb, vb)
        acc_s[...] = alpha * acc_s[...] + pv
        m_s[...] = m_new

    sub = 256
    cums = []
    carry = carry_s[...]
    for t in range(tk // sub):
        cpart = _cumsum_lanes(clf_ref[0, :, t * sub:(t + 1) * sub], sub) + carry
        carry = cpart[:, sub - 1:sub]
        cums.append(cpart)
    carry_s[...] = carry
    c_keys = jnp.concatenate(cums, axis=1)
    s = _dot(qbd, ck_ref[0].astype(BF16)) - expand_heads(c_keys)
    update(s, cv_ref[0].astype(BF16), True)

    @pl.when(kc == nkc - 1)
    def _():
        c_new = _cumsum_lanes(lfn_ref[0], LANES) + carry
        pad = jnp.zeros((LANES - sq, D), F32)
        kn = jnp.concatenate([kn_ref[0], pad], axis=0).astype(BF16)
        vn = jnp.concatenate([vn_ref[0], pad], axis=0).astype(BF16)
        c_rows = expand_heads(c_new)
        rq = lax.broadcasted_iota(jnp.int32, (nrow, LANES), 0) % sq
        kidx = lax.broadcasted_iota(jnp.int32, (nrow, LANES), 1)
        cq = jnp.sum(jnp.where(kidx == rq, c_rows, 0.0), axis=-1, keepdims=True)
        s_new = _dot_nt(qbd, kn) + (cq - c_rows)
        s_new = jnp.where(kidx <= rq, s_new, NEG_INF)
        m_s[...] = m_s[...] + cq
        update(s_new, vn, False)
        r = acc_s[...] / l_s[...]
        r = jnp.where(rowh == colh, r, 0.0)
        out = r[0:sq, :]
        for hh in range(1, H):
            out = out + r[hh * sq:(hh + 1) * sq, :]
        o_ref[0] = out


def _attn_sample(q, k_new, v_new, lf_new, cache_k, cache_v, cache_lf, layer):
    nb, sq, _ = q.shape
    past = cache_k.shape[3]
    tk = 1024
    nkc = past // tk
    cur = pl.BlockSpec((1, sq, D), lambda b, kc: (b, 0, 0))
    return pl.pallas_call(
        functools.partial(_attn_sample_kernel, sq=sq, tk=tk, nkc=nkc),
        grid=(nb, nkc),
        in_specs=[cur, cur, cur,
                  pl.BlockSpec((1, H, LANES), lambda b, kc: (b, 0, 0)),
                  pl.BlockSpec((None, 1, D, tk), lambda b, kc: (layer, b, 0, kc)),
                  pl.BlockSpec((None, 1, D, tk), lambda b, kc: (layer, b, 0, kc)),
                  pl.BlockSpec((None, 1, H, tk), lambda b, kc: (layer, b, 0, kc))],
        out_specs=cur,
        out_shape=jax.ShapeDtypeStruct((nb, sq, D), F32),
        scratch_shapes=[pltpu.VMEM((H * sq, 1), F32), pltpu.VMEM((H * sq, 1), F32),
                        pltpu.VMEM((H * sq, D), F32), pltpu.VMEM((H, 1), F32)],
        compiler_params=_cparams(2),
    )(q, k_new, v_new, lf_new, cache_k, cache_v, cache_lf)


def _lru_kernel(x_ref, g_ref, sc_ref, sh_ref, wx_ref, wg_ref, cw_ref, cb_ref, wa_ref, ba_ref,
                wi_ref, bi_ref, lam_ref, h0_ref, c0_ref, y_ref, hl_ref, cn_ref, xp_s, hc_s,
                *, tm, tpg, stride, hdr):
    i = pl.program_id(0)
    nst = (CONV_W - 1) * stride

    @pl.when(i % tpg == 0)
    def _():
        xp_s[hdr - nst:hdr, :] = c0_ref[0]
        hc_s[...] = h0_ref[0]

    h = _norm_mod(x_ref[...], g_ref[...], sc_ref[...], sh_ref[...]).astype(BF16)
    xb = _dot(h, wx_ref[...])
    gb = _dot(h, wg_ref[...])
    xp_s[hdr:hdr + tm, :] = xb
    xc = cb_ref[...] + xb * cw_ref[CONV_W - 1:CONV_W, :]
    for j in range(CONV_W - 1):
        d = (CONV_W - 1 - j) * stride
        xc = xc + xp_s[hdr - d:hdr - d + tm, :] * cw_ref[j:j + 1, :]
    tail = xp_s[hdr + tm - nst:hdr + tm, :]
    cn_ref[0] = tail
    xp_s[hdr - nst:hdr, :] = tail

    xcb = xc.astype(BF16)
    r = _sigmoid(_dot(xcb, wa_ref[...]) + ba_ref[...])
    gi = _sigmoid(_dot(xcb, wi_ref[...]) + bi_ref[...])
    log_a = LRU_C * r * _log_sigmoid(lam_ref[...])
    a = jnp.exp(log_a)
    t = jnp.tanh(log_a)
    u = jnp.sqrt((-2.0 * t) / (1.0 - t)) * gi * xc

    row = lax.broadcasted_iota(jnp.int32, (tm, 1), 0)
    sft = stride
    while sft < tm:
        keep = row >= sft
        a_sh = pltpu.roll(a, sft, 0)
        u_sh = pltpu.roll(u, sft, 0)
        u = jnp.where(keep, a * u_sh + u, u)
        a = jnp.where(keep, a * a_sh, a)
        sft *= 2
    hc = hc_s[...]
    if stride == 1:
        hs = a * hc + u
    else:
        hs = a * jnp.concatenate([hc] * (tm // stride), axis=0) + u
    last = hs[tm - stride:tm, :]
    hc_s[...] = last
    hl_ref[0] = last
    y_ref[...] = (hs * _gelu_tanh(gb)).astype(BF16)


def _lru_mix(x, g, mod, wx, wg, cw, cb, wa, ba, wi, bi, lam, h0, c0, rt, stride):
    tm, tpg = rt.tm, rt.tpg
    hdr = -(-(CONV_W - 1) * stride // 8) * 8
    nst = (CONV_W - 1) * stride
    full = lambda shape: pl.BlockSpec(shape, lambda i: (0,) * len(shape))
    row = pl.BlockSpec((tm, D), lambda i: (i, 0))
    grp = rt.groups
    return pl.pallas_call(
        functools.partial(_lru_kernel, tm=tm, tpg=tpg, stride=stride, hdr=hdr),
        grid=(rt.n_tiles,),
        in_specs=[row, full((1, D)), rt.mod_spec(1), rt.mod_spec(0),
                  full((D, D)), full((D, D)), full((CONV_W, D)), full((1, D)),
                  full((D, D)), full((1, D)), full((D, D)), full((1, D)), full((1, D)),
                  pl.BlockSpec((1, stride, D), lambda i: (i // tpg, 0, 0)),
                  pl.BlockSpec((1, nst, D), lambda i: (i // tpg, 0, 0))],
        out_specs=[row,
                   pl.BlockSpec((1, stride, D), lambda i: (i // tpg, 0, 0)),
                   pl.BlockSpec((1, nst, D), lambda i: (i // tpg, 0, 0))],
        out_shape=[jax.ShapeDtypeStruct((rt.rows, D), BF16),
                   jax.ShapeDtypeStruct((grp, stride, D), F32),
                   jax.ShapeDtypeStruct((grp, nst, D), F32)],
        scratch_shapes=[pltpu.VMEM((hdr + tm, D), F32), pltpu.VMEM((stride, D), F32)],
        compiler_params=_cparams(1),
    )(x, g, mod, mod, wx, wg, cw, cb, wa, ba, wi, bi, lam, h0, c0)


def _mix_in(a_ref, pair_major):
    if pair_major:
        return jnp.concatenate([a_ref[0, hp] for hp in range(NPAIR)], axis=1)
    return a_ref[...]


def _ffn_kernel(x_ref, a_ref, ga_ref, wo_ref, g2_ref, sc_ref, sh_ref, gm_ref, wg_ref, wu_ref, wd_ref,
                o_ref, x1_s, h_s, acc_s, *, nf, pair_major):
    f = pl.program_id(1)

    @pl.when(f == 0)
    def _():
        a = _mix_in(a_ref, pair_major).astype(BF16)
        x1 = x_ref[...] + ga_ref[...] * _dot(a, wo_ref[...])
        x1_s[...] = x1
        h_s[...] = _norm_mod(x1, g2_ref[...], sc_ref[...], sh_ref[...]).astype(BF16)
        acc_s[...] = jnp.zeros_like(acc_s)

    h = h_s[...]
    act = (_silu(_dot(h, wg_ref[...])) * _dot(h, wu_ref[...])).astype(BF16)
    acc_s[...] += _dot(act, wd_ref[...])

    @pl.when(f == nf - 1)
    def _():
        o_ref[...] = x1_s[...] + gm_ref[...] * acc_s[...]


def _route_kernel(x_ref, a_ref, ga_ref, wo_ref, g2_ref, sc_ref, sh_ref, wr_ref,
                  x1_ref, h_ref, route_ref, gate_ref, cnt_ref, carry_s, *, tm):
    i = pl.program_id(0)

    @pl.when(i == 0)
    def _():
        carry_s[...] = jnp.zeros_like(carry_s)

    lane = lax.broadcasted_iota(jnp.int32, (tm, LANES), 1)
    x1 = x_ref[...] + ga_ref[...] * _dot(a_ref[...].astype(BF16), wo_ref[...])
    x1_ref[...] = x1
    hf = _norm_mod(x1, g2_ref[...], sc_ref[...], sh_ref[...])
    h_ref[...] = hf
    logits = jnp.dot(hf, wr_ref[...], preferred_element_type=F32, precision=lax.Precision.HIGHEST)
    logits = jnp.where(lane < NEXP, logits, NEG_INF)
    m1 = jnp.max(logits, axis=-1, keepdims=True)
    i1 = jnp.min(jnp.where(logits == m1, lane, LANES), axis=-1, keepdims=True)
    rest = jnp.where(lane == i1, NEG_INF, logits)
    m2 = jnp.max(rest, axis=-1, keepdims=True)
    i2 = jnp.min(jnp.where(rest == m2, lane, LANES), axis=-1, keepdims=True)
    t = jnp.exp(m2 - m1)
    gate_ref[...] = jnp.where(lane == 0, 1.0 / (1.0 + t), jnp.where(lane == 1, t / (1.0 + t), 0.0))
    sel = (lane == i1) | (lane == i2)
    r = lax.broadcasted_iota(jnp.int32, (tm, tm), 0)
    c = lax.broadcasted_iota(jnp.int32, (tm, tm), 1)
    before = _dot((c < r).astype(BF16), sel.astype(BF16)) + carry_s[...]
    rank1 = jnp.sum(jnp.where(lane == i1, before, 0.0), axis=-1, keepdims=True).astype(jnp.int32)
    rank2 = jnp.sum(jnp.where(lane == i2, before, 0.0), axis=-1, keepdims=True).astype(jnp.int32)
    route_ref[...] = jnp.where(lane == 0, i1, jnp.where(lane == 1, i2, jnp.where(
        lane == 2, rank1, jnp.where(lane == 3, rank2, 0))))
    total = carry_s[...] + jnp.sum(sel.astype(F32), axis=0, keepdims=True)
    carry_s[...] = total
    cnt_ref[...] = total.astype(jnp.int32)


def _dispatch_kernel(dest_ref, h_ref, zero_ref, o_ref, sem, *, tm):
    del zero_ref
    base = pl.program_id(0) * tm

    def issue(r, carry):
        for s in range(2):
            d = dest_ref[2 * (base + r) + s]
            pltpu.make_async_copy(h_ref.at[pl.ds(r, 1), :], o_ref.at[pl.ds(d, 1), :], sem.at[s]).start()
        return carry

    lax.fori_loop(0, tm, issue, 0)
    for s in range(2):
        pltpu.make_async_copy(h_ref, o_ref.at[pl.ds(0, tm), :], sem.at[s]).wait()


def _experts_kernel(te_ref, nu_ref, hs_ref, wg_ref, wu_ref, wd_ref, o_ref, hb_s, acc_s, *, nf):
    del te_ref
    i = pl.program_id(0)
    f = pl.program_id(1)

    @pl.when(i < nu_ref[0])
    def _():
        @pl.when(f == 0)
        def _():
            hb_s[...] = hs_ref[...].astype(BF16)
            acc_s[...] = jnp.zeros_like(acc_s)

        h = hb_s[...]
        act = (_silu(_dot(h, wg_ref[0, 0])) * _dot(h, wu_ref[0, 0])).astype(BF16)
        acc_s[...] += _dot(act, wd_ref[0, 0])

        @pl.when(f == nf - 1)
        def _():
            o_ref[...] = acc_s[...]

    @pl.when((i >= nu_ref[0]) & (f == 0))
    def _():
        o_ref[...] = jnp.zeros_like(o_ref)


def _combine_kernel(dest_ref, x1_ref, gm_ref, gate_ref, gf_ref, y_ref, o_ref, b0_s, b1_s, sem,
                    *, tm, final_norm):
    base = pl.program_id(0) * tm
    bufs = (b0_s, b1_s)

    def issue(r, carry):
        for s in range(2):
            d = dest_ref[2 * (base + r) + s]
            pltpu.make_async_copy(y_ref.at[pl.ds(d, 1), :], bufs[s].at[pl.ds(r, 1), :], sem.at[s]).start()
        return carry

    lax.fori_loop(0, tm, issue, 0)
    for s in range(2):
        pltpu.make_async_copy(y_ref.at[pl.ds(0, tm), :], bufs[s], sem.at[s]).wait()
    gate = gate_ref[...]
    y = gate[:, 0:1] * b0_s[...] + gate[:, 1:2] * b1_s[...]
    x2 = x1_ref[...] + gm_ref[...] * y
    if final_norm:
        ms = jnp.mean(x2 * x2, axis=-1, keepdims=True)
        x2 = x2 * lax.rsqrt(ms + EPS) * gf_ref[...]
    o_ref[...] = x2


def _post_ffn(x, a, mod, wo, g2, w_gu, w_down, layer, rt, pair_major):
    tm, tpg = rt.tm, rt.tpg
    nf = DFF // TF
    row = pl.BlockSpec((tm, D), lambda i, f: (i, 0))
    if pair_major:
        a_spec = pl.BlockSpec((1, NPAIR, tm, LANES), lambda i, f: (i // tpg, 0, i % tpg, 0))
    else:
        a_spec = row
    c2 = lambda shape: pl.BlockSpec(shape, lambda i, f: (0,) * len(shape))
    return pl.pallas_call(
        functools.partial(_ffn_kernel, nf=nf, pair_major=pair_major),
        grid=(rt.n_tiles, nf),
        in_specs=[row, a_spec, rt.mod_spec(2), c2((D, D)), c2((1, D)),
                  rt.mod_spec(4), rt.mod_spec(3), rt.mod_spec(5),
                  pl.BlockSpec((None, D, TF), lambda i, f: (layer, 0, f)),
                  pl.BlockSpec((None, D, TF), lambda i, f: (layer, 0, nf + f)),
                  pl.BlockSpec((None, TF, D), lambda i, f: (layer, f, 0))],
        out_specs=row,
        out_shape=jax.ShapeDtypeStruct((rt.rows, D), F32),
        scratch_shapes=[pltpu.VMEM((tm, D), F32), pltpu.VMEM((tm, D), BF16), pltpu.VMEM((tm, D), F32)],
        compiler_params=_cparams(2),
    )(x, a, mod, wo, g2, mod, mod, mod, w_gu, w_gu, w_down)


def _post_moe(x, a, mod, wo, g2, w_router, w_gu, w_down, layer, rt, final_g):
    tm, rows, n_tiles = rt.tm, rt.rows, rt.n_tiles
    nf = DFF // TF
    row = pl.BlockSpec((tm, D), lambda i, *_: (i, 0))
    lanes = pl.BlockSpec((tm, LANES), lambda i, *_: (i, 0))
    full = lambda shape: pl.BlockSpec(shape, lambda i, *_: (0,) * len(shape))
    rows_f32 = jax.ShapeDtypeStruct((rows, D), F32)
    x1, hf, route, gate, cnt = pl.pallas_call(
        functools.partial(_route_kernel, tm=tm),
        grid=(n_tiles,),
        in_specs=[row, row, rt.mod_spec(2), full((D, D)), full((1, D)), rt.mod_spec(4), rt.mod_spec(3),
                  full((D, LANES))],
        out_specs=[row, row, lanes, lanes, full((1, LANES))],
        out_shape=[rows_f32, rows_f32, jax.ShapeDtypeStruct((rows, LANES), jnp.int32),
                   jax.ShapeDtypeStruct((rows, LANES), F32), jax.ShapeDtypeStruct((1, LANES), jnp.int32)],
        scratch_shapes=[pltpu.VMEM((1, LANES), F32)],
        compiler_params=_cparams(1),
    )(x, a, mod, wo, g2, mod, mod, w_router)

    n_tiles_g = -(-(2 * rows + NEXP * (TG - 1)) // TG)
    tiles_e = (cnt[0, :NEXP] + (TG - 1)) // TG
    tile_end = jnp.cumsum(tiles_e)
    start = (tile_end - tiles_e) * TG
    is_e = route[:, 0:2, None] == jnp.arange(NEXP, dtype=jnp.int32)
    dest = (route[:, 2:4] + jnp.sum(jnp.where(is_e, start, 0), axis=-1)).reshape(-1).astype(jnp.int32)
    tile_ids = jnp.arange(n_tiles_g, dtype=jnp.int32)
    tile_expert = jnp.minimum(jnp.sum(tile_ids[:, None] >= tile_end[None, :], axis=1), NEXP - 1)
    tile_expert = tile_expert.astype(jnp.int32)
    n_used = tile_end[NEXP - 1:].astype(jnp.int32)

    sorted_h = pl.pallas_call(
        functools.partial(_dispatch_kernel, tm=tm),
        grid_spec=pltpu.PrefetchScalarGridSpec(
            num_scalar_prefetch=1, grid=(n_tiles,),
            in_specs=[row, pl.BlockSpec(memory_space=pl.ANY)],
            out_specs=pl.BlockSpec(memory_space=pl.ANY),
            scratch_shapes=[pltpu.SemaphoreType.DMA((2,))]),
        out_shape=jax.ShapeDtypeStruct((n_tiles_g * TG, D), F32),
        input_output_aliases={2: 0},
        compiler_params=_cparams(1),
    )(dest, hf, jnp.zeros((n_tiles_g * TG, D), F32))

    def tile_of(i, nu):
        return jnp.minimum(i, nu[0] - 1)

    def chunk_of(i, f, nu):
        return jnp.where(i < nu[0], f, nf - 1)

    y_sorted = pl.pallas_call(
        functools.partial(_experts_kernel, nf=nf),
        grid_spec=pltpu.PrefetchScalarGridSpec(
            num_scalar_prefetch=2, grid=(n_tiles_g, nf),
            in_specs=[
                pl.BlockSpec((TG, D), lambda i, f, te, nu: (tile_of(i, nu), 0)),
                pl.BlockSpec((1, 1, D, TF),
                             lambda i, f, te, nu: (layer, te[tile_of(i, nu)], 0, chunk_of(i, f, nu))),
                pl.BlockSpec((1, 1, D, TF),
                             lambda i, f, te, nu: (layer, te[tile_of(i, nu)], 0, nf + chunk_of(i, f, nu))),
                pl.BlockSpec((1, 1, TF, D),
                             lambda i, f, te, nu: (layer, te[tile_of(i, nu)], chunk_of(i, f, nu), 0))],
            out_specs=pl.BlockSpec((TG, D), lambda i, f, te, nu: (i, 0)),
            scratch_shapes=[pltpu.VMEM((TG, D), BF16), pltpu.VMEM((TG, D), F32)]),
        out_shape=jax.ShapeDtypeStruct((n_tiles_g * TG, D), F32),
        compiler_params=_cparams(2),
    )(tile_expert, n_used, sorted_h, w_gu, w_gu, w_down)

    gf = jnp.ones((1, D), F32) if final_g is None else final_g
    return pl.pallas_call(
        functools.partial(_combine_kernel, tm=tm, final_norm=final_g is not None),
        grid_spec=pltpu.PrefetchScalarGridSpec(
            num_scalar_prefetch=1, grid=(n_tiles,),
            in_specs=[row, rt.mod_spec(5), lanes, full((1, D)), pl.BlockSpec(memory_space=pl.ANY)],
            out_specs=row,
            scratch_shapes=[pltpu.VMEM((tm, D), F32), pltpu.VMEM((tm, D), F32),
                            pltpu.SemaphoreType.DMA((2,))]),
        out_shape=rows_f32,
        compiler_params=_cparams(1),
    )(dest, x1, mod, gate, gf, y_sorted)


def _final_norm_kernel(x_ref, g_ref, o_ref):
    x = x_ref[...]
    ms = jnp.mean(x * x, axis=-1, keepdims=True)
    o_ref[...] = x * lax.rsqrt(ms + EPS) * g_ref[...]


def _final_norm(x, g, tm):
    rows = x.shape[0]
    return pl.pallas_call(
        _final_norm_kernel,
        grid=(rows // tm,),
        in_specs=[pl.BlockSpec((tm, D), lambda i: (i, 0)), pl.BlockSpec((1, D), lambda i: (0, 0))],
        out_specs=pl.BlockSpec((tm, D), lambda i: (i, 0)),
        out_shape=jax.ShapeDtypeStruct((rows, D), F32),
        compiler_params=_cparams(1),
    )(x, g)


def _prep_weights(p):
    w = {}
    fw = p['fox_w_in']
    w['wq'] = fw[:, :, 0 * D:1 * D].astype(BF16)
    w['wk'] = fw[:, :, 1 * D:2 * D].astype(BF16)
    w['wv'] = fw[:, :, 2 * D:3 * D].astype(BF16)
    w['wf'] = jnp.pad(fw[:, :, 3 * D:], ((0, 0), (0, 0), (0, LANES - H))).astype(BF16)
    w['bf'] = jnp.pad(p['fox_b_f'], ((0, 0), (0, LANES - H)))[:, None, :]
    w['wkt'] = w['wk'].transpose(0, 2, 1)
    w['wvt'] = w['wv'].transpose(0, 2, 1)
    w['wft'] = fw[:, :, 3 * D:].transpose(0, 2, 1).astype(BF16)
    w['bft'] = p['fox_b_f'][:, :, None]
    w['fox_wo'] = p['fox_w_out'].astype(BF16)
    w['lru_wx'] = p['lru_w_in'][:, :, :D].astype(BF16)
    w['lru_wg'] = p['lru_w_in'][:, :, D:].astype(BF16)
    nblk = p['lru_w_a'].shape[1]
    eye = jnp.eye(nblk, dtype=F32)
    bd = lambda m: jnp.einsum('lnkj,nm->lnkmj', m, eye).reshape(m.shape[0], D, D).astype(BF16)
    w['lru_wa'] = bd(p['lru_w_a'])
    w['lru_wi'] = bd(p['lru_w_i'])
    w['lru_wo'] = p['lru_w_out'].astype(BF16)
    w['ffn_gu'] = p['ffn_w_gu'].astype(BF16)
    w['ffn_down'] = p['ffn_w_down'].astype(BF16)
    w['router'] = jnp.pad(p['moe_router'], ((0, 0), (0, 0), (0, LANES - NEXP)))
    w['moe_gu'] = p['moe_w_gu'].astype(BF16)
    w['moe_down'] = p['moe_w_down'].astype(BF16)
    return w


def _trunk(x, mods, p, w, rts, prompt, seq, fox_past, lru_past):
    rt, rt_ffn, rt_moe = rts
    depth = p['norm_mix_g'].shape[0]
    n_fox = w['wq'].shape[0]
    final_g = p['final_g'][None, :]
    fox_new, lru_new = [], []
    for l in range(depth):
        j = l // 2
        last = l == depth - 1
        mod = mods[l]
        g1 = p['norm_mix_g'][l][None, :]
        g2 = p['norm_ffn_g'][l][None, :]
        if l % 2 == 0:
            if prompt:
                kt, vt, lft, qa, kat, vp = _fox_in_prompt(
                    x, g1, mod, w['wq'][j], w['wkt'][j], w['wvt'][j], w['wv'][j], w['wf'][j], w['bf'][j],
                    w['wft'][j], w['bft'][j], rt, seq, j, n_fox, fox_new[0] if fox_new else None)
                a = _attn_prompt(qa, kat, vp, tq=512)
                fox_new = [(kt, vt, lft)]
            else:
                k, v, lf, q = _fox_in_sample(x, g1, mod, w['wq'][j], w['wk'][j], w['wv'][j], w['wf'][j],
                                             w['bf'][j], rt)
                nb = rt.rows // seq
                to_b = lambda t: t.reshape(seq, nb, -1).transpose(1, 0, 2)
                kb, vb, lfb = to_b(k), to_b(v), to_b(lf)
                ck, cv, clf = fox_past
                lfn_t = jnp.pad(lfb.transpose(0, 2, 1), ((0, 0), (0, 0), (0, LANES - seq)))
                ab = _attn_sample(to_b(q), kb, vb, lfn_t, ck, cv, clf, j)
                a = ab.transpose(1, 0, 2).reshape(rt.rows, D)
                fox_new.append((kb, vb, lfb))
            x = _post_ffn(x, a, mod, w['fox_wo'][j], g2, w['ffn_gu'], w['ffn_down'], j, rt_ffn,
                          pair_major=prompt)
            if last:
                x = _final_norm(x, final_g, rt.tm)
        else:
            h0, c0, stride = lru_past[j]
            y, hl, cn = _lru_mix(x, g1, mod, w['lru_wx'][j], w['lru_wg'][j], p['lru_conv_w'][j],
                                 p['lru_conv_b'][j][None, :], w['lru_wa'][j], p['lru_b_a'][j][None, :],
                                 w['lru_wi'][j], p['lru_b_i'][j][None, :], p['lru_lam'][j][None, :],
                                 h0, c0, rt, stride)
            lru_new.append((hl, cn))
            x = _post_moe(x, y, mod, w['lru_wo'][j], g2, w['router'][j], w['moe_gu'], w['moe_down'], j,
                          rt_moe, final_g if last else None)
    return x, fox_new, lru_new


def kernel(x_prompt, x_sample, c_prompt, c_sample, cache_k, cache_v, cache_logf, state_h, state_conv,
           norm_mix_g, norm_ffn_g, final_g, w_ada, b_ada, fox_w_in, fox_b_f, fox_w_out,
           lru_w_in, lru_conv_w, lru_conv_b, lru_w_a, lru_b_a, lru_w_i, lru_b_i, lru_lam, lru_w_out,
           ffn_w_gu, ffn_w_down, moe_router, moe_w_gu, moe_w_down):
    p = {
        'norm_mix_g': norm_mix_g, 'norm_ffn_g': norm_ffn_g, 'final_g': final_g,
        'fox_w_in': fox_w_in, 'fox_b_f': fox_b_f, 'fox_w_out': fox_w_out,
        'lru_w_in': lru_w_in, 'lru_conv_w': lru_conv_w, 'lru_conv_b': lru_conv_b,
        'lru_w_a': lru_w_a, 'lru_b_a': lru_b_a, 'lru_w_i': lru_w_i, 'lru_b_i': lru_b_i,
        'lru_lam': lru_lam, 'lru_w_out': lru_w_out,
        'ffn_w_gu': ffn_w_gu, 'ffn_w_down': ffn_w_down,
        'moe_router': moe_router, 'moe_w_gu': moe_w_gu, 'moe_w_down': moe_w_down,
    }
    bp, sp, _ = x_prompt.shape
    bs, ss, _ = x_sample.shape
    depth = w_ada.shape[0]
    n_fox, n_lru = cache_k.shape[0], state_h.shape[0]
    past = cache_k.shape[2]
    w = _prep_weights(p)

    nc = bp + bs
    ncp = -(-nc // 8) * 8
    c_all = jnp.pad(jnp.concatenate([c_prompt, c_sample], axis=0), ((0, ncp - nc), (0, 0)))
    mod_all = _ada_mod(c_all, w_ada, b_ada)
    mods_p = [mod_all[l, :bp].reshape(bp, 6, 1, D).transpose(1, 0, 2, 3) for l in range(depth)]
    mods_s = [jnp.tile(mod_all[l, bp:nc].reshape(bs, 6, D).transpose(1, 0, 2), (1, ss, 1))[:, None]
              for l in range(depth)]

    rts_p = tuple(_Rows(bp * sp, tm, bp, per_row_mod=False) for tm in (TM_MIX, TM_FFN, TM_MOE))
    zeros_h = jnp.zeros((bp, 1, D), F32)
    zeros_c = jnp.zeros((bp, CONV_W - 1, D), F32)
    y_p, fox_p, lru_p = _trunk(x_prompt.reshape(bp * sp, D), mods_p, p, w, rts_p, True, sp,
                               None, [(zeros_h, zeros_c, 1)] * n_lru)

    rt_s = _Rows(bs * ss, bs * ss, 1, per_row_mod=True)
    xs = x_sample.transpose(1, 0, 2).reshape(ss * bs, D)
    ckt = cache_k.transpose(0, 1, 3, 4, 2).reshape(n_fox, bs, D, past)
    cvt = cache_v.transpose(0, 1, 3, 4, 2).reshape(n_fox, bs, D, past)
    clft = cache_logf.transpose(0, 1, 3, 2)
    lru_past = [(state_h[j][None], state_conv[j].transpose(1, 0, 2).reshape(1, (CONV_W - 1) * bs, D), bs)
                for j in range(n_lru)]
    y_s, fox_s, lru_s = _trunk(xs, mods_s, p, w, (rt_s, rt_s, rt_s), False, ss, (ckt, cvt, clft), lru_past)

    y_prompt = y_p.reshape(bp, sp, D)
    y_sample = y_s.reshape(ss, bs, D).transpose(1, 0, 2)
    kt_all, vt_all, lft_all = fox_p[0]
    k_prompt = kt_all.reshape(n_fox, bp, H, HD, sp).transpose(0, 1, 4, 2, 3)
    v_prompt = vt_all.reshape(n_fox, bp, H, HD, sp).transpose(0, 1, 4, 2, 3)
    logf_prompt = lft_all.transpose(0, 1, 3, 2)
    h_prompt = jnp.stack([s[0][:, 0, :] for s in lru_p])
    conv_prompt = jnp.stack([s[1] for s in lru_p])
    k_sample = jnp.stack([f[0] for f in fox_s]).reshape(n_fox, bs, ss, H, HD)
    v_sample = jnp.stack([f[1] for f in fox_s]).reshape(n_fox, bs, ss, H, HD)
    logf_sample = jnp.stack([f[2] for f in fox_s])
    h_sample = jnp.stack([s[0][0] for s in lru_s])
    conv_sample = jnp.stack([s[1][0].reshape(CONV_W - 1, bs, D).transpose(1, 0, 2) for s in lru_s])
    return (y_prompt, y_sample, k_prompt, v_prompt, logf_prompt, h_prompt, conv_prompt,
            k_sample, v_sample, logf_sample, h_sample, conv_sample)
```

```python
import functools
import math

import jax
import jax.numpy as jnp
from jax import lax
from jax.experimental import pallas as pl
from jax.experimental.pallas import tpu as pltpu

F32 = jnp.float32
BF16 = jnp.bfloat16

D = 1024
H = 16
HD = 64
NPAIR = H // 2
LANES = 128
DFF = 3584
NEXP = 8
CONV_W = 4
LRU_C = 8.0
EPS = 1e-6
QK_SCALE = HD ** -0.5
LOG2E = math.log2(math.e)
NEG_INF = float("-inf")
VMEM_LIMIT = 56 * 1024 * 1024
TM_MIX = 512
TM_FFN = 512
TM_MOE = 512
TF = 1792
TF_MOE = 1792
TG = 512


def _cparams(n_axes):
    return pltpu.CompilerParams(dimension_semantics=("arbitrary",) * n_axes,
                                vmem_limit_bytes=VMEM_LIMIT)


def _dot(a, b):
    return jnp.dot(a, b, preferred_element_type=F32)


def _dot_nt(a, b):
    return lax.dot_general(a, b, (((1,), (1,)), ((), ())), preferred_element_type=F32)


def _norm_mod(x, g, sc, sh):
    ms = jnp.mean(x * x, axis=-1, keepdims=True)
    return (x * lax.rsqrt(ms + EPS) * g) * (1.0 + sc) + sh


def _log_sigmoid(x):
    return jnp.minimum(x, 0.0) - jnp.log1p(jnp.exp(-jnp.abs(x)))


def _sigmoid(x):
    return 1.0 / (1.0 + jnp.exp(-x))


def _silu(x):
    return x * _sigmoid(x)


def _gelu_tanh(x):
    return 0.5 * x * (1.0 + jnp.tanh(math.sqrt(2.0 / math.pi) * (x + 0.044715 * (x * x * x))))


def _split3(x):
    hi = x.astype(BF16)
    r = x - hi.astype(F32)
    mid = r.astype(BF16)
    lo = (r - mid.astype(F32)).astype(BF16)
    return hi, mid, lo


def _cumsum_rows(x, n):
    r = lax.broadcasted_iota(jnp.int32, (n, n), 0)
    c = lax.broadcasted_iota(jnp.int32, (n, n), 1)
    tri = (c <= r).astype(BF16)
    hi, mid, lo = _split3(x)
    return _dot(tri, hi) + _dot(tri, mid) + _dot(tri, lo)


def _cumsum_lanes(x, n):
    r = lax.broadcasted_iota(jnp.int32, (n, n), 0)
    c = lax.broadcasted_iota(jnp.int32, (n, n), 1)
    tri = (r <= c).astype(BF16)
    hi, mid, lo = _split3(x)
    return _dot(hi, tri) + _dot(mid, tri) + _dot(lo, tri)


def _mod_kernel(c_ref, w_ref, b_ref, o_ref):
    c = c_ref[...]
    s = _silu(c).astype(BF16)
    o_ref[0] = _dot(s, w_ref[0].astype(BF16)) + b_ref[0]


def _ada_mod(c_all, w_ada, b_ada):
    nl, _, nmod = w_ada.shape
    rows = c_all.shape[0]
    tn = 1536
    return pl.pallas_call(
        _mod_kernel,
        grid=(nl, nmod // tn),
        in_specs=[
            pl.BlockSpec((rows, D), lambda l, j: (0, 0)),
            pl.BlockSpec((1, D, tn), lambda l, j: (l, 0, j)),
            pl.BlockSpec((1, 1, tn), lambda l, j: (l, 0, j)),
        ],
        out_specs=pl.BlockSpec((1, rows, tn), lambda l, j: (l, 0, j)),
        out_shape=jax.ShapeDtypeStruct((nl, rows, nmod), F32),
        compiler_params=_cparams(2),
    )(c_all, w_ada, b_ada.reshape(nl, 1, nmod))


class _Rows:
    def __init__(self, rows, tm, groups, per_row_mod):
        self.rows, self.tm, self.groups = rows, tm, groups
        self.n_tiles = rows // tm
        self.tpg = self.n_tiles // groups
        self.per_row_mod = per_row_mod

    def mod_spec(self, which):
        tpg, tm = self.tpg, self.tm
        if self.per_row_mod:
            return pl.BlockSpec((None, None, tm, D), lambda i, *_: (which, 0, i, 0))
        return pl.BlockSpec((None, None, 1, D), lambda i, *_: (which, i // tpg, 0, 0))


def _fox_in_sample_kernel(x_ref, g_ref, sc_ref, sh_ref, wq_ref, wk_ref, wv_ref, wf_ref, bf_ref,
                          k_ref, v_ref, lf_ref, q_ref):
    h = _norm_mod(x_ref[...], g_ref[...], sc_ref[...], sh_ref[...]).astype(BF16)
    q_ref[...] = _dot(h, wq_ref[...]) * QK_SCALE
    k_ref[...] = _dot(h, wk_ref[...])
    v_ref[...] = _dot(h, wv_ref[...])
    lf = _log_sigmoid(_dot(h, wf_ref[...]) + bf_ref[...])
    lf_ref[...] = lf[:, :H]


def _fox_in_sample(x, g, mod, wq, wk, wv, wf, bf, rt):
    tm = rt.tm
    full = lambda shape: pl.BlockSpec(shape, lambda i: (0,) * len(shape))
    row = pl.BlockSpec((tm, D), lambda i: (i, 0))
    rows_f32 = jax.ShapeDtypeStruct((rt.rows, D), F32)
    return pl.pallas_call(
        _fox_in_sample_kernel,
        grid=(rt.n_tiles,),
        in_specs=[row, full((1, D)), rt.mod_spec(1), rt.mod_spec(0),
                  full((D, D)), full((D, D)), full((D, D)), full((D, LANES)), full((1, LANES))],
        out_specs=[row, row, pl.BlockSpec((tm, H), lambda i: (i, 0)), row],
        out_shape=[rows_f32, rows_f32, jax.ShapeDtypeStruct((rt.rows, H), F32), rows_f32],
        compiler_params=_cparams(1),
    )(x, g, mod, mod, wq, wk, wv, wf, bf)


def _fox_in_prompt_kernel(x_ref, g_ref, sc_ref, sh_ref, wq_ref, wkt_ref, wvt_ref, wv_ref, wf_ref, bf_ref,
                          wft_ref, bft_ref, *rest, tm, tpg, n_prev):
    kt_ref, vt_ref, lft_ref, qa_ref, kat_ref, vp_ref, carry_r, carry_c = rest[n_prev:]
    i = pl.program_id(0)

    @pl.when(i % tpg == 0)
    def _():
        carry_r[...] = jnp.zeros_like(carry_r)
        carry_c[...] = jnp.zeros_like(carry_c)

    h = _norm_mod(x_ref[...], g_ref[...], sc_ref[...], sh_ref[...]).astype(BF16)
    q = _dot(h, wq_ref[...]) * (QK_SCALE * LOG2E)
    kt = _dot_nt(wkt_ref[...], h)
    kt_ref[0, 0] = kt
    vt_ref[0, 0] = _dot_nt(wvt_ref[...], h)
    vb = _dot(h, wv_ref[...]).astype(BF16)
    for hp in range(NPAIR):
        vp_ref[0, hp] = vb[:, hp * LANES:(hp + 1) * LANES]

    lf = _log_sigmoid(_dot(h, wf_ref[...]) + bf_ref[...])
    lft = _log_sigmoid(_dot_nt(wft_ref[...], h) + bft_ref[...])
    lft_ref[0, 0] = lft
    cum = _cumsum_rows(lf, tm) + carry_r[...]
    carry_r[...] = cum[tm - 1:tm, :]
    cumt = _cumsum_lanes(lft, tm) + carry_c[...]
    carry_c[...] = cumt[:, tm - 1:tm]
    hi, mid, lo = [p.astype(F32) for p in _split3(cum * LOG2E)]
    hit, midt, lot = [p.astype(F32) for p in _split3(cumt * LOG2E)]
    lane = lax.broadcasted_iota(jnp.int32, (tm, HD), 1)
    srow = lax.broadcasted_iota(jnp.int32, (HD, tm), 0)
    for hh in range(H):
        chi, cmid, clo = hi[:, hh:hh + 1], mid[:, hh:hh + 1], lo[:, hh:hh + 1]
        q_tail = jnp.where(lane == 0, chi, jnp.where(lane == 1, cmid, jnp.where(
            lane == 2, clo, jnp.where(lane < 6, 1.0, 0.0))))
        khi, kmid, klo = hit[hh:hh + 1, :], midt[hh:hh + 1, :], lot[hh:hh + 1, :]
        k_tail = jnp.where(srow < 3, 1.0, jnp.where(srow == 3, -khi, jnp.where(
            srow == 4, -kmid, jnp.where(srow == 5, -klo, 0.0))))
        qa_ref[0, hh, :, 0:HD] = q[:, hh * HD:(hh + 1) * HD].astype(BF16)
        qa_ref[0, hh, :, HD:2 * HD] = q_tail.astype(BF16)
        kat_ref[0, hh, 0, 0:HD, :] = kt[hh * HD:(hh + 1) * HD, :].astype(BF16)
        kat_ref[0, hh, 0, HD:2 * HD, :] = k_tail.astype(BF16)


def _fox_in_prompt(x, g, mod, wq, wkt, wvt, wv, wf, bf, wft, bft, rt, seq, layer, n_layers, prev):
    tm, tpg, b = rt.tm, rt.tpg, rt.groups
    full = lambda shape: pl.BlockSpec(shape, lambda i: (0,) * len(shape))
    row = pl.BlockSpec((tm, D), lambda i: (i, 0))
    if prev is None:
        prev = (jnp.zeros((n_layers, b, D, seq), F32), jnp.zeros((n_layers, b, D, seq), F32),
                jnp.zeros((n_layers, b, H, seq), F32))
    prev = tuple(prev)
    n_in = 12
    in_specs = [row, full((1, D)), rt.mod_spec(1), rt.mod_spec(0),
                full((D, D)), full((D, D)), full((D, D)), full((D, D)), full((D, LANES)),
                full((1, LANES)), full((H, D)), full((H, 1))]
    in_specs += [pl.BlockSpec(memory_space=pl.ANY)] * len(prev)
    state_spec = lambda rows: pl.BlockSpec((1, 1, rows, tm), lambda i: (layer, i // tpg, 0, i % tpg))
    out_specs = [state_spec(D), state_spec(D), state_spec(H),
                 pl.BlockSpec((1, H, tm, LANES), lambda i: (i // tpg, 0, i % tpg, 0)),
                 pl.BlockSpec((1, H, 1, LANES, tm), lambda i: (i // tpg, 0, i % tpg, 0, 0)),
                 pl.BlockSpec((1, NPAIR, tm, LANES), lambda i: (i // tpg, 0, i % tpg, 0))]
    out_shape = [jax.ShapeDtypeStruct((n_layers, b, D, seq), F32),
                 jax.ShapeDtypeStruct((n_layers, b, D, seq), F32),
                 jax.ShapeDtypeStruct((n_layers, b, H, seq), F32),
                 jax.ShapeDtypeStruct((b, H, seq, LANES), BF16),
                 jax.ShapeDtypeStruct((b, H, seq // tm, LANES, tm), BF16),
                 jax.ShapeDtypeStruct((b, NPAIR, seq, LANES), BF16)]
    return pl.pallas_call(
        functools.partial(_fox_in_prompt_kernel, tm=tm, tpg=tpg, n_prev=len(prev)),
        grid=(rt.n_tiles,),
        in_specs=in_specs, out_specs=out_specs, out_shape=out_shape,
        scratch_shapes=[pltpu.VMEM((1, LANES), F32), pltpu.VMEM((H, 1), F32)],
        input_output_aliases={n_in + t: t for t in range(len(prev))},
        compiler_params=_cparams(1),
    )(x, g, mod, mod, wq, wkt, wvt, wv, wf, bf, wft, bft, *prev)


def _attn_kernel(qa_ref, kat_ref, vp_ref, o_ref, m0, m1, l0, l1, a0, a1, sa0, sa1, sb0, sb1, *, tq, tk):
    qi = pl.program_id(2)
    m_s, l_s, acc_s, sa_s, sb_s = (m0, m1), (l0, l1), (a0, a1), (sa0, sa1), (sb0, sb1)
    q = [qa_ref[0, 0], qa_ref[0, 1]]
    for j in range(2):
        m_s[j][...] = jnp.full_like(m_s[j], NEG_INF)
        l_s[j][...] = jnp.zeros_like(l_s[j])
        acc_s[j][...] = jnp.zeros_like(acc_s[j])
        sa_s[j][...] = _dot(q[j], kat_ref[0, j, 0])

    def consume(j, kblk, s):
        v = vp_ref[0, 0, pl.ds(pl.multiple_of(kblk * tk, tk), tk), :]
        m_old = m_s[j][...]
        m_new = jnp.maximum(m_old, jnp.max(s, axis=-1, keepdims=True))
        alpha = jnp.exp2(m_old - m_new)
        p = jnp.exp2(s - jnp.concatenate([m_new] * (tk // LANES), axis=1))
        psum = p[:, 0:LANES]
        for t in range(1, tk // LANES):
            psum = psum + p[:, t * LANES:(t + 1) * LANES]
        l_s[j][...] = alpha * l_s[j][...] + psum
        acc_s[j][...] = alpha * acc_s[j][...] + _dot(p.astype(BF16), v)
        m_s[j][...] = m_new

    def scores(j, kblk):
        return _dot(q[j], kat_ref[0, j, kblk])

    def pair(k2, carry):
        kb = 2 * k2
        for j in range(2):
            sb_s[j][...] = scores(j, kb + 1)
            consume(j, kb, sa_s[j][...])
        for j in range(2):
            sa_s[j][...] = scores(j, kb + 2)
            consume(j, kb + 1, sb_s[j][...])
        return carry

    lax.fori_loop(0, qi // 2, pair, 0)

    def diagonal(j, s):
        r = lax.broadcasted_iota(jnp.int32, (tq, tk), 0)
        c = lax.broadcasted_iota(jnp.int32, (tq, tk), 1)
        consume(j, qi, jnp.where(c <= r, s, NEG_INF))

    @pl.when(qi % 2 == 1)
    def _():
        for j in range(2):
            sb_s[j][...] = scores(j, qi)
            consume(j, qi - 1, sa_s[j][...])
        for j in range(2):
            diagonal(j, sb_s[j][...])

    @pl.when(qi % 2 == 0)
    def _():
        for j in range(2):
            diagonal(j, sa_s[j][...])

    outs = [acc_s[j][...] / jnp.sum(l_s[j][...], axis=-1, keepdims=True) for j in range(2)]
    lane = lax.broadcasted_iota(jnp.int32, (tq, LANES), 1)
    o_ref[0, 0] = jnp.where(lane < HD, outs[0], outs[1]).astype(BF16)


def _attn_prompt(qa, kat, vp, tq):
    b, _, seq, _ = qa.shape
    nkb, tk = kat.shape[2], kat.shape[4]
    return pl.pallas_call(
        functools.partial(_attn_kernel, tq=tq, tk=tk),
        grid=(b, NPAIR, seq // tq),
        in_specs=[
            pl.BlockSpec((1, 2, tq, LANES), lambda bb, hp, qi: (bb, hp, qi, 0)),
            pl.BlockSpec((1, 2, nkb, LANES, tk), lambda bb, hp, qi: (bb, hp, 0, 0, 0)),
            pl.BlockSpec((1, 1, seq, LANES), lambda bb, hp, qi: (bb, hp, 0, 0)),
        ],
        out_specs=pl.BlockSpec((1, 1, tq, LANES), lambda bb, hp, qi: (bb, hp, qi, 0)),
        out_shape=jax.ShapeDtypeStruct((b, NPAIR, seq, LANES), BF16),
        scratch_shapes=[pltpu.VMEM((tq, LANES), F32)] * 6 + [pltpu.VMEM((tq, tk), F32)] * 4,
        compiler_params=_cparams(3),
    )(qa, kat, vp)


def _attn_sample_kernel(q_ref, kn_ref, vn_ref, lfn_ref, ck_ref, cv_ref, clf_ref, o_ref,
                        m_s, l_s, acc_s, carry_s, *, sq, tk, nkc):
    kc = pl.program_id(1)
    nrow = H * sq

    @pl.when(kc == 0)
    def _():
        m_s[...] = jnp.full_like(m_s, NEG_INF)
        l_s[...] = jnp.zeros_like(l_s)
        acc_s[...] = jnp.zeros_like(acc_s)
        carry_s[...] = jnp.zeros_like(carry_s)

    rowh = lax.broadcasted_iota(jnp.int32, (nrow, D), 0) // sq
    colh = lax.broadcasted_iota(jnp.int32, (nrow, D), 1) // HD
    q = q_ref[0]
    qbd = jnp.where(rowh == colh, jnp.concatenate([q] * H, axis=0), 0.0).astype(BF16)

    def expand_heads(ct):
        n = ct.shape[1]
        return jnp.concatenate([jnp.broadcast_to(ct[hh:hh + 1, :], (sq, n)) for hh in range(H)], axis=0)

    def update(s, vb, v_transposed):
        m_old = m_s[...]
        m_new = jnp.maximum(m_old, jnp.max(s, axis=-1, keepdims=True))
        alpha = jnp.exp(m_old - m_new)
        p = jnp.exp(s - m_new)
        l_s[...] = alpha * l_s[...] + jnp.sum(p, axis=-1, keepdims=True)
        pb = p.astype(BF16)
        pv = _dot_nt(pb, vb) if v_transposed else _dot(pb, vb)
        acc_s[...] = alpha * acc_s[...] + pv
        m_s[...] = m_new

    sub = 256
    cums = []
    carry = carry_s[...]
    for t in range(tk // sub):
        cpart = _cumsum_lanes(clf_ref[0, :, t * sub:(t + 1) * sub], sub) + carry
        carry = cpart[:, sub - 1:sub]
        cums.append(cpart)
    carry_s[...] = carry
    c_keys = jnp.concatenate(cums, axis=1)
    s = _dot(qbd, ck_ref[0].astype(BF16)) - expand_heads(c_keys)
    update(s, cv_ref[0].astype(BF16), True)

    @pl.when(kc == nkc - 1)
    def _():
        c_new = _cumsum_lanes(lfn_ref[0], LANES) + carry
        pad = jnp.zeros((LANES - sq, D), F32)
        kn = jnp.concatenate([kn_ref[0], pad], axis=0).astype(BF16)
        vn = jnp.concatenate([vn_ref[0], pad], axis=0).astype(BF16)
        c_rows = expand_heads(c_new)
        rq = lax.broadcasted_iota(jnp.int32, (nrow, LANES), 0) % sq
        kidx = lax.broadcasted_iota(jnp.int32, (nrow, LANES), 1)
        cq = jnp.sum(jnp.where(kidx == rq, c_rows, 0.0), axis=-1, keepdims=True)
        s_new = _dot_nt(qbd, kn) + (cq - c_rows)
        s_new = jnp.where(kidx <= rq, s_new, NEG_INF)
        m_s[...] = m_s[...] + cq
        update(s_new, vn, False)
        r = acc_s[...] / l_s[...]
        r = jnp.where(rowh == colh, r, 0.0)
        out = r[0:sq, :]
        for hh in range(1, H):
            out = out + r[hh * sq:(hh + 1) * sq, :]
        o_ref[0] = out


def _attn_sample(q, k_new, v_new, lf_new, cache_k, cache_v, cache_lf, layer):
    nb, sq, _ = q.shape
    past = cache_k.shape[3]
    tk = 1024
    nkc = past // tk
    cur = pl.BlockSpec((1, sq, D), lambda b, kc: (b, 0, 0))
    return pl.pallas_call(
        functools.partial(_attn_sample_kernel, sq=sq, tk=tk, nkc=nkc),
        grid=(nb, nkc),
        in_specs=[cur, cur, cur,
                  pl.BlockSpec((1, H, LANES), lambda b, kc: (b, 0, 0)),
                  pl.BlockSpec((None, 1, D, tk), lambda b, kc: (layer, b, 0, kc)),
                  pl.BlockSpec((None, 1, D, tk), lambda b, kc: (layer, b, 0, kc)),
                  pl.BlockSpec((None, 1, H, tk), lambda b, kc: (layer, b, 0, kc))],
        out_specs=cur,
        out_shape=jax.ShapeDtypeStruct((nb, sq, D), F32),
        scratch_shapes=[pltpu.VMEM((H * sq, 1), F32), pltpu.VMEM((H * sq, 1), F32),
                        pltpu.VMEM((H * sq, D), F32), pltpu.VMEM((H, 1), F32)],
        compiler_params=_cparams(2),
    )(q, k_new, v_new, lf_new, cache_k, cache_v, cache_lf)


def _lru_kernel(x_ref, g_ref, sc_ref, sh_ref, wx_ref, wg_ref, cw_ref, cb_ref, wa_ref, ba_ref,
                wi_ref, bi_ref, lam_ref, h0_ref, c0_ref, y_ref, hl_ref, cn_ref, xp_s, hc_s,
                *, tm, tpg, stride, hdr):
    i = pl.program_id(0)
    nst = (CONV_W - 1) * stride

    @pl.when(i % tpg == 0)
    def _():
        xp_s[hdr - nst:hdr, :] = c0_ref[0]
        hc_s[...] = h0_ref[0]

    h = _norm_mod(x_ref[...], g_ref[...], sc_ref[...], sh_ref[...]).astype(BF16)
    xb = _dot(h, wx_ref[...])
    gb = _dot(h, wg_ref[...])
    xp_s[hdr:hdr + tm, :] = xb
    xc = cb_ref[...] + xb * cw_ref[CONV_W - 1:CONV_W, :]
    for j in range(CONV_W - 1):
        d = (CONV_W - 1 - j) * stride
        xc = xc + xp_s[hdr - d:hdr - d + tm, :] * cw_ref[j:j + 1, :]
    tail = xp_s[hdr + tm - nst:hdr + tm, :]
    cn_ref[0] = tail
    xp_s[hdr - nst:hdr, :] = tail

    xcb = xc.astype(BF16)
    r = _sigmoid(_dot(xcb, wa_ref[...]) + ba_ref[...])
    gi = _sigmoid(_dot(xcb, wi_ref[...]) + bi_ref[...])
    log_a = LRU_C * r * _log_sigmoid(lam_ref[...])
    a = jnp.exp(log_a)
    t = jnp.tanh(log_a)
    u = jnp.sqrt((-2.0 * t) / (1.0 - t)) * gi * xc

    row = lax.broadcasted_iota(jnp.int32, (tm, 1), 0)
    sft = stride
    while sft < tm:
        keep = row >= sft
        a_sh = pltpu.roll(a, sft, 0)
        u_sh = pltpu.roll(u, sft, 0)
        u = jnp.where(keep, a * u_sh + u, u)
        a = jnp.where(keep, a * a_sh, a)
        sft *= 2
    hc = hc_s[...]
    if stride == 1:
        hs = a * hc + u
    else:
        hs = a * jnp.concatenate([hc] * (tm // stride), axis=0) + u
    last = hs[tm - stride:tm, :]
    hc_s[...] = last
    hl_ref[0] = last
    y_ref[...] = (hs * _gelu_tanh(gb)).astype(BF16)


def _lru_mix(x, g, mod, wx, wg, cw, cb, wa, ba, wi, bi, lam, h0, c0, rt, stride):
    tm, tpg = rt.tm, rt.tpg
    hdr = -(-(CONV_W - 1) * stride // 8) * 8
    nst = (CONV_W - 1) * stride
    full = lambda shape: pl.BlockSpec(shape, lambda i: (0,) * len(shape))
    row = pl.BlockSpec((tm, D), lambda i: (i, 0))
    grp = rt.groups
    return pl.pallas_call(
        functools.partial(_lru_kernel, tm=tm, tpg=tpg, stride=stride, hdr=hdr),
        grid=(rt.n_tiles,),
        in_specs=[row, full((1, D)), rt.mod_spec(1), rt.mod_spec(0),
                  full((D, D)), full((D, D)), full((CONV_W, D)), full((1, D)),
                  full((D, D)), full((1, D)), full((D, D)), full((1, D)), full((1, D)),
                  pl.BlockSpec((1, stride, D), lambda i: (i // tpg, 0, 0)),
                  pl.BlockSpec((1, nst, D), lambda i: (i // tpg, 0, 0))],
        out_specs=[row,
                   pl.BlockSpec((1, stride, D), lambda i: (i // tpg, 0, 0)),
                   pl.BlockSpec((1, nst, D), lambda i: (i // tpg, 0, 0))],
        out_shape=[jax.ShapeDtypeStruct((rt.rows, D), BF16),
                   jax.ShapeDtypeStruct((grp, stride, D), F32),
                   jax.ShapeDtypeStruct((grp, nst, D), F32)],
        scratch_shapes=[pltpu.VMEM((hdr + tm, D), F32), pltpu.VMEM((stride, D), F32)],
        compiler_params=_cparams(1),
    )(x, g, mod, mod, wx, wg, cw, cb, wa, ba, wi, bi, lam, h0, c0)


def _mix_in(a_ref, pair_major):
    if pair_major:
        return jnp.concatenate([a_ref[0, hp] for hp in range(NPAIR)], axis=1)
    return a_ref[...]


def _ffn_kernel(x_ref, a_ref, ga_ref, wo_ref, g2_ref, sc_ref, sh_ref, gm_ref, wg_ref, wu_ref, wd_ref,
                o_ref, x1_s, h_s, acc_s, *, nf, pair_major):
    f = pl.program_id(1)

    @pl.when(f == 0)
    def _():
        a = _mix_in(a_ref, pair_major).astype(BF16)
        x1 = x_ref[...] + ga_ref[...] * _dot(a, wo_ref[...])
        x1_s[...] = x1
        h_s[...] = _norm_mod(x1, g2_ref[...], sc_ref[...], sh_ref[...]).astype(BF16)
        acc_s[...] = jnp.zeros_like(acc_s)

    h = h_s[...]
    act = (_silu(_dot(h, wg_ref[...])) * _dot(h, wu_ref[...])).astype(BF16)
    acc_s[...] += _dot(act, wd_ref[...])

    @pl.when(f == nf - 1)
    def _():
        o_ref[...] = x1_s[...] + gm_ref[...] * acc_s[...]


def _route_kernel(x_ref, a_ref, ga_ref, wo_ref, g2_ref, sc_ref, sh_ref, wr_ref,
                  x1_ref, h_ref, route_ref, gate_ref, cnt_ref, carry_s, *, tm):
    i = pl.program_id(0)

    @pl.when(i == 0)
    def _():
        carry_s[...] = jnp.zeros_like(carry_s)

    lane = lax.broadcasted_iota(jnp.int32, (tm, LANES), 1)
    x1 = x_ref[...] + ga_ref[...] * _dot(a_ref[...].astype(BF16), wo_ref[...])
    x1_ref[...] = x1
    hf = _norm_mod(x1, g2_ref[...], sc_ref[...], sh_ref[...])
    h_ref[...] = hf
    logits = jnp.dot(hf, wr_ref[...], preferred_element_type=F32, precision=lax.Precision.HIGHEST)
    logits = jnp.where(lane < NEXP, logits, NEG_INF)
    m1 = jnp.max(logits, axis=-1, keepdims=True)
    i1 = jnp.min(jnp.where(logits == m1, lane, LANES), axis=-1, keepdims=True)
    rest = jnp.where(lane == i1, NEG_INF, logits)
    m2 = jnp.max(rest, axis=-1, keepdims=True)
    i2 = jnp.min(jnp.where(rest == m2, lane, LANES), axis=-1, keepdims=True)
    t = jnp.exp(m2 - m1)
    gate_ref[...] = jnp.where(lane == 0, 1.0 / (1.0 + t), jnp.where(lane == 1, t / (1.0 + t), 0.0))
    sel = (lane == i1) | (lane == i2)
    r = lax.broadcasted_iota(jnp.int32, (tm, tm), 0)
    c = lax.broadcasted_iota(jnp.int32, (tm, tm), 1)
    before = _dot((c < r).astype(BF16), sel.astype(BF16)) + carry_s[...]
    rank1 = jnp.sum(jnp.where(lane == i1, before, 0.0), axis=-1, keepdims=True).astype(jnp.int32)
    rank2 = jnp.sum(jnp.where(lane == i2, before, 0.0), axis=-1, keepdims=True).astype(jnp.int32)
    route_ref[...] = jnp.where(lane == 0, i1, jnp.where(lane == 1, i2, jnp.where(
        lane == 2, rank1, jnp.where(lane == 3, rank2, 0))))
    total = carry_s[...] + jnp.sum(sel.astype(F32), axis=0, keepdims=True)
    carry_s[...] = total
    cnt_ref[...] = total.astype(jnp.int32)


def _dispatch_kernel(dest_ref, h_ref, zero_ref, o_ref, sem, *, tm):
    del zero_ref
    base = pl.program_id(0) * tm

    def issue(r, carry):
        for s in range(2):
            d = dest_ref[2 * (base + r) + s]
            pltpu.make_async_copy(h_ref.at[pl.ds(r, 1), :], o_ref.at[pl.ds(d, 1), :], sem.at[s]).start()
        return carry

    lax.fori_loop(0, tm, issue, 0)
    for s in range(2):
        pltpu.make_async_copy(h_ref, o_ref.at[pl.ds(0, tm), :], sem.at[s]).wait()


def _experts_kernel(te_ref, nu_ref, hs_ref, wg_ref, wu_ref, wd_ref, o_ref, hb_s, acc_s, *, nf):
    del te_ref
    i = pl.program_id(0)
    f = pl.program_id(1)

    @pl.when(i < nu_ref[0])
    def _():
        @pl.when(f == 0)
        def _():
            hb_s[...] = hs_ref[...].astype(BF16)
            acc_s[...] = jnp.zeros_like(acc_s)

        h = hb_s[...]
        act = (_silu(_dot(h, wg_ref[0, 0])) * _dot(h, wu_ref[0, 0])).astype(BF16)
        acc_s[...] += _dot(act, wd_ref[0, 0])

        @pl.when(f == nf - 1)
        def _():
            o_ref[...] = acc_s[...]

    @pl.when((i >= nu_ref[0]) & (f == 0))
    def _():
        o_ref[...] = jnp.zeros_like(o_ref)


def _combine_kernel(dest_ref, x1_ref, gm_ref, gate_ref, gf_ref, y_ref, o_ref, b0_s, b1_s, sem,
                    *, tm, final_norm):
    base = pl.program_id(0) * tm
    bufs = (b0_s, b1_s)

    def issue(r, carry):
        for s in range(2):
            d = dest_ref[2 * (base + r) + s]
            pltpu.make_async_copy(y_ref.at[pl.ds(d, 1), :], bufs[s].at[pl.ds(r, 1), :], sem.at[s]).start()
        return carry

    lax.fori_loop(0, tm, issue, 0)
    for s in range(2):
        pltpu.make_async_copy(y_ref.at[pl.ds(0, tm), :], bufs[s], sem.at[s]).wait()
    gate = gate_ref[...]
    y = gate[:, 0:1] * b0_s[...] + gate[:, 1:2] * b1_s[...]
    x2 = x1_ref[...] + gm_ref[...] * y
    if final_norm:
        ms = jnp.mean(x2 * x2, axis=-1, keepdims=True)
        x2 = x2 * lax.rsqrt(ms + EPS) * gf_ref[...]
    o_ref[...] = x2


def _post_ffn(x, a, mod, wo, g2, w_gu, w_down, layer, rt, pair_major):
    tm, tpg = rt.tm, rt.tpg
    nf = DFF // TF
    row = pl.BlockSpec((tm, D), lambda i, f: (i, 0))
    if pair_major:
        a_spec = pl.BlockSpec((1, NPAIR, tm, LANES), lambda i, f: (i // tpg, 0, i % tpg, 0))
    else:
        a_spec = row
    c2 = lambda shape: pl.BlockSpec(shape, lambda i, f: (0,) * len(shape))
    return pl.pallas_call(
        functools.partial(_ffn_kernel, nf=nf, pair_major=pair_major),
        grid=(rt.n_tiles, nf),
        in_specs=[row, a_spec, rt.mod_spec(2), c2((D, D)), c2((1, D)),
                  rt.mod_spec(4), rt.mod_spec(3), rt.mod_spec(5),
                  pl.BlockSpec((None, D, TF), lambda i, f: (layer, 0, f)),
                  pl.BlockSpec((None, D, TF), lambda i, f: (layer, 0, nf + f)),
                  pl.BlockSpec((None, TF, D), lambda i, f: (layer, f, 0))],
        out_specs=row,
        out_shape=jax.ShapeDtypeStruct((rt.rows, D), F32),
        scratch_shapes=[pltpu.VMEM((tm, D), F32), pltpu.VMEM((tm, D), BF16), pltpu.VMEM((tm, D), F32)],
        compiler_params=_cparams(2),
    )(x, a, mod, wo, g2, mod, mod, mod, w_gu, w_gu, w_down)


def _post_moe(x, a, mod, wo, g2, w_router, w_gu, w_down, layer, rt, final_g):
    tm, rows, n_tiles = rt.tm, rt.rows, rt.n_tiles
    nf = DFF // TF_MOE
    row = pl.BlockSpec((tm, D), lambda i, *_: (i, 0))
    lanes = pl.BlockSpec((tm, LANES), lambda i, *_: (i, 0))
    full = lambda shape: pl.BlockSpec(shape, lambda i, *_: (0,) * len(shape))
    rows_f32 = jax.ShapeDtypeStruct((rows, D), F32)
    x1, hf, route, gate, cnt = pl.pallas_call(
        functools.partial(_route_kernel, tm=tm),
        grid=(n_tiles,),
        in_specs=[row, row, rt.mod_spec(2), full((D, D)), full((1, D)), rt.mod_spec(4), rt.mod_spec(3),
                  full((D, LANES))],
        out_specs=[row, row, lanes, lanes, full((1, LANES))],
        out_shape=[rows_f32, rows_f32, jax.ShapeDtypeStruct((rows, LANES), jnp.int32),
                   jax.ShapeDtypeStruct((rows, LANES), F32), jax.ShapeDtypeStruct((1, LANES), jnp.int32)],
        scratch_shapes=[pltpu.VMEM((1, LANES), F32)],
        compiler_params=_cparams(1),
    )(x, a, mod, wo, g2, mod, mod, w_router)

    n_tiles_g = -(-(2 * rows + NEXP * (TG - 1)) // TG)
    tiles_e = (cnt[0, :NEXP] + (TG - 1)) // TG
    tile_end = jnp.cumsum(tiles_e)
    start = (tile_end - tiles_e) * TG
    is_e = route[:, 0:2, None] == jnp.arange(NEXP, dtype=jnp.int32)
    dest = (route[:, 2:4] + jnp.sum(jnp.where(is_e, start, 0), axis=-1)).reshape(-1).astype(jnp.int32)
    tile_ids = jnp.arange(n_tiles_g, dtype=jnp.int32)
    tile_expert = jnp.minimum(jnp.sum(tile_ids[:, None] >= tile_end[None, :], axis=1), NEXP - 1)
    tile_expert = tile_expert.astype(jnp.int32)
    n_used = tile_end[NEXP - 1:].astype(jnp.int32)

    sorted_h = pl.pallas_call(
        functools.partial(_dispatch_kernel, tm=tm),
        grid_spec=pltpu.PrefetchScalarGridSpec(
            num_scalar_prefetch=1, grid=(n_tiles,),
            in_specs=[row, pl.BlockSpec(memory_space=pl.ANY)],
            out_specs=pl.BlockSpec(memory_space=pl.ANY),
            scratch_shapes=[pltpu.SemaphoreType.DMA((2,))]),
        out_shape=jax.ShapeDtypeStruct((n_tiles_g * TG, D), F32),
        input_output_aliases={2: 0},
        compiler_params=_cparams(1),
    )(dest, hf, jnp.zeros((n_tiles_g * TG, D), F32))

    def tile_of(i, nu):
        return jnp.maximum(jnp.minimum(i, nu[0] - 1), 0)

    def chunk_of(i, f, nu):
        return jnp.where(i < nu[0], f, nf - 1)

    y_sorted = pl.pallas_call(
        functools.partial(_experts_kernel, nf=nf),
        grid_spec=pltpu.PrefetchScalarGridSpec(
            num_scalar_prefetch=2, grid=(n_tiles_g, nf),
            in_specs=[
                pl.BlockSpec((TG, D), lambda i, f, te, nu: (tile_of(i, nu), 0)),
                pl.BlockSpec((1, 1, D, TF_MOE),
                             lambda i, f, te, nu: (layer, te[tile_of(i, nu)], 0, chunk_of(i, f, nu))),
                pl.BlockSpec((1, 1, D, TF_MOE),
                             lambda i, f, te, nu: (layer, te[tile_of(i, nu)], 0, nf + chunk_of(i, f, nu))),
                pl.BlockSpec((1, 1, TF_MOE, D),
                             lambda i, f, te, nu: (layer, te[tile_of(i, nu)], chunk_of(i, f, nu), 0))],
            out_specs=pl.BlockSpec((TG, D), lambda i, f, te, nu: (i, 0)),
            scratch_shapes=[pltpu.VMEM((TG, D), BF16), pltpu.VMEM((TG, D), F32)]),
        out_shape=jax.ShapeDtypeStruct((n_tiles_g * TG, D), F32),
        compiler_params=_cparams(2),
    )(tile_expert, n_used, sorted_h, w_gu, w_gu, w_down)

    gf = jnp.ones((1, D), F32) if final_g is None else final_g
    return pl.pallas_call(
        functools.partial(_combine_kernel, tm=tm, final_norm=final_g is not None),
        grid_spec=pltpu.PrefetchScalarGridSpec(
            num_scalar_prefetch=1, grid=(n_tiles,),
            in_specs=[row, rt.mod_spec(5), lanes, full((1, D)), pl.BlockSpec(memory_space=pl.ANY)],
            out_specs=row,
            scratch_shapes=[pltpu.VMEM((tm, D), F32), pltpu.VMEM((tm, D), F32),
                            pltpu.SemaphoreType.DMA((2,))]),
        out_shape=rows_f32,
        compiler_params=_cparams(1),
    )(dest, x1, mod, gate, gf, y_sorted)


def _final_norm_kernel(x_ref, g_ref, o_ref):
    x = x_ref[...]
    ms = jnp.mean(x * x, axis=-1, keepdims=True)
    o_ref[...] = x * lax.rsqrt(ms + EPS) * g_ref[...]


def _final_norm(x, g, tm):
    rows = x.shape[0]
    return pl.pallas_call(
        _final_norm_kernel,
        grid=(rows // tm,),
        in_specs=[pl.BlockSpec((tm, D), lambda i: (i, 0)), pl.BlockSpec((1, D), lambda i: (0, 0))],
        out_specs=pl.BlockSpec((tm, D), lambda i: (i, 0)),
        out_shape=jax.ShapeDtypeStruct((rows, D), F32),
        compiler_params=_cparams(1),
    )(x, g)


def _prep_weights(p):
    w = {}
    fw = p['fox_w_in']
    w['wq'] = fw[:, :, 0 * D:1 * D].astype(BF16)
    w['wk'] = fw[:, :, 1 * D:2 * D].astype(BF16)
    w['wv'] = fw[:, :, 2 * D:3 * D].astype(BF16)
    w['wf'] = jnp.pad(fw[:, :, 3 * D:], ((0, 0), (0, 0), (0, LANES - H))).astype(BF16)
    w['bf'] = jnp.pad(p['fox_b_f'], ((0, 0), (0, LANES - H)))[:, None, :]
    w['wkt'] = w['wk'].transpose(0, 2, 1)
    w['wvt'] = w['wv'].transpose(0, 2, 1)
    w['wft'] = fw[:, :, 3 * D:].transpose(0, 2, 1).astype(BF16)
    w['bft'] = p['fox_b_f'][:, :, None]
    w['fox_wo'] = p['fox_w_out'].astype(BF16)
    w['lru_wx'] = p['lru_w_in'][:, :, :D].astype(BF16)
    w['lru_wg'] = p['lru_w_in'][:, :, D:].astype(BF16)
    nblk = p['lru_w_a'].shape[1]
    eye = jnp.eye(nblk, dtype=F32)
    bd = lambda m: jnp.einsum('lnkj,nm->lnkmj', m, eye).reshape(m.shape[0], D, D).astype(BF16)
    w['lru_wa'] = bd(p['lru_w_a'])
    w['lru_wi'] = bd(p['lru_w_i'])
    w['lru_wo'] = p['lru_w_out'].astype(BF16)
    w['ffn_gu'] = p['ffn_w_gu'].astype(BF16)
    w['ffn_down'] = p['ffn_w_down'].astype(BF16)
    w['router'] = jnp.pad(p['moe_router'], ((0, 0), (0, 0), (0, LANES - NEXP)))
    w['moe_gu'] = p['moe_w_gu'].astype(BF16)
    w['moe_down'] = p['moe_w_down'].astype(BF16)
    return w


def _trunk(x, mods, p, w, rts, prompt, seq, fox_past, lru_past):
    rt, rt_ffn, rt_moe = rts
    depth = p['norm_mix_g'].shape[0]
    n_fox = w['wq'].shape[0]
    final_g = p['final_g'][None, :]
    fox_new, lru_new = [], []
    for l in range(depth):
        j = l // 2
        last = l == depth - 1
        mod = mods[l]
        g1 = p['norm_mix_g'][l][None, :]
        g2 = p['norm_ffn_g'][l][None, :]
        if l % 2 == 0:
            if prompt:
                kt, vt, lft, qa, kat, vp = _fox_in_prompt(
                    x, g1, mod, w['wq'][j], w['wkt'][j], w['wvt'][j], w['wv'][j], w['wf'][j], w['bf'][j],
                    w['wft'][j], w['bft'][j], rt, seq, j, n_fox, fox_new[0] if fox_new else None)
                a = _attn_prompt(qa, kat, vp, tq=512)
                fox_new = [(kt, vt, lft)]
            else:
                k, v, lf, q = _fox_in_sample(x, g1, mod, w['wq'][j], w['wk'][j], w['wv'][j], w['wf'][j],
                                             w['bf'][j], rt)
                nb = rt.rows // seq
                to_b = lambda t: t.reshape(seq, nb, -1).transpose(1, 0, 2)
                kb, vb, lfb = to_b(k), to_b(v), to_b(lf)
                ck, cv, clf = fox_past
                lfn_t = jnp.pad(lfb.transpose(0, 2, 1), ((0, 0), (0, 0), (0, LANES - seq)))
                ab = _attn_sample(to_b(q), kb, vb, lfn_t, ck, cv, clf, j)
                a = ab.transpose(1, 0, 2).reshape(rt.rows, D)
                fox_new.append((kb, vb, lfb))
            x = _post_ffn(x, a, mod, w['fox_wo'][j], g2, w['ffn_gu'], w['ffn_down'], j, rt_ffn,
                          pair_major=prompt)
            if last:
                x = _final_norm(x, final_g, rt.tm)
        else:
            h0, c0, stride = lru_past[j]
            y, hl, cn = _lru_mix(x, g1, mod, w['lru_wx'][j], w['lru_wg'][j], p['lru_conv_w'][j],
                                 p['lru_conv_b'][j][None, :], w['lru_wa'][j], p['lru_b_a'][j][None, :],
                                 w['lru_wi'][j], p['lru_b_i'][j][None, :], p['lru_lam'][j][None, :],
                                 h0, c0, rt, stride)
            lru_new.append((hl, cn))
            x = _post_moe(x, y, mod, w['lru_wo'][j], g2, w['router'][j], w['moe_gu'], w['moe_down'], j,
                          rt_moe, final_g if last else None)
    return x, fox_new, lru_new


def kernel(x_prompt, x_sample, c_prompt, c_sample, cache_k, cache_v, cache_logf, state_h, state_conv,
           norm_mix_g, norm_ffn_g, final_g, w_ada, b_ada, fox_w_in, fox_b_f, fox_w_out,
           lru_w_in, lru_conv_w, lru_conv_b, lru_w_a, lru_b_a, lru_w_i, lru_b_i, lru_lam, lru_w_out,
           ffn_w_gu, ffn_w_down, moe_router, moe_w_gu, moe_w_down):
    p = {
        'norm_mix_g': norm_mix_g, 'norm_ffn_g': norm_ffn_g, 'final_g': final_g,
        'fox_w_in': fox_w_in, 'fox_b_f': fox_b_f, 'fox_w_out': fox_w_out,
        'lru_w_in': lru_w_in, 'lru_conv_w': lru_conv_w, 'lru_conv_b': lru_conv_b,
        'lru_w_a': lru_w_a, 'lru_b_a': lru_b_a, 'lru_w_i': lru_w_i, 'lru_b_i': lru_b_i,
        'lru_lam': lru_lam, 'lru_w_out': lru_w_out,
        'ffn_w_gu': ffn_w_gu, 'ffn_w_down': ffn_w_down,
        'moe_router': moe_router, 'moe_w_gu': moe_w_gu, 'moe_w_down': moe_w_down,
    }
    bp, sp, _ = x_prompt.shape
    bs, ss, _ = x_sample.shape
    depth = w_ada.shape[0]
    n_fox, n_lru = cache_k.shape[0], state_h.shape[0]
    past = cache_k.shape[2]
    w = _prep_weights(p)

    nc = bp + bs
    ncp = -(-nc // 8) * 8
    c_all = jnp.pad(jnp.concatenate([c_prompt, c_sample], axis=0), ((0, ncp - nc), (0, 0)))
    mod_all = _ada_mod(c_all, w_ada, b_ada)
    mods_p = [mod_all[l, :bp].reshape(bp, 6, 1, D).transpose(1, 0, 2, 3) for l in range(depth)]
    mods_s = [jnp.tile(mod_all[l, bp:nc].reshape(bs, 6, D).transpose(1, 0, 2), (1, ss, 1))[:, None]
              for l in range(depth)]

    rts_p = tuple(_Rows(bp * sp, tm, bp, per_row_mod=False) for tm in (TM_MIX, TM_FFN, TM_MOE))
    zeros_h = jnp.zeros((bp, 1, D), F32)
    zeros_c = jnp.zeros((bp, CONV_W - 1, D), F32)
    y_p, fox_p, lru_p = _trunk(x_prompt.reshape(bp * sp, D), mods_p, p, w, rts_p, True, sp,
                               None, [(zeros_h, zeros_c, 1)] * n_lru)

    rt_s = _Rows(bs * ss, bs * ss, 1, per_row_mod=True)
    xs = x_sample.transpose(1, 0, 2).reshape(ss * bs, D)
    ckt = cache_k.transpose(0, 1, 3, 4, 2).reshape(n_fox, bs, D, past)
    cvt = cache_v.transpose(0, 1, 3, 4, 2).reshape(n_fox, bs, D, past)
    clft = cache_logf.transpose(0, 1, 3, 2)
    lru_past = [(state_h[j][None], state_conv[j].transpose(1, 0, 2).reshape(1, (CONV_W - 1) * bs, D), bs)
                for j in range(n_lru)]
    y_s, fox_s, lru_s = _trunk(xs, mods_s, p, w, (rt_s, rt_s, rt_s), False, ss, (ckt, cvt, clft), lru_past)

    y_prompt = y_p.reshape(bp, sp, D)
    y_sample = y_s.reshape(ss, bs, D).transpose(1, 0, 2)
    kt_all, vt_all, lft_all = fox_p[0]
    k_prompt = kt_all.reshape(n_fox, bp, H, HD, sp).transpose(0, 1, 4, 2, 3)
    v_prompt = vt_all.reshape(n_fox, bp, H, HD, sp).transpose(0, 1, 4, 2, 3)
    logf_prompt = lft_all.transpose(0, 1, 3, 2)
    h_prompt = jnp.stack([s[0][:, 0, :] for s in lru_p])
    conv_prompt = jnp.stack([s[1] for s in lru_p])
    k_sample = jnp.stack([f[0] for f in fox_s]).reshape(n_fox, bs, ss, H, HD)
    v_sample = jnp.stack([f[1] for f in fox_s]).reshape(n_fox, bs, ss, H, HD)
    logf_sample = jnp.stack([f[2] for f in fox_s])
    h_sample = jnp.stack([s[0][0] for s in lru_s])
    conv_sample = jnp.stack([s[1][0].reshape(CONV_W - 1, bs, D).transpose(1, 0, 2) for s in lru_s])
    return (y_prompt, y_sample, k_prompt, v_prompt, logf_prompt, h_prompt, conv_prompt,
            k_sample, v_sample, logf_sample, h_sample, conv_sample)
```

```python
import functools
import math

import jax
import jax.numpy as jnp
from jax import lax
from jax.experimental import pallas as pl
from jax.experimental.pallas import tpu as pltpu

F32 = jnp.float32
BF16 = jnp.bfloat16

D = 1024
H = 16
HD = 64
NPAIR = H // 2
LANES = 128
DFF = 3584
NEXP = 8
CONV_W = 4
LRU_C = 8.0
EPS = 1e-6
QK_SCALE = HD ** -0.5
LOG2E = math.log2(math.e)
NEG_INF = float("-inf")
VMEM_LIMIT = 56 * 1024 * 1024
TM_MIX = 512
TM_FFN = 512
TM_MOE = 512
TF = 1792
TF_MOE = 1792
TG = 512
SLAB = D // LANES
SCAN_ROWS = 8


def _cparams(n_axes):
    return pltpu.CompilerParams(dimension_semantics=("arbitrary",) * n_axes,
                                vmem_limit_bytes=VMEM_LIMIT)


def _dot(a, b):
    return jnp.dot(a, b, preferred_element_type=F32)


def _dot_nt(a, b):
    return lax.dot_general(a, b, (((1,), (1,)), ((), ())), preferred_element_type=F32)


def _norm_mod(x, g, sc, sh):
    ms = jnp.mean(x * x, axis=-1, keepdims=True)
    return (x * lax.rsqrt(ms + EPS) * g) * (1.0 + sc) + sh


def _log_sigmoid(x):
    return jnp.minimum(x, 0.0) - jnp.log1p(jnp.exp(-jnp.abs(x)))


def _sigmoid(x):
    return 0.5 * (1.0 + jnp.tanh(0.5 * x))


def _silu(x):
    return x * _sigmoid(x)


def _gelu_tanh(x):
    return 0.5 * x * (1.0 + jnp.tanh(math.sqrt(2.0 / math.pi) * (x + 0.044715 * (x * x * x))))


def _split3(x):
    hi = x.astype(BF16)
    r = x - hi.astype(F32)
    mid = r.astype(BF16)
    lo = (r - mid.astype(F32)).astype(BF16)
    return hi, mid, lo


def _cumsum_rows(x, n):
    r = lax.broadcasted_iota(jnp.int32, (n, n), 0)
    c = lax.broadcasted_iota(jnp.int32, (n, n), 1)
    tri = (c <= r).astype(BF16)
    hi, mid, lo = _split3(x)
    return _dot(tri, hi) + _dot(tri, mid) + _dot(tri, lo)


def _cumsum_lanes(x, n):
    r = lax.broadcasted_iota(jnp.int32, (n, n), 0)
    c = lax.broadcasted_iota(jnp.int32, (n, n), 1)
    tri = (r <= c).astype(BF16)
    hi, mid, lo = _split3(x)
    return _dot(hi, tri) + _dot(mid, tri) + _dot(lo, tri)


def _mod_kernel(c_ref, w_ref, b_ref, o_ref):
    c = c_ref[...]
    s = _silu(c).astype(BF16)
    o_ref[0] = _dot(s, w_ref[0].astype(BF16)) + b_ref[0]


def _ada_mod(c_all, w_ada, b_ada):
    nl, _, nmod = w_ada.shape
    rows = c_all.shape[0]
    tn = 1536
    return pl.pallas_call(
        _mod_kernel,
        grid=(nl, nmod // tn),
        in_specs=[
            pl.BlockSpec((rows, D), lambda l, j: (0, 0)),
            pl.BlockSpec((1, D, tn), lambda l, j: (l, 0, j)),
            pl.BlockSpec((1, 1, tn), lambda l, j: (l, 0, j)),
        ],
        out_specs=pl.BlockSpec((1, rows, tn), lambda l, j: (l, 0, j)),
        out_shape=jax.ShapeDtypeStruct((nl, rows, nmod), F32),
        compiler_params=_cparams(2),
    )(c_all, w_ada, b_ada.reshape(nl, 1, nmod))


class _Rows:
    def __init__(self, rows, tm, groups, per_row_mod):
        self.rows, self.tm, self.groups = rows, tm, groups
        self.n_tiles = rows // tm
        self.tpg = self.n_tiles // groups
        self.per_row_mod = per_row_mod

    def mod_spec(self, which):
        tpg, tm = self.tpg, self.tm
        if self.per_row_mod:
            return pl.BlockSpec((None, None, tm, D), lambda i, *_: (which, 0, i, 0))
        return pl.BlockSpec((None, None, 1, D), lambda i, *_: (which, i // tpg, 0, 0))


def _fox_in_sample_kernel(x_ref, g_ref, sc_ref, sh_ref, wq_ref, wk_ref, wv_ref, wf_ref, bf_ref,
                          k_ref, v_ref, lf_ref, q_ref):
    h = _norm_mod(x_ref[...], g_ref[...], sc_ref[...], sh_ref[...]).astype(BF16)
    q_ref[...] = _dot(h, wq_ref[...]) * QK_SCALE
    k_ref[...] = _dot(h, wk_ref[...])
    v_ref[...] = _dot(h, wv_ref[...])
    lf = _log_sigmoid(_dot(h, wf_ref[...]) + bf_ref[...])
    lf_ref[...] = lf[:, :H]


def _fox_in_sample(x, g, mod, wq, wk, wv, wf, bf, rt):
    tm = rt.tm
    full = lambda shape: pl.BlockSpec(shape, lambda i: (0,) * len(shape))
    row = pl.BlockSpec((tm, D), lambda i: (i, 0))
    rows_f32 = jax.ShapeDtypeStruct((rt.rows, D), F32)
    return pl.pallas_call(
        _fox_in_sample_kernel,
        grid=(rt.n_tiles,),
        in_specs=[row, full((1, D)), rt.mod_spec(1), rt.mod_spec(0),
                  full((D, D)), full((D, D)), full((D, D)), full((D, LANES)), full((1, LANES))],
        out_specs=[row, row, pl.BlockSpec((tm, H), lambda i: (i, 0)), row],
        out_shape=[rows_f32, rows_f32, jax.ShapeDtypeStruct((rt.rows, H), F32), rows_f32],
        compiler_params=_cparams(1),
    )(x, g, mod, mod, wq, wk, wv, wf, bf)


def _fox_in_prompt_kernel(x_ref, g_ref, sc_ref, sh_ref, wq_ref, wkt_ref, wvt_ref, wv_ref, wf_ref, bf_ref,
                          wft_ref, bft_ref, *rest, tm, tpg, n_prev):
    kt_ref, vt_ref, lft_ref, qa_ref, kat_ref, vp_ref, carry_r, carry_c = rest[n_prev:]
    i = pl.program_id(0)

    @pl.when(i % tpg == 0)
    def _():
        carry_r[...] = jnp.zeros_like(carry_r)
        carry_c[...] = jnp.zeros_like(carry_c)

    h = _norm_mod(x_ref[...], g_ref[...], sc_ref[...], sh_ref[...]).astype(BF16)
    q = _dot(h, wq_ref[...]) * (QK_SCALE * LOG2E)
    kt = _dot_nt(wkt_ref[...], h)
    kt_ref[0, 0] = kt
    vt_ref[0, 0] = _dot_nt(wvt_ref[...], h)
    vb = _dot(h, wv_ref[...]).astype(BF16)
    for hp in range(NPAIR):
        vp_ref[0, hp] = vb[:, hp * LANES:(hp + 1) * LANES]

    lf = _log_sigmoid(_dot(h, wf_ref[...]) + bf_ref[...])
    lft = _log_sigmoid(_dot_nt(wft_ref[...], h) + bft_ref[...])
    lft_ref[0, 0] = lft
    cum = _cumsum_rows(lf, tm) + carry_r[...]
    carry_r[...] = cum[tm - 1:tm, :]
    cumt = _cumsum_lanes(lft, tm) + carry_c[...]
    carry_c[...] = cumt[:, tm - 1:tm]
    hi, mid, lo = [p.astype(F32) for p in _split3(cum * LOG2E)]
    hit, midt, lot = [p.astype(F32) for p in _split3(cumt * LOG2E)]
    lane = lax.broadcasted_iota(jnp.int32, (tm, HD), 1)
    srow = lax.broadcasted_iota(jnp.int32, (HD, tm), 0)
    for hh in range(H):
        chi, cmid, clo = hi[:, hh:hh + 1], mid[:, hh:hh + 1], lo[:, hh:hh + 1]
        q_tail = jnp.where(lane == 0, chi, jnp.where(lane == 1, cmid, jnp.where(
            lane == 2, clo, jnp.where(lane < 6, 1.0, 0.0))))
        khi, kmid, klo = hit[hh:hh + 1, :], midt[hh:hh + 1, :], lot[hh:hh + 1, :]
        k_tail = jnp.where(srow < 3, 1.0, jnp.where(srow == 3, -khi, jnp.where(
            srow == 4, -kmid, jnp.where(srow == 5, -klo, 0.0))))
        qa_ref[0, hh, :, 0:HD] = q[:, hh * HD:(hh + 1) * HD].astype(BF16)
        qa_ref[0, hh, :, HD:2 * HD] = q_tail.astype(BF16)
        kat_ref[0, hh, 0, 0:HD, :] = kt[hh * HD:(hh + 1) * HD, :].astype(BF16)
        kat_ref[0, hh, 0, HD:2 * HD, :] = k_tail.astype(BF16)


def _fox_in_prompt(x, g, mod, wq, wkt, wvt, wv, wf, bf, wft, bft, rt, seq, layer, n_layers, prev):
    tm, tpg, b = rt.tm, rt.tpg, rt.groups
    full = lambda shape: pl.BlockSpec(shape, lambda i: (0,) * len(shape))
    row = pl.BlockSpec((tm, D), lambda i: (i, 0))
    if prev is None:
        prev = (jnp.zeros((n_layers, b, D, seq), F32), jnp.zeros((n_layers, b, D, seq), F32),
                jnp.zeros((n_layers, b, H, seq), F32))
    prev = tuple(prev)
    n_in = 12
    in_specs = [row, full((1, D)), rt.mod_spec(1), rt.mod_spec(0),
                full((D, D)), full((D, D)), full((D, D)), full((D, D)), full((D, LANES)),
                full((1, LANES)), full((H, D)), full((H, 1))]
    in_specs += [pl.BlockSpec(memory_space=pl.ANY)] * len(prev)
    state_spec = lambda rows: pl.BlockSpec((1, 1, rows, tm), lambda i: (layer, i // tpg, 0, i % tpg))
    out_specs = [state_spec(D), state_spec(D), state_spec(H),
                 pl.BlockSpec((1, H, tm, LANES), lambda i: (i // tpg, 0, i % tpg, 0)),
                 pl.BlockSpec((1, H, 1, LANES, tm), lambda i: (i // tpg, 0, i % tpg, 0, 0)),
                 pl.BlockSpec((1, NPAIR, tm, LANES), lambda i: (i // tpg, 0, i % tpg, 0))]
    out_shape = [jax.ShapeDtypeStruct((n_layers, b, D, seq), F32),
                 jax.ShapeDtypeStruct((n_layers, b, D, seq), F32),
                 jax.ShapeDtypeStruct((n_layers, b, H, seq), F32),
                 jax.ShapeDtypeStruct((b, H, seq, LANES), BF16),
                 jax.ShapeDtypeStruct((b, H, seq // tm, LANES, tm), BF16),
                 jax.ShapeDtypeStruct((b, NPAIR, seq, LANES), BF16)]
    return pl.pallas_call(
        functools.partial(_fox_in_prompt_kernel, tm=tm, tpg=tpg, n_prev=len(prev)),
        grid=(rt.n_tiles,),
        in_specs=in_specs, out_specs=out_specs, out_shape=out_shape,
        scratch_shapes=[pltpu.VMEM((1, LANES), F32), pltpu.VMEM((H, 1), F32)],
        input_output_aliases={n_in + t: t for t in range(len(prev))},
        compiler_params=_cparams(1),
    )(x, g, mod, mod, wq, wkt, wvt, wv, wf, bf, wft, bft, *prev)


def _attn_kernel(qa_ref, kat_ref, vp_ref, o_ref, m0, m1, l0, l1, a0, a1, sa0, sa1, sb0, sb1, *, tq, tk):
    qi = pl.program_id(2)
    m_s, l_s, acc_s, sa_s, sb_s = (m0, m1), (l0, l1), (a0, a1), (sa0, sa1), (sb0, sb1)
    q = [qa_ref[0, 0], qa_ref[0, 1]]
    for j in range(2):
        m_s[j][...] = jnp.full_like(m_s[j], NEG_INF)
        l_s[j][...] = jnp.zeros_like(l_s[j])
        acc_s[j][...] = jnp.zeros_like(acc_s[j])
        sa_s[j][...] = _dot(q[j], kat_ref[0, j, 0])

    def consume(j, kblk, s):
        v = vp_ref[0, 0, pl.ds(pl.multiple_of(kblk * tk, tk), tk), :]
        m_old = m_s[j][...]
        m_new = jnp.maximum(m_old, jnp.max(s, axis=-1, keepdims=True))
        alpha = jnp.exp2(m_old - m_new)
        p = jnp.exp2(s - jnp.concatenate([m_new] * (tk // LANES), axis=1))
        psum = p[:, 0:LANES]
        for t in range(1, tk // LANES):
            psum = psum + p[:, t * LANES:(t + 1) * LANES]
        l_s[j][...] = alpha * l_s[j][...] + psum
        acc_s[j][...] = alpha * acc_s[j][...] + _dot(p.astype(BF16), v)
        m_s[j][...] = m_new

    def scores(j, kblk):
        return _dot(q[j], kat_ref[0, j, kblk])

    def pair(k2, carry):
        kb = 2 * k2
        for j in range(2):
            sb_s[j][...] = scores(j, kb + 1)
            consume(j, kb, sa_s[j][...])
        for j in range(2):
            sa_s[j][...] = scores(j, kb + 2)
            consume(j, kb + 1, sb_s[j][...])
        return carry

    lax.fori_loop(0, qi // 2, pair, 0)

    def diagonal(j, s):
        r = lax.broadcasted_iota(jnp.int32, (tq, tk), 0)
        c = lax.broadcasted_iota(jnp.int32, (tq, tk), 1)
        consume(j, qi, jnp.where(c <= r, s, NEG_INF))

    @pl.when(qi % 2 == 1)
    def _():
        for j in range(2):
            sb_s[j][...] = scores(j, qi)
            consume(j, qi - 1, sa_s[j][...])
        for j in range(2):
            diagonal(j, sb_s[j][...])

    @pl.when(qi % 2 == 0)
    def _():
        for j in range(2):
            diagonal(j, sa_s[j][...])

    outs = [acc_s[j][...] / jnp.sum(l_s[j][...], axis=-1, keepdims=True) for j in range(2)]
    lane = lax.broadcasted_iota(jnp.int32, (tq, LANES), 1)
    o_ref[0, 0] = jnp.where(lane < HD, outs[0], outs[1]).astype(BF16)


def _attn_prompt(qa, kat, vp, tq):
    b, _, seq, _ = qa.shape
    nkb, tk = kat.shape[2], kat.shape[4]
    return pl.pallas_call(
        functools.partial(_attn_kernel, tq=tq, tk=tk),
        grid=(b, NPAIR, seq // tq),
        in_specs=[
            pl.BlockSpec((1, 2, tq, LANES), lambda bb, hp, qi: (bb, hp, qi, 0)),
            pl.BlockSpec((1, 2, nkb, LANES, tk), lambda bb, hp, qi: (bb, hp, 0, 0, 0)),
            pl.BlockSpec((1, 1, seq, LANES), lambda bb, hp, qi: (bb, hp, 0, 0)),
        ],
        out_specs=pl.BlockSpec((1, 1, tq, LANES), lambda bb, hp, qi: (bb, hp, qi, 0)),
        out_shape=jax.ShapeDtypeStruct((b, NPAIR, seq, LANES), BF16),
        scratch_shapes=[pltpu.VMEM((tq, LANES), F32)] * 6 + [pltpu.VMEM((tq, tk), F32)] * 4,
        compiler_params=_cparams(3),
    )(qa, kat, vp)


def _attn_sample_kernel(q_ref, kn_ref, vn_ref, lfn_ref, ck_ref, cv_ref, clf_ref, o_ref,
                        m_s, l_s, acc_s, carry_s, *, sq, tk, nkc):
    kc = pl.program_id(1)
    nrow = H * sq

    @pl.when(kc == 0)
    def _():
        m_s[...] = jnp.full_like(m_s, NEG_INF)
        l_s[...] = jnp.zeros_like(l_s)
        acc_s[...] = jnp.zeros_like(acc_s)
        carry_s[...] = jnp.zeros_like(carry_s)

    rowh = lax.broadcasted_iota(jnp.int32, (nrow, D), 0) // sq
    colh = lax.broadcasted_iota(jnp.int32, (nrow, D), 1) // HD
    q = q_ref[0]
    qbd = jnp.where(rowh == colh, jnp.concatenate([q] * H, axis=0), 0.0).astype(BF16)

    def expand_heads(ct):
        n = ct.shape[1]
        return jnp.concatenate([jnp.broadcast_to(ct[hh:hh + 1, :], (sq, n)) for hh in range(H)], axis=0)

    def update(s, vb, v_transposed):
        m_old = m_s[...]
        m_new = jnp.maximum(m_old, jnp.max(s, axis=-1, keepdims=True))
        alpha = jnp.exp(m_old - m_new)
        p = jnp.exp(s - m_new)
        l_s[...] = alpha * l_s[...] + jnp.sum(p, axis=-1, keepdims=True)
        pb = p.astype(BF16)
        pv = _dot_nt(pb, vb) if v_transposed else _dot(pb, vb)
        acc_s[...] = alpha * acc_s[...] + pv
        m_s[...] = m_new

    sub = 256
    cums = []
    carry = carry_s[...]
    for t in range(tk // sub):
        cpart = _cumsum_lanes(clf_ref[0, :, t * sub:(t + 1) * sub], sub) + carry
        carry = cpart[:, sub - 1:sub]
        cums.append(cpart)
    carry_s[...] = carry
    c_keys = jnp.concatenate(cums, axis=1)
    s = _dot(qbd, ck_ref[0].astype(BF16)) - expand_heads(c_keys)
    update(s, cv_ref[0].astype(BF16), True)

    @pl.when(kc == nkc - 1)
    def _():
        c_new = _cumsum_lanes(lfn_ref[0], LANES) + carry
        pad = jnp.zeros((LANES - sq, D), F32)
        kn = jnp.concatenate([kn_ref[0], pad], axis=0).astype(BF16)
        vn = jnp.concatenate([vn_ref[0], pad], axis=0).astype(BF16)
        c_rows = expand_heads(c_new)
        rq = lax.broadcasted_iota(jnp.int32, (nrow, LANES), 0) % sq
        kidx = lax.broadcasted_iota(jnp.int32, (nrow, LANES), 1)
        cq = jnp.sum(jnp.where(kidx == rq, c_rows, 0.0), axis=-1, keepdims=True)
        s_new = _dot_nt(qbd, kn) + (cq - c_rows)
        s_new = jnp.where(kidx <= rq, s_new, NEG_INF)
        m_s[...] = m_s[...] + cq
        update(s_new, vn, False)
        r = acc_s[...] / l_s[...]
        r = jnp.where(rowh == colh, r, 0.0)
        out = r[0:sq, :]
        for hh in range(1, H):
            out = out + r[hh * sq:(hh + 1) * sq, :]
        o_ref[0] = out


def _attn_sample(q, k_new, v_new, lf_new, cache_k, cache_v, cache_lf, layer):
    nb, sq, _ = q.shape
    past = cache_k.shape[3]
    tk = 1024
    nkc = past // tk
    cur = pl.BlockSpec((1, sq, D), lambda b, kc: (b, 0, 0))
    return pl.pallas_call(
        functools.partial(_attn_sample_kernel, sq=sq, tk=tk, nkc=nkc),
        grid=(nb, nkc),
        in_specs=[cur, cur, cur,
                  pl.BlockSpec((1, H, LANES), lambda b, kc: (b, 0, 0)),
                  pl.BlockSpec((None, 1, D, tk), lambda b, kc: (layer, b, 0, kc)),
                  pl.BlockSpec((None, 1, D, tk), lambda b, kc: (layer, b, 0, kc)),
                  pl.BlockSpec((None, 1, H, tk), lambda b, kc: (layer, b, 0, kc))],
        out_specs=cur,
        out_shape=jax.ShapeDtypeStruct((nb, sq, D), F32),
        scratch_shapes=[pltpu.VMEM((H * sq, 1), F32), pltpu.VMEM((H * sq, 1), F32),
                        pltpu.VMEM((H * sq, D), F32), pltpu.VMEM((H, 1), F32)],
        compiler_params=_cparams(2),
    )(q, k_new, v_new, lf_new, cache_k, cache_v, cache_lf)


def _lru_kernel(x_ref, g_ref, sc_ref, sh_ref, wx_ref, wg_ref, cw_ref, cb_ref, wa_ref, ba_ref,
                wi_ref, bi_ref, lam_ref, h0_ref, c0_ref, y_ref, hl_ref, cn_ref, xp_s, hc_s,
                *, tm, tpg, stride, hdr):
    i = pl.program_id(0)
    nst = (CONV_W - 1) * stride

    @pl.when(i % tpg == 0)
    def _():
        xp_s[hdr - nst:hdr, :] = c0_ref[0]
        hc_s[...] = h0_ref[0]

    h = _norm_mod(x_ref[...], g_ref[...], sc_ref[...], sh_ref[...]).astype(BF16)
    xb = _dot(h, wx_ref[...])
    gb = _dot(h, wg_ref[...])
    xp_s[hdr:hdr + tm, :] = xb
    xc = cb_ref[...] + xb * cw_ref[CONV_W - 1:CONV_W, :]
    for j in range(CONV_W - 1):
        d = (CONV_W - 1 - j) * stride
        shifted = jnp.concatenate([xp_s[hdr - d:2 * hdr - d, :], pltpu.roll(xb, d, 0)[hdr:, :]], axis=0)
        xc = xc + shifted * cw_ref[j:j + 1, :]
    tail = xp_s[hdr + tm - nst:hdr + tm, :]
    cn_ref[0] = tail
    xp_s[hdr - nst:hdr, :] = tail

    xcb = xc.astype(BF16)
    r = _sigmoid(_dot(xcb, wa_ref[...]) + ba_ref[...])
    gi = _sigmoid(_dot(xcb, wi_ref[...]) + bi_ref[...])
    log_a = LRU_C * r * _log_sigmoid(lam_ref[...])
    a = jnp.exp(log_a)
    u = jnp.sqrt(-jnp.tanh(log_a) * (1.0 + a * a)) * gi * xc

    cr = min(SCAN_ROWS, tm) if stride == 1 else tm
    row = lax.broadcasted_iota(jnp.int32, (tm, 1), 0) % cr
    sft = stride
    while sft < cr:
        keep = row >= sft
        a_sh = pltpu.roll(a, sft, 0)
        u_sh = pltpu.roll(u, sft, 0)
        u = jnp.where(keep, a * u_sh + u, u)
        a = jnp.where(keep, a * a_sh, a)
        sft *= 2
    hc = hc_s[...]
    if stride == 1:
        parts = []
        for c in range(tm // cr):
            part = a[c * cr:(c + 1) * cr, :] * hc + u[c * cr:(c + 1) * cr, :]
            hc = part[cr - 1:cr, :]
            parts.append(part)
        hs = jnp.concatenate(parts, axis=0)
    else:
        hs = a * jnp.concatenate([hc] * (tm // stride), axis=0) + u
    last = hs[tm - stride:tm, :]
    hc_s[...] = last
    hl_ref[0] = last
    y_ref[...] = (hs * _gelu_tanh(gb)).astype(BF16)


def _lru_mix(x, g, mod, wx, wg, cw, cb, wa, ba, wi, bi, lam, h0, c0, rt, stride):
    tm, tpg = rt.tm, rt.tpg
    hdr = -(-(CONV_W - 1) * stride // 8) * 8
    nst = (CONV_W - 1) * stride
    full = lambda shape: pl.BlockSpec(shape, lambda i: (0,) * len(shape))
    row = pl.BlockSpec((tm, D), lambda i: (i, 0))
    grp = rt.groups
    return pl.pallas_call(
        functools.partial(_lru_kernel, tm=tm, tpg=tpg, stride=stride, hdr=hdr),
        grid=(rt.n_tiles,),
        in_specs=[row, full((1, D)), rt.mod_spec(1), rt.mod_spec(0),
                  full((D, D)), full((D, D)), full((CONV_W, D)), full((1, D)),
                  full((D, D)), full((1, D)), full((D, D)), full((1, D)), full((1, D)),
                  pl.BlockSpec((1, stride, D), lambda i: (i // tpg, 0, 0)),
                  pl.BlockSpec((1, nst, D), lambda i: (i // tpg, 0, 0))],
        out_specs=[row,
                   pl.BlockSpec((1, stride, D), lambda i: (i // tpg, 0, 0)),
                   pl.BlockSpec((1, nst, D), lambda i: (i // tpg, 0, 0))],
        out_shape=[jax.ShapeDtypeStruct((rt.rows, D), BF16),
                   jax.ShapeDtypeStruct((grp, stride, D), F32),
                   jax.ShapeDtypeStruct((grp, nst, D), F32)],
        scratch_shapes=[pltpu.VMEM((hdr + tm, D), F32), pltpu.VMEM((stride, D), F32)],
        compiler_params=_cparams(1),
    )(x, g, mod, mod, wx, wg, cw, cb, wa, ba, wi, bi, lam, h0, c0)


def _mix_in(a_ref, pair_major):
    if pair_major:
        return jnp.concatenate([a_ref[0, hp] for hp in range(NPAIR)], axis=1)
    return a_ref[...]


def _ffn_kernel(x_ref, a_ref, ga_ref, wo_ref, g2_ref, sc_ref, sh_ref, gm_ref, wg_ref, wu_ref, wd_ref,
                o_ref, x1_s, h_s, acc_s, *, nf, pair_major):
    f = pl.program_id(1)

    @pl.when(f == 0)
    def _():
        a = _mix_in(a_ref, pair_major).astype(BF16)
        x1 = x_ref[...] + ga_ref[...] * _dot(a, wo_ref[...])
        x1_s[...] = x1
        h_s[...] = _norm_mod(x1, g2_ref[...], sc_ref[...], sh_ref[...]).astype(BF16)
        acc_s[...] = jnp.zeros_like(acc_s)

    h = h_s[...]
    act = (_silu(_dot(h, wg_ref[...])) * _dot(h, wu_ref[...])).astype(BF16)
    acc_s[...] += _dot(act, wd_ref[...])

    @pl.when(f == nf - 1)
    def _():
        o_ref[...] = x1_s[...] + gm_ref[...] * acc_s[...]


def _route_kernel(x_ref, a_ref, ga_ref, wo_ref, g2_ref, sc_ref, sh_ref, wr_ref,
                  x1_ref, h_ref, route_ref, gate_ref, cnt_ref, carry_s, *, tm):
    i = pl.program_id(0)

    @pl.when(i == 0)
    def _():
        carry_s[...] = jnp.zeros_like(carry_s)

    lane = lax.broadcasted_iota(jnp.int32, (tm, LANES), 1)
    x1 = x_ref[...] + ga_ref[...] * _dot(a_ref[...].astype(BF16), wo_ref[...])
    x1_ref[...] = x1
    hf = _norm_mod(x1, g2_ref[...], sc_ref[...], sh_ref[...])
    _to_slabs(h_ref, hf, tm)
    logits = jnp.dot(hf, wr_ref[...], preferred_element_type=F32, precision=lax.Precision.HIGHEST)
    logits = jnp.where(lane < NEXP, logits, NEG_INF)
    m1 = jnp.max(logits, axis=-1, keepdims=True)
    i1 = jnp.min(jnp.where(logits == m1, lane, LANES), axis=-1, keepdims=True)
    rest = jnp.where(lane == i1, NEG_INF, logits)
    m2 = jnp.max(rest, axis=-1, keepdims=True)
    i2 = jnp.min(jnp.where(rest == m2, lane, LANES), axis=-1, keepdims=True)
    t = jnp.exp(m2 - m1)
    gate_ref[...] = jnp.where(lane == 0, 1.0 / (1.0 + t), jnp.where(lane == 1, t / (1.0 + t), 0.0))
    sel = (lane == i1) | (lane == i2)
    r = lax.broadcasted_iota(jnp.int32, (tm, tm), 0)
    c = lax.broadcasted_iota(jnp.int32, (tm, tm), 1)
    before = _dot((c < r).astype(BF16), sel.astype(BF16)) + carry_s[...]
    rank1 = jnp.sum(jnp.where(lane == i1, before, 0.0), axis=-1, keepdims=True).astype(jnp.int32)
    rank2 = jnp.sum(jnp.where(lane == i2, before, 0.0), axis=-1, keepdims=True).astype(jnp.int32)
    route_ref[...] = jnp.where(lane == 0, i1, jnp.where(lane == 1, i2, jnp.where(
        lane == 2, rank1, jnp.where(lane == 3, rank2, 0))))
    total = carry_s[...] + jnp.sum(sel.astype(F32), axis=0, keepdims=True)
    carry_s[...] = total
    cnt_ref[...] = total.astype(jnp.int32)


def _to_slabs(ref, x, n):
    for c in range(SLAB):
        ref[pl.ds(c, n, stride=SLAB), :] = x[:, c * LANES:(c + 1) * LANES]


def _from_slabs(ref, n):
    return jnp.concatenate([ref[pl.ds(c, n, stride=SLAB), :] for c in range(SLAB)], axis=1)


def _slab(ref, t):
    return ref.at[pl.ds(pl.multiple_of(SLAB * t, SLAB), SLAB), :]


def _dispatch_kernel(dest_ref, h_ref, zero_ref, o_ref, sem, *, tm):
    del zero_ref
    base = pl.program_id(0) * tm

    def issue(r, carry):
        for s in range(2):
            d = dest_ref[2 * (base + r) + s]
            pltpu.make_async_copy(_slab(h_ref, r), _slab(o_ref, d), sem.at[s]).start()
        return carry

    lax.fori_loop(0, tm, issue, 0, unroll=4)
    for s in range(2):
        pltpu.make_async_copy(h_ref, o_ref.at[pl.ds(0, tm * SLAB), :], sem.at[s]).wait()


def _experts_kernel(te_ref, nu_ref, hs_ref, wg_ref, wu_ref, wd_ref, o_ref, hb_s, acc_s, *, nf, tg):
    del te_ref
    i = pl.program_id(0)
    f = pl.program_id(1)

    @pl.when(i < nu_ref[0])
    def _():
        @pl.when(f == 0)
        def _():
            hb_s[...] = _from_slabs(hs_ref, tg).astype(BF16)
            acc_s[...] = jnp.zeros_like(acc_s)

        h = hb_s[...]
        act = (_silu(_dot(h, wg_ref[0, 0])) * _dot(h, wu_ref[0, 0])).astype(BF16)
        acc_s[...] += _dot(act, wd_ref[0, 0])

        @pl.when(f == nf - 1)
        def _():
            _to_slabs(o_ref, acc_s[...], tg)

    @pl.when((i >= nu_ref[0]) & (f == 0))
    def _():
        o_ref[...] = jnp.zeros_like(o_ref)


def _combine_kernel(dest_ref, x1_ref, gm_ref, gate_ref, gf_ref, y_ref, o_ref, b0_s, b1_s, sem,
                    *, tm, final_norm):
    base = pl.program_id(0) * tm
    bufs = (b0_s, b1_s)

    def issue(r, carry):
        for s in range(2):
            d = dest_ref[2 * (base + r) + s]
            pltpu.make_async_copy(_slab(y_ref, d), _slab(bufs[s], r), sem.at[s]).start()
        return carry

    lax.fori_loop(0, tm, issue, 0, unroll=4)
    for s in range(2):
        pltpu.make_async_copy(y_ref.at[pl.ds(0, tm * SLAB), :], bufs[s], sem.at[s]).wait()
    gate = gate_ref[...]
    y = gate[:, 0:1] * _from_slabs(b0_s, tm) + gate[:, 1:2] * _from_slabs(b1_s, tm)
    x2 = x1_ref[...] + gm_ref[...] * y
    if final_norm:
        ms = jnp.mean(x2 * x2, axis=-1, keepdims=True)
        x2 = x2 * lax.rsqrt(ms + EPS) * gf_ref[...]
    o_ref[...] = x2


def _post_ffn(x, a, mod, wo, g2, w_gu, w_down, layer, rt, pair_major):
    tm, tpg = rt.tm, rt.tpg
    nf = DFF // TF
    row = pl.BlockSpec((tm, D), lambda i, f: (i, 0))
    if pair_major:
        a_spec = pl.BlockSpec((1, NPAIR, tm, LANES), lambda i, f: (i // tpg, 0, i % tpg, 0))
    else:
        a_spec = row
    c2 = lambda shape: pl.BlockSpec(shape, lambda i, f: (0,) * len(shape))
    return pl.pallas_call(
        functools.partial(_ffn_kernel, nf=nf, pair_major=pair_major),
        grid=(rt.n_tiles, nf),
        in_specs=[row, a_spec, rt.mod_spec(2), c2((D, D)), c2((1, D)),
                  rt.mod_spec(4), rt.mod_spec(3), rt.mod_spec(5),
                  pl.BlockSpec((None, D, TF), lambda i, f: (layer, 0, f)),
                  pl.BlockSpec((None, D, TF), lambda i, f: (layer, 0, nf + f)),
                  pl.BlockSpec((None, TF, D), lambda i, f: (layer, f, 0))],
        out_specs=row,
        out_shape=jax.ShapeDtypeStruct((rt.rows, D), F32),
        scratch_shapes=[pltpu.VMEM((tm, D), F32), pltpu.VMEM((tm, D), BF16), pltpu.VMEM((tm, D), F32)],
        compiler_params=_cparams(2),
    )(x, a, mod, wo, g2, mod, mod, mod, w_gu, w_gu, w_down)


def _post_moe(x, a, mod, wo, g2, w_router, w_gu, w_down, layer, rt, final_g):
    tm, rows, n_tiles = rt.tm, rt.rows, rt.n_tiles
    nf = DFF // TF_MOE
    row = pl.BlockSpec((tm, D), lambda i, *_: (i, 0))
    lanes = pl.BlockSpec((tm, LANES), lambda i, *_: (i, 0))
    full = lambda shape: pl.BlockSpec(shape, lambda i, *_: (0,) * len(shape))
    rows_f32 = jax.ShapeDtypeStruct((rows, D), F32)
    slabs = lambda n: pl.BlockSpec((n * SLAB, LANES), lambda i, *_: (i, 0))
    tg = min(TG, rows // 2)
    x1, hf, route, gate, cnt = pl.pallas_call(
        functools.partial(_route_kernel, tm=tm),
        grid=(n_tiles,),
        in_specs=[row, row, rt.mod_spec(2), full((D, D)), full((1, D)), rt.mod_spec(4), rt.mod_spec(3),
                  full((D, LANES))],
        out_specs=[row, slabs(tm), lanes, lanes, full((1, LANES))],
        out_shape=[rows_f32, jax.ShapeDtypeStruct((rows * SLAB, LANES), F32),
                   jax.ShapeDtypeStruct((rows, LANES), jnp.int32),
                   jax.ShapeDtypeStruct((rows, LANES), F32), jax.ShapeDtypeStruct((1, LANES), jnp.int32)],
        scratch_shapes=[pltpu.VMEM((1, LANES), F32)],
        compiler_params=_cparams(1),
    )(x, a, mod, wo, g2, mod, mod, w_router)

    n_tiles_g = -(-(2 * rows + NEXP * (tg - 1)) // tg)
    tiles_e = (cnt[0, :NEXP] + (tg - 1)) // tg
    tile_end = jnp.cumsum(tiles_e)
    start = (tile_end - tiles_e) * tg
    is_e = route[:, 0:2, None] == jnp.arange(NEXP, dtype=jnp.int32)
    dest = (route[:, 2:4] + jnp.sum(jnp.where(is_e, start, 0), axis=-1)).reshape(-1).astype(jnp.int32)
    tile_ids = jnp.arange(n_tiles_g, dtype=jnp.int32)
    tile_expert = jnp.minimum(jnp.sum(tile_ids[:, None] >= tile_end[None, :], axis=1), NEXP - 1)
    tile_expert = tile_expert.astype(jnp.int32)
    n_used = tile_end[NEXP - 1:].astype(jnp.int32)

    sorted_h = pl.pallas_call(
        functools.partial(_dispatch_kernel, tm=tm),
        grid_spec=pltpu.PrefetchScalarGridSpec(
            num_scalar_prefetch=1, grid=(n_tiles,),
            in_specs=[slabs(tm), pl.BlockSpec(memory_space=pl.ANY)],
            out_specs=pl.BlockSpec(memory_space=pl.ANY),
            scratch_shapes=[pltpu.SemaphoreType.DMA((2,))]),
        out_shape=jax.ShapeDtypeStruct((n_tiles_g * tg * SLAB, LANES), F32),
        input_output_aliases={2: 0},
        compiler_params=_cparams(1),
    )(dest, hf, jnp.zeros((n_tiles_g * tg * SLAB, LANES), F32))

    def tile_of(i, nu):
        return jnp.maximum(jnp.minimum(i, nu[0] - 1), 0)

    def chunk_of(i, f, nu):
        return jnp.where(i < nu[0], f, nf - 1)

    y_sorted = pl.pallas_call(
        functools.partial(_experts_kernel, nf=nf, tg=tg),
        grid_spec=pltpu.PrefetchScalarGridSpec(
            num_scalar_prefetch=2, grid=(n_tiles_g, nf),
            in_specs=[
                pl.BlockSpec((tg * SLAB, LANES), lambda i, f, te, nu: (tile_of(i, nu), 0)),
                pl.BlockSpec((1, 1, D, TF_MOE),
                             lambda i, f, te, nu: (layer, te[tile_of(i, nu)], 0, chunk_of(i, f, nu))),
                pl.BlockSpec((1, 1, D, TF_MOE),
                             lambda i, f, te, nu: (layer, te[tile_of(i, nu)], 0, nf + chunk_of(i, f, nu))),
                pl.BlockSpec((1, 1, TF_MOE, D),
                             lambda i, f, te, nu: (layer, te[tile_of(i, nu)], chunk_of(i, f, nu), 0))],
            out_specs=pl.BlockSpec((tg * SLAB, LANES), lambda i, f, te, nu: (i, 0)),
            scratch_shapes=[pltpu.VMEM((tg, D), BF16), pltpu.VMEM((tg, D), F32)]),
        out_shape=jax.ShapeDtypeStruct((n_tiles_g * tg * SLAB, LANES), F32),
        compiler_params=_cparams(2),
    )(tile_expert, n_used, sorted_h, w_gu, w_gu, w_down)

    gf = jnp.ones((1, D), F32) if final_g is None else final_g
    return pl.pallas_call(
        functools.partial(_combine_kernel, tm=tm, final_norm=final_g is not None),
        grid_spec=pltpu.PrefetchScalarGridSpec(
            num_scalar_prefetch=1, grid=(n_tiles,),
            in_specs=[row, rt.mod_spec(5), lanes, full((1, D)), pl.BlockSpec(memory_space=pl.ANY)],
            out_specs=row,
            scratch_shapes=[pltpu.VMEM((tm * SLAB, LANES), F32), pltpu.VMEM((tm * SLAB, LANES), F32),
                            pltpu.SemaphoreType.DMA((2,))]),
        out_shape=rows_f32,
        compiler_params=_cparams(1),
    )(dest, x1, mod, gate, gf, y_sorted)


def _final_norm_kernel(x_ref, g_ref, o_ref):
    x = x_ref[...]
    ms = jnp.mean(x * x, axis=-1, keepdims=True)
    o_ref[...] = x * lax.rsqrt(ms + EPS) * g_ref[...]


def _final_norm(x, g, tm):
    rows = x.shape[0]
    return pl.pallas_call(
        _final_norm_kernel,
        grid=(rows // tm,),
        in_specs=[pl.BlockSpec((tm, D), lambda i: (i, 0)), pl.BlockSpec((1, D), lambda i: (0, 0))],
        out_specs=pl.BlockSpec((tm, D), lambda i: (i, 0)),
        out_shape=jax.ShapeDtypeStruct((rows, D), F32),
        compiler_params=_cparams(1),
    )(x, g)


def _prep_weights(p):
    w = {}
    fw = p['fox_w_in']
    w['wq'] = fw[:, :, 0 * D:1 * D].astype(BF16)
    w['wk'] = fw[:, :, 1 * D:2 * D].astype(BF16)
    w['wv'] = fw[:, :, 2 * D:3 * D].astype(BF16)
    w['wf'] = jnp.pad(fw[:, :, 3 * D:], ((0, 0), (0, 0), (0, LANES - H))).astype(BF16)
    w['bf'] = jnp.pad(p['fox_b_f'], ((0, 0), (0, LANES - H)))[:, None, :]
    w['wkt'] = w['wk'].transpose(0, 2, 1)
    w['wvt'] = w['wv'].transpose(0, 2, 1)
    w['wft'] = fw[:, :, 3 * D:].transpose(0, 2, 1).astype(BF16)
    w['bft'] = p['fox_b_f'][:, :, None]
    w['fox_wo'] = p['fox_w_out'].astype(BF16)
    w['lru_wx'] = p['lru_w_in'][:, :, :D].astype(BF16)
    w['lru_wg'] = p['lru_w_in'][:, :, D:].astype(BF16)
    nblk = p['lru_w_a'].shape[1]
    eye = jnp.eye(nblk, dtype=F32)
    bd = lambda m: jnp.einsum('lnkj,nm->lnkmj', m, eye).reshape(m.shape[0], D, D).astype(BF16)
    w['lru_wa'] = bd(p['lru_w_a'])
    w['lru_wi'] = bd(p['lru_w_i'])
    w['lru_wo'] = p['lru_w_out'].astype(BF16)
    w['ffn_gu'] = p['ffn_w_gu'].astype(BF16)
    w['ffn_down'] = p['ffn_w_down'].astype(BF16)
    w['router'] = jnp.pad(p['moe_router'], ((0, 0), (0, 0), (0, LANES - NEXP)))
    w['moe_gu'] = p['moe_w_gu'].astype(BF16)
    w['moe_down'] = p['moe_w_down'].astype(BF16)
    return w


def _trunk(x, mods, p, w, rts, prompt, seq, fox_past, lru_past):
    rt, rt_ffn, rt_moe = rts
    depth = p['norm_mix_g'].shape[0]
    n_fox = w['wq'].shape[0]
    final_g = p['final_g'][None, :]
    fox_new, lru_new = [], []
    for l in range(depth):
        j = l // 2
        last = l == depth - 1
        mod = mods[l]
        g1 = p['norm_mix_g'][l][None, :]
        g2 = p['norm_ffn_g'][l][None, :]
        if l % 2 == 0:
            if prompt:
                kt, vt, lft, qa, kat, vp = _fox_in_prompt(
                    x, g1, mod, w['wq'][j], w['wkt'][j], w['wvt'][j], w['wv'][j], w['wf'][j], w['bf'][j],
                    w['wft'][j], w['bft'][j], rt, seq, j, n_fox, fox_new[0] if fox_new else None)
                a = _attn_prompt(qa, kat, vp, tq=512)
                fox_new = [(kt, vt, lft)]
            else:
                k, v, lf, q = _fox_in_sample(x, g1, mod, w['wq'][j], w['wk'][j], w['wv'][j], w['wf'][j],
                                             w['bf'][j], rt)
                nb = rt.rows // seq
                to_b = lambda t: t.reshape(seq, nb, -1).transpose(1, 0, 2)
                kb, vb, lfb = to_b(k), to_b(v), to_b(lf)
                ck, cv, clf = fox_past
                lfn_t = jnp.pad(lfb.transpose(0, 2, 1), ((0, 0), (0, 0), (0, LANES - seq)))
                ab = _attn_sample(to_b(q), kb, vb, lfn_t, ck, cv, clf, j)
                a = ab.transpose(1, 0, 2).reshape(rt.rows, D)
                fox_new.append((kb, vb, lfb))
            x = _post_ffn(x, a, mod, w['fox_wo'][j], g2, w['ffn_gu'], w['ffn_down'], j, rt_ffn,
                          pair_major=prompt)
            if last:
                x = _final_norm(x, final_g, rt.tm)
        else:
            h0, c0, stride = lru_past[j]
            y, hl, cn = _lru_mix(x, g1, mod, w['lru_wx'][j], w['lru_wg'][j], p['lru_conv_w'][j],
                                 p['lru_conv_b'][j][None, :], w['lru_wa'][j], p['lru_b_a'][j][None, :],
                                 w['lru_wi'][j], p['lru_b_i'][j][None, :], p['lru_lam'][j][None, :],
                                 h0, c0, rt, stride)
            lru_new.append((hl, cn))
            x = _post_moe(x, y, mod, w['lru_wo'][j], g2, w['router'][j], w['moe_gu'], w['moe_down'], j,
                          rt_moe, final_g if last else None)
    return x, fox_new, lru_new


def kernel(x_prompt, x_sample, c_prompt, c_sample, cache_k, cache_v, cache_logf, state_h, state_conv,
           norm_mix_g, norm_ffn_g, final_g, w_ada, b_ada, fox_w_in, fox_b_f, fox_w_out,
           lru_w_in, lru_conv_w, lru_conv_b, lru_w_a, lru_b_a, lru_w_i, lru_b_i, lru_lam, lru_w_out,
           ffn_w_gu, ffn_w_down, moe_router, moe_w_gu, moe_w_down):
    p = {
        'norm_mix_g': norm_mix_g, 'norm_ffn_g': norm_ffn_g, 'final_g': final_g,
        'fox_w_in': fox_w_in, 'fox_b_f': fox_b_f, 'fox_w_out': fox_w_out,
        'lru_w_in': lru_w_in, 'lru_conv_w': lru_conv_w, 'lru_conv_b': lru_conv_b,
        'lru_w_a': lru_w_a, 'lru_b_a': lru_b_a, 'lru_w_i': lru_w_i, 'lru_b_i': lru_b_i,
        'lru_lam': lru_lam, 'lru_w_out': lru_w_out,
        'ffn_w_gu': ffn_w_gu, 'ffn_w_down': ffn_w_down,
        'moe_router': moe_router, 'moe_w_gu': moe_w_gu, 'moe_w_down': moe_w_down,
    }
    bp, sp, _ = x_prompt.shape
    bs, ss, _ = x_sample.shape
    depth = w_ada.shape[0]
    n_fox, n_lru = cache_k.shape[0], state_h.shape[0]
    past = cache_k.shape[2]
    w = _prep_weights(p)

    nc = bp + bs
    ncp = -(-nc // 8) * 8
    c_all = jnp.pad(jnp.concatenate([c_prompt, c_sample], axis=0), ((0, ncp - nc), (0, 0)))
    mod_all = _ada_mod(c_all, w_ada, b_ada)
    mods_p = [mod_all[l, :bp].reshape(bp, 6, 1, D).transpose(1, 0, 2, 3) for l in range(depth)]
    mods_s = [jnp.tile(mod_all[l, bp:nc].reshape(bs, 6, D).transpose(1, 0, 2), (1, ss, 1))[:, None]
              for l in range(depth)]

    rts_p = tuple(_Rows(bp * sp, tm, bp, per_row_mod=False) for tm in (TM_MIX, TM_FFN, TM_MOE))
    zeros_h = jnp.zeros((bp, 1, D), F32)
    zeros_c = jnp.zeros((bp, CONV_W - 1, D), F32)
    y_p, fox_p, lru_p = _trunk(x_prompt.reshape(bp * sp, D), mods_p, p, w, rts_p, True, sp,
                               None, [(zeros_h, zeros_c, 1)] * n_lru)

    rt_s = _Rows(bs * ss, bs * ss, 1, per_row_mod=True)
    xs = x_sample.transpose(1, 0, 2).reshape(ss * bs, D)
    ckt = cache_k.transpose(0, 1, 3, 4, 2).reshape(n_fox, bs, D, past)
    cvt = cache_v.transpose(0, 1, 3, 4, 2).reshape(n_fox, bs, D, past)
    clft = cache_logf.transpose(0, 1, 3, 2)
    lru_past = [(state_h[j][None], state_conv[j].transpose(1, 0, 2).reshape(1, (CONV_W - 1) * bs, D), bs)
                for j in range(n_lru)]
    y_s, fox_s, lru_s = _trunk(xs, mods_s, p, w, (rt_s, rt_s, rt_s), False, ss, (ckt, cvt, clft), lru_past)

    y_prompt = y_p.reshape(bp, sp, D)
    y_sample = y_s.reshape(ss, bs, D).transpose(1, 0, 2)
    kt_all, vt_all, lft_all = fox_p[0]
    k_prompt = kt_all.reshape(n_fox, bp, H, HD, sp).transpose(0, 1, 4, 2, 3)
    v_prompt = vt_all.reshape(n_fox, bp, H, HD, sp).transpose(0, 1, 4, 2, 3)
    logf_prompt = lft_all.transpose(0, 1, 3, 2)
    h_prompt = jnp.stack([s[0][:, 0, :] for s in lru_p])
    conv_prompt = jnp.stack([s[1] for s in lru_p])
    k_sample = jnp.stack([f[0] for f in fox_s]).reshape(n_fox, bs, ss, H, HD)
    v_sample = jnp.stack([f[1] for f in fox_s]).reshape(n_fox, bs, ss, H, HD)
    logf_sample = jnp.stack([f[2] for f in fox_s])
    h_sample = jnp.stack([s[0][0] for s in lru_s])
    conv_sample = jnp.stack([s[1][0].reshape(CONV_W - 1, bs, D).transpose(1, 0, 2) for s in lru_s])
    return (y_prompt, y_sample, k_prompt, v_prompt, logf_prompt, h_prompt, conv_prompt,
            k_sample, v_sample, logf_sample, h_sample, conv_sample)
```

```python
import functools
import math

import jax
import jax.numpy as jnp
from jax import lax
from jax.experimental import pallas as pl
from jax.experimental.pallas import tpu as pltpu

F32 = jnp.float32
BF16 = jnp.bfloat16

D = 1024
H = 16
HD = 64
NPAIR = H // 2
LANES = 128
DFF = 3584
NEXP = 8
CONV_W = 4
LRU_C = 8.0
EPS = 1e-6
QK_SCALE = HD ** -0.5
LOG2E = math.log2(math.e)
NEG_INF = float("-inf")
VMEM_LIMIT = 56 * 1024 * 1024
TM_MIX = 512
TM_FFN = 512
TM_MOE = 512
TF = 1792
TF_MOE = 1792
TG = 512
SLAB = D // LANES
SCAN_ROWS = 8


def _cparams(n_axes):
    return pltpu.CompilerParams(dimension_semantics=("arbitrary",) * n_axes,
                                vmem_limit_bytes=VMEM_LIMIT)


def _dot(a, b):
    return jnp.dot(a, b, preferred_element_type=F32)


def _dot_nt(a, b):
    return lax.dot_general(a, b, (((1,), (1,)), ((), ())), preferred_element_type=F32)


def _norm_mod(x, g, sc, sh):
    ms = jnp.mean(x * x, axis=-1, keepdims=True)
    return (x * lax.rsqrt(ms + EPS) * g) * (1.0 + sc) + sh


def _log_sigmoid(x):
    return jnp.minimum(x, 0.0) - jnp.log1p(jnp.exp(-jnp.abs(x)))


def _sigmoid(x):
    return 0.5 * (1.0 + jnp.tanh(0.5 * x))


def _silu(x):
    return x * _sigmoid(x)


def _gelu_tanh(x):
    return 0.5 * x * (1.0 + jnp.tanh(math.sqrt(2.0 / math.pi) * (x + 0.044715 * (x * x * x))))


def _split3(x):
    hi = x.astype(BF16)
    r = x - hi.astype(F32)
    mid = r.astype(BF16)
    lo = (r - mid.astype(F32)).astype(BF16)
    return hi, mid, lo


def _cumsum_rows(x, n):
    r = lax.broadcasted_iota(jnp.int32, (n, n), 0)
    c = lax.broadcasted_iota(jnp.int32, (n, n), 1)
    tri = (c <= r).astype(BF16)
    hi, mid, lo = _split3(x)
    return _dot(tri, hi) + _dot(tri, mid) + _dot(tri, lo)


def _cumsum_lanes(x, n):
    r = lax.broadcasted_iota(jnp.int32, (n, n), 0)
    c = lax.broadcasted_iota(jnp.int32, (n, n), 1)
    tri = (r <= c).astype(BF16)
    hi, mid, lo = _split3(x)
    return _dot(hi, tri) + _dot(mid, tri) + _dot(lo, tri)


def _mod_kernel(c_ref, w_ref, b_ref, o_ref):
    c = c_ref[...]
    s = _silu(c).astype(BF16)
    o_ref[0] = _dot(s, w_ref[0].astype(BF16)) + b_ref[0]


def _ada_mod(c_all, w_ada, b_ada):
    nl, _, nmod = w_ada.shape
    rows = c_all.shape[0]
    tn = 1536
    return pl.pallas_call(
        _mod_kernel,
        grid=(nl, nmod // tn),
        in_specs=[
            pl.BlockSpec((rows, D), lambda l, j: (0, 0)),
            pl.BlockSpec((1, D, tn), lambda l, j: (l, 0, j)),
            pl.BlockSpec((1, 1, tn), lambda l, j: (l, 0, j)),
        ],
        out_specs=pl.BlockSpec((1, rows, tn), lambda l, j: (l, 0, j)),
        out_shape=jax.ShapeDtypeStruct((nl, rows, nmod), F32),
        compiler_params=_cparams(2),
    )(c_all, w_ada, b_ada.reshape(nl, 1, nmod))


class _Rows:
    def __init__(self, rows, tm, groups, per_row_mod):
        self.rows, self.tm, self.groups = rows, tm, groups
        self.n_tiles = rows // tm
        self.tpg = self.n_tiles // groups
        self.per_row_mod = per_row_mod

    def mod_spec(self, which):
        tpg, tm = self.tpg, self.tm
        if self.per_row_mod:
            return pl.BlockSpec((None, None, tm, D), lambda i, *_: (which, 0, i, 0))
        return pl.BlockSpec((None, None, 1, D), lambda i, *_: (which, i // tpg, 0, 0))


def _fox_in_sample_kernel(x_ref, g_ref, sc_ref, sh_ref, wq_ref, wk_ref, wv_ref, wf_ref, bf_ref,
                          k_ref, v_ref, lf_ref, q_ref):
    h = _norm_mod(x_ref[...], g_ref[...], sc_ref[...], sh_ref[...]).astype(BF16)
    q_ref[...] = _dot(h, wq_ref[...]) * QK_SCALE
    k_ref[...] = _dot(h, wk_ref[...])
    v_ref[...] = _dot(h, wv_ref[...])
    lf = _log_sigmoid(_dot(h, wf_ref[...]) + bf_ref[...])
    lf_ref[...] = lf[:, :H]


def _fox_in_sample(x, g, mod, wq, wk, wv, wf, bf, rt):
    tm = rt.tm
    full = lambda shape: pl.BlockSpec(shape, lambda i: (0,) * len(shape))
    row = pl.BlockSpec((tm, D), lambda i: (i, 0))
    rows_f32 = jax.ShapeDtypeStruct((rt.rows, D), F32)
    return pl.pallas_call(
        _fox_in_sample_kernel,
        grid=(rt.n_tiles,),
        in_specs=[row, full((1, D)), rt.mod_spec(1), rt.mod_spec(0),
                  full((D, D)), full((D, D)), full((D, D)), full((D, LANES)), full((1, LANES))],
        out_specs=[row, row, pl.BlockSpec((tm, H), lambda i: (i, 0)), row],
        out_shape=[rows_f32, rows_f32, jax.ShapeDtypeStruct((rt.rows, H), F32), rows_f32],
        compiler_params=_cparams(1),
    )(x, g, mod, mod, wq, wk, wv, wf, bf)


def _fox_in_prompt_kernel(x_ref, g_ref, sc_ref, sh_ref, wq_ref, wkt_ref, wvt_ref, wv_ref, wf_ref, bf_ref,
                          wft_ref, bft_ref, *rest, tm, tpg, n_prev):
    kt_ref, vt_ref, lft_ref, qa_ref, kat_ref, vp_ref, carry_r, carry_c = rest[n_prev:]
    i = pl.program_id(0)

    @pl.when(i % tpg == 0)
    def _():
        carry_r[...] = jnp.zeros_like(carry_r)
        carry_c[...] = jnp.zeros_like(carry_c)

    h = _norm_mod(x_ref[...], g_ref[...], sc_ref[...], sh_ref[...]).astype(BF16)
    q = _dot(h, wq_ref[...]) * (QK_SCALE * LOG2E)
    kt = _dot_nt(wkt_ref[...], h)
    kt_ref[0, 0] = kt
    vt_ref[0, 0] = _dot_nt(wvt_ref[...], h)
    vb = _dot(h, wv_ref[...]).astype(BF16)
    for hp in range(NPAIR):
        vp_ref[0, hp] = vb[:, hp * LANES:(hp + 1) * LANES]

    lf = _log_sigmoid(_dot(h, wf_ref[...]) + bf_ref[...])
    lft = _log_sigmoid(_dot_nt(wft_ref[...], h) + bft_ref[...])
    lft_ref[0, 0] = lft
    cum = _cumsum_rows(lf, tm) + carry_r[...]
    carry_r[...] = cum[tm - 1:tm, :]
    cumt = _cumsum_lanes(lft, tm) + carry_c[...]
    carry_c[...] = cumt[:, tm - 1:tm]
    hi, mid, lo = [p.astype(F32) for p in _split3(cum * LOG2E)]
    hit, midt, lot = [p.astype(F32) for p in _split3(cumt * LOG2E)]
    lane = lax.broadcasted_iota(jnp.int32, (tm, HD), 1)
    srow = lax.broadcasted_iota(jnp.int32, (HD, tm), 0)
    for hh in range(H):
        chi, cmid, clo = hi[:, hh:hh + 1], mid[:, hh:hh + 1], lo[:, hh:hh + 1]
        q_tail = jnp.where(lane == 0, chi, jnp.where(lane == 1, cmid, jnp.where(
            lane == 2, clo, jnp.where(lane < 6, 1.0, 0.0))))
        khi, kmid, klo = hit[hh:hh + 1, :], midt[hh:hh + 1, :], lot[hh:hh + 1, :]
        k_tail = jnp.where(srow < 3, 1.0, jnp.where(srow == 3, -khi, jnp.where(
            srow == 4, -kmid, jnp.where(srow == 5, -klo, 0.0))))
        qa_ref[0, hh, :, 0:HD] = q[:, hh * HD:(hh + 1) * HD].astype(BF16)
        qa_ref[0, hh, :, HD:2 * HD] = q_tail.astype(BF16)
        kat_ref[0, hh, 0, 0:HD, :] = kt[hh * HD:(hh + 1) * HD, :].astype(BF16)
        kat_ref[0, hh, 0, HD:2 * HD, :] = k_tail.astype(BF16)


def _fox_in_prompt(x, g, mod, wq, wkt, wvt, wv, wf, bf, wft, bft, rt, seq, layer, n_layers, prev):
    tm, tpg, b = rt.tm, rt.tpg, rt.groups
    full = lambda shape: pl.BlockSpec(shape, lambda i: (0,) * len(shape))
    row = pl.BlockSpec((tm, D), lambda i: (i, 0))
    if prev is None:
        prev = (jnp.zeros((n_layers, b, D, seq), F32), jnp.zeros((n_layers, b, D, seq), F32),
                jnp.zeros((n_layers, b, H, seq), F32))
    prev = tuple(prev)
    n_in = 12
    in_specs = [row, full((1, D)), rt.mod_spec(1), rt.mod_spec(0),
                full((D, D)), full((D, D)), full((D, D)), full((D, D)), full((D, LANES)),
                full((1, LANES)), full((H, D)), full((H, 1))]
    in_specs += [pl.BlockSpec(memory_space=pl.ANY)] * len(prev)
    state_spec = lambda rows: pl.BlockSpec((1, 1, rows, tm), lambda i: (layer, i // tpg, 0, i % tpg))
    out_specs = [state_spec(D), state_spec(D), state_spec(H),
                 pl.BlockSpec((1, H, tm, LANES), lambda i: (i // tpg, 0, i % tpg, 0)),
                 pl.BlockSpec((1, H, 1, LANES, tm), lambda i: (i // tpg, 0, i % tpg, 0, 0)),
                 pl.BlockSpec((1, NPAIR, tm, LANES), lambda i: (i // tpg, 0, i % tpg, 0))]
    out_shape = [jax.ShapeDtypeStruct((n_layers, b, D, seq), F32),
                 jax.ShapeDtypeStruct((n_layers, b, D, seq), F32),
                 jax.ShapeDtypeStruct((n_layers, b, H, seq), F32),
                 jax.ShapeDtypeStruct((b, H, seq, LANES), BF16),
                 jax.ShapeDtypeStruct((b, H, seq // tm, LANES, tm), BF16),
                 jax.ShapeDtypeStruct((b, NPAIR, seq, LANES), BF16)]
    return pl.pallas_call(
        functools.partial(_fox_in_prompt_kernel, tm=tm, tpg=tpg, n_prev=len(prev)),
        grid=(rt.n_tiles,),
        in_specs=in_specs, out_specs=out_specs, out_shape=out_shape,
        scratch_shapes=[pltpu.VMEM((1, LANES), F32), pltpu.VMEM((H, 1), F32)],
        input_output_aliases={n_in + t: t for t in range(len(prev))},
        compiler_params=_cparams(1),
    )(x, g, mod, mod, wq, wkt, wvt, wv, wf, bf, wft, bft, *prev)


def _attn_kernel(qa_ref, kat_ref, vp_ref, o_ref, m0, m1, l0, l1, a0, a1, sa0, sa1, sb0, sb1, *, tq, tk):
    qi = pl.program_id(2)
    m_s, l_s, acc_s, sa_s, sb_s = (m0, m1), (l0, l1), (a0, a1), (sa0, sa1), (sb0, sb1)
    q = [qa_ref[0, 0], qa_ref[0, 1]]
    for j in range(2):
        m_s[j][...] = jnp.full_like(m_s[j], NEG_INF)
        l_s[j][...] = jnp.zeros_like(l_s[j])
        acc_s[j][...] = jnp.zeros_like(acc_s[j])
        sa_s[j][...] = _dot(q[j], kat_ref[0, j, 0])

    def consume(j, kblk, s):
        v = vp_ref[0, 0, pl.ds(pl.multiple_of(kblk * tk, tk), tk), :]
        m_old = m_s[j][...]
        m_new = jnp.maximum(m_old, jnp.max(s, axis=-1, keepdims=True))
        alpha = jnp.exp2(m_old - m_new)
        p = jnp.exp2(s - jnp.concatenate([m_new] * (tk // LANES), axis=1))
        psum = p[:, 0:LANES]
        for t in range(1, tk // LANES):
            psum = psum + p[:, t * LANES:(t + 1) * LANES]
        l_s[j][...] = alpha * l_s[j][...] + psum
        acc_s[j][...] = alpha * acc_s[j][...] + _dot(p.astype(BF16), v)
        m_s[j][...] = m_new

    def scores(j, kblk):
        return _dot(q[j], kat_ref[0, j, kblk])

    def pair(k2, carry):
        kb = 2 * k2
        for j in range(2):
            sb_s[j][...] = scores(j, kb + 1)
            consume(j, kb, sa_s[j][...])
        for j in range(2):
            sa_s[j][...] = scores(j, kb + 2)
            consume(j, kb + 1, sb_s[j][...])
        return carry

    lax.fori_loop(0, qi // 2, pair, 0)

    def diagonal(j, s):
        r = lax.broadcasted_iota(jnp.int32, (tq, tk), 0)
        c = lax.broadcasted_iota(jnp.int32, (tq, tk), 1)
        consume(j, qi, jnp.where(c <= r, s, NEG_INF))

    @pl.when(qi % 2 == 1)
    def _():
        for j in range(2):
            sb_s[j][...] = scores(j, qi)
            consume(j, qi - 1, sa_s[j][...])
        for j in range(2):
            diagonal(j, sb_s[j][...])

    @pl.when(qi % 2 == 0)
    def _():
        for j in range(2):
            diagonal(j, sa_s[j][...])

    outs = [acc_s[j][...] / jnp.sum(l_s[j][...], axis=-1, keepdims=True) for j in range(2)]
    lane = lax.broadcasted_iota(jnp.int32, (tq, LANES), 1)
    o_ref[0, 0] = jnp.where(lane < HD, outs[0], outs[1]).astype(BF16)


def _attn_prompt(qa, kat, vp, tq):
    b, _, seq, _ = qa.shape
    nkb, tk = kat.shape[2], kat.shape[4]
    return pl.pallas_call(
        functools.partial(_attn_kernel, tq=tq, tk=tk),
        grid=(b, NPAIR, seq // tq),
        in_specs=[
            pl.BlockSpec((1, 2, tq, LANES), lambda bb, hp, qi: (bb, hp, qi, 0)),
            pl.BlockSpec((1, 2, nkb, LANES, tk), lambda bb, hp, qi: (bb, hp, 0, 0, 0)),
            pl.BlockSpec((1, 1, seq, LANES), lambda bb, hp, qi: (bb, hp, 0, 0)),
        ],
        out_specs=pl.BlockSpec((1, 1, tq, LANES), lambda bb, hp, qi: (bb, hp, qi, 0)),
        out_shape=jax.ShapeDtypeStruct((b, NPAIR, seq, LANES), BF16),
        scratch_shapes=[pltpu.VMEM((tq, LANES), F32)] * 6 + [pltpu.VMEM((tq, tk), F32)] * 4,
        compiler_params=_cparams(3),
    )(qa, kat, vp)


def _attn_sample_kernel(q_ref, kn_ref, vn_ref, lfn_ref, ck_ref, cv_ref, clf_ref, o_ref,
                        m_s, l_s, acc_s, carry_s, *, sq, tk, nkc):
    kc = pl.program_id(1)
    nrow = H * sq

    @pl.when(kc == 0)
    def _():
        m_s[...] = jnp.full_like(m_s, NEG_INF)
        l_s[...] = jnp.zeros_like(l_s)
        acc_s[...] = jnp.zeros_like(acc_s)
        carry_s[...] = jnp.zeros_like(carry_s)

    rowh = lax.broadcasted_iota(jnp.int32, (nrow, D), 0) // sq
    colh = lax.broadcasted_iota(jnp.int32, (nrow, D), 1) // HD
    q = q_ref[0]
    qbd = jnp.where(rowh == colh, jnp.concatenate([q] * H, axis=0), 0.0).astype(BF16)

    def expand_heads(ct):
        n = ct.shape[1]
        return jnp.concatenate([jnp.broadcast_to(ct[hh:hh + 1, :], (sq, n)) for hh in range(H)], axis=0)

    def update(s, vb, v_transposed):
        m_old = m_s[...]
        m_new = jnp.maximum(m_old, jnp.max(s, axis=-1, keepdims=True))
        alpha = jnp.exp(m_old - m_new)
        p = jnp.exp(s - m_new)
        l_s[...] = alpha * l_s[...] + jnp.sum(p, axis=-1, keepdims=True)
        pb = p.astype(BF16)
        pv = _dot_nt(pb, vb) if v_transposed else _dot(pb, vb)
        acc_s[...] = alpha * acc_s[...] + pv
        m_s[...] = m_new

    sub = 256
    cums = []
    carry = carry_s[...]
    for t in range(tk // sub):
        cpart = _cumsum_lanes(clf_ref[0, :, t * sub:(t + 1) * sub], sub) + carry
        carry = cpart[:, sub - 1:sub]
        cums.append(cpart)
    carry_s[...] = carry
    c_keys = jnp.concatenate(cums, axis=1)
    s = _dot(qbd, ck_ref[0].astype(BF16)) - expand_heads(c_keys)
    update(s, cv_ref[0].astype(BF16), True)

    @pl.when(kc == nkc - 1)
    def _():
        c_new = _cumsum_lanes(lfn_ref[0], LANES) + carry
        pad = jnp.zeros((LANES - sq, D), F32)
        kn = jnp.concatenate([kn_ref[0], pad], axis=0).astype(BF16)
        vn = jnp.concatenate([vn_ref[0], pad], axis=0).astype(BF16)
        c_rows = expand_heads(c_new)
        rq = lax.broadcasted_iota(jnp.int32, (nrow, LANES), 0) % sq
        kidx = lax.broadcasted_iota(jnp.int32, (nrow, LANES), 1)
        cq = jnp.sum(jnp.where(kidx == rq, c_rows, 0.0), axis=-1, keepdims=True)
        s_new = _dot_nt(qbd, kn) + (cq - c_rows)
        s_new = jnp.where(kidx <= rq, s_new, NEG_INF)
        m_s[...] = m_s[...] + cq
        update(s_new, vn, False)
        r = acc_s[...] / l_s[...]
        r = jnp.where(rowh == colh, r, 0.0)
        out = r[0:sq, :]
        for hh in range(1, H):
            out = out + r[hh * sq:(hh + 1) * sq, :]
        o_ref[0] = out


def _attn_sample(q, k_new, v_new, lf_new, cache_k, cache_v, cache_lf, layer):
    nb, sq, _ = q.shape
    past = cache_k.shape[3]
    tk = 1024
    nkc = past // tk
    cur = pl.BlockSpec((1, sq, D), lambda b, kc: (b, 0, 0))
    return pl.pallas_call(
        functools.partial(_attn_sample_kernel, sq=sq, tk=tk, nkc=nkc),
        grid=(nb, nkc),
        in_specs=[cur, cur, cur,
                  pl.BlockSpec((1, H, LANES), lambda b, kc: (b, 0, 0)),
                  pl.BlockSpec((None, 1, D, tk), lambda b, kc: (layer, b, 0, kc)),
                  pl.BlockSpec((None, 1, D, tk), lambda b, kc: (layer, b, 0, kc)),
                  pl.BlockSpec((None, 1, H, tk), lambda b, kc: (layer, b, 0, kc))],
        out_specs=cur,
        out_shape=jax.ShapeDtypeStruct((nb, sq, D), F32),
        scratch_shapes=[pltpu.VMEM((H * sq, 1), F32), pltpu.VMEM((H * sq, 1), F32),
                        pltpu.VMEM((H * sq, D), F32), pltpu.VMEM((H, 1), F32)],
        compiler_params=_cparams(2),
    )(q, k_new, v_new, lf_new, cache_k, cache_v, cache_lf)


def _lru_kernel(x_ref, g_ref, sc_ref, sh_ref, wx_ref, wg_ref, cw_ref, cb_ref, wa_ref, ba_ref,
                wi_ref, bi_ref, lam_ref, h0_ref, c0_ref, y_ref, hl_ref, cn_ref, xp_s, hc_s,
                *, tm, tpg, stride, hdr):
    i = pl.program_id(0)
    nst = (CONV_W - 1) * stride

    @pl.when(i % tpg == 0)
    def _():
        xp_s[hdr - nst:hdr, :] = c0_ref[0]
        hc_s[...] = h0_ref[0]

    h = _norm_mod(x_ref[...], g_ref[...], sc_ref[...], sh_ref[...]).astype(BF16)
    xb = _dot(h, wx_ref[...])
    gb = _dot(h, wg_ref[...])
    xp_s[hdr:hdr + tm, :] = xb
    xc = cb_ref[...] + xb * cw_ref[CONV_W - 1:CONV_W, :]
    for j in range(CONV_W - 1):
        d = (CONV_W - 1 - j) * stride
        shifted = jnp.concatenate([xp_s[hdr - d:2 * hdr - d, :], pltpu.roll(xb, d, 0)[hdr:, :]], axis=0)
        xc = xc + shifted * cw_ref[j:j + 1, :]
    tail = xp_s[hdr + tm - nst:hdr + tm, :]
    cn_ref[0] = tail
    xp_s[hdr - nst:hdr, :] = tail

    xcb = xc.astype(BF16)
    r = _sigmoid(_dot(xcb, wa_ref[...]) + ba_ref[...])
    gi = _sigmoid(_dot(xcb, wi_ref[...]) + bi_ref[...])
    log_a = LRU_C * r * _log_sigmoid(lam_ref[...])
    a = jnp.exp(log_a)
    u = jnp.sqrt(-jnp.tanh(log_a) * (1.0 + a * a)) * gi * xc

    cr = min(SCAN_ROWS, tm) if stride == 1 else tm
    row = lax.broadcasted_iota(jnp.int32, (tm, 1), 0) % cr
    sft = stride
    while sft < cr:
        keep = row >= sft
        a_sh = pltpu.roll(a, sft, 0)
        u_sh = pltpu.roll(u, sft, 0)
        u = jnp.where(keep, a * u_sh + u, u)
        a = jnp.where(keep, a * a_sh, a)
        sft *= 2
    hc = hc_s[...]
    if stride == 1:
        parts = []
        for c in range(tm // cr):
            part = a[c * cr:(c + 1) * cr, :] * hc + u[c * cr:(c + 1) * cr, :]
            hc = part[cr - 1:cr, :]
            parts.append(part)
        hs = jnp.concatenate(parts, axis=0)
    else:
        hs = a * jnp.concatenate([hc] * (tm // stride), axis=0) + u
    last = hs[tm - stride:tm, :]
    hc_s[...] = last
    hl_ref[0] = last
    y_ref[...] = (hs * _gelu_tanh(gb)).astype(BF16)


def _lru_mix(x, g, mod, wx, wg, cw, cb, wa, ba, wi, bi, lam, h0, c0, rt, stride):
    tm, tpg = rt.tm, rt.tpg
    hdr = -(-(CONV_W - 1) * stride // 8) * 8
    nst = (CONV_W - 1) * stride
    full = lambda shape: pl.BlockSpec(shape, lambda i: (0,) * len(shape))
    row = pl.BlockSpec((tm, D), lambda i: (i, 0))
    grp = rt.groups
    return pl.pallas_call(
        functools.partial(_lru_kernel, tm=tm, tpg=tpg, stride=stride, hdr=hdr),
        grid=(rt.n_tiles,),
        in_specs=[row, full((1, D)), rt.mod_spec(1), rt.mod_spec(0),
                  full((D, D)), full((D, D)), full((CONV_W, D)), full((1, D)),
                  full((D, D)), full((1, D)), full((D, D)), full((1, D)), full((1, D)),
                  pl.BlockSpec((1, stride, D), lambda i: (i // tpg, 0, 0)),
                  pl.BlockSpec((1, nst, D), lambda i: (i // tpg, 0, 0))],
        out_specs=[row,
                   pl.BlockSpec((1, stride, D), lambda i: (i // tpg, 0, 0)),
                   pl.BlockSpec((1, nst, D), lambda i: (i // tpg, 0, 0))],
        out_shape=[jax.ShapeDtypeStruct((rt.rows, D), BF16),
                   jax.ShapeDtypeStruct((grp, stride, D), F32),
                   jax.ShapeDtypeStruct((grp, nst, D), F32)],
        scratch_shapes=[pltpu.VMEM((hdr + tm, D), F32), pltpu.VMEM((stride, D), F32)],
        compiler_params=_cparams(1),
    )(x, g, mod, mod, wx, wg, cw, cb, wa, ba, wi, bi, lam, h0, c0)


def _mix_in(a_ref, pair_major):
    if pair_major:
        return jnp.concatenate([a_ref[0, hp] for hp in range(NPAIR)], axis=1)
    return a_ref[...]


def _ffn_kernel(x_ref, a_ref, ga_ref, wo_ref, g2_ref, sc_ref, sh_ref, gm_ref, wg_ref, wu_ref, wd_ref,
                o_ref, x1_s, h_s, acc_s, *, nf, pair_major):
    f = pl.program_id(1)

    @pl.when(f == 0)
    def _():
        a = _mix_in(a_ref, pair_major).astype(BF16)
        x1 = x_ref[...] + ga_ref[...] * _dot(a, wo_ref[...])
        x1_s[...] = x1
        h_s[...] = _norm_mod(x1, g2_ref[...], sc_ref[...], sh_ref[...]).astype(BF16)
        acc_s[...] = jnp.zeros_like(acc_s)

    h = h_s[...]
    act = (_silu(_dot(h, wg_ref[...])) * _dot(h, wu_ref[...])).astype(BF16)
    acc_s[...] += _dot(act, wd_ref[...])

    @pl.when(f == nf - 1)
    def _():
        o_ref[...] = x1_s[...] + gm_ref[...] * acc_s[...]


def _route_kernel(x_ref, a_ref, ga_ref, wo_ref, g2_ref, sc_ref, sh_ref, wr_ref,
                  x1_ref, h_ref, route_ref, gate_ref, cnt_ref, carry_s, *, tm):
    i = pl.program_id(0)

    @pl.when(i == 0)
    def _():
        carry_s[...] = jnp.zeros_like(carry_s)

    lane = lax.broadcasted_iota(jnp.int32, (tm, LANES), 1)
    x1 = x_ref[...] + ga_ref[...] * _dot(a_ref[...].astype(BF16), wo_ref[...])
    x1_ref[...] = x1
    hf = _norm_mod(x1, g2_ref[...], sc_ref[...], sh_ref[...])
    _to_slabs(h_ref, hf, tm)
    h_hi, h_lo, _ = _split3(hf)
    w_hi, w_lo, _ = _split3(wr_ref[...])
    logits = _dot(h_hi, w_hi) + (_dot(h_hi, w_lo) + _dot(h_lo, w_hi))
    logits = jnp.where(lane < NEXP, logits, NEG_INF)
    m1 = jnp.max(logits, axis=-1, keepdims=True)
    i1 = jnp.min(jnp.where(logits == m1, lane, LANES), axis=-1, keepdims=True)
    rest = jnp.where(lane == i1, NEG_INF, logits)
    m2 = jnp.max(rest, axis=-1, keepdims=True)
    i2 = jnp.min(jnp.where(rest == m2, lane, LANES), axis=-1, keepdims=True)
    t = jnp.exp(m2 - m1)
    gate_ref[...] = jnp.where(lane == 0, 1.0 / (1.0 + t), jnp.where(lane == 1, t / (1.0 + t), 0.0))
    sel = (lane == i1) | (lane == i2)
    r = lax.broadcasted_iota(jnp.int32, (tm, tm), 0)
    c = lax.broadcasted_iota(jnp.int32, (tm, tm), 1)
    before = _dot((c < r).astype(BF16), sel.astype(BF16)) + carry_s[...]
    rank1 = jnp.sum(jnp.where(lane == i1, before, 0.0), axis=-1, keepdims=True).astype(jnp.int32)
    rank2 = jnp.sum(jnp.where(lane == i2, before, 0.0), axis=-1, keepdims=True).astype(jnp.int32)
    route_ref[...] = jnp.where(lane == 0, i1, jnp.where(lane == 1, i2, jnp.where(
        lane == 2, rank1, jnp.where(lane == 3, rank2, 0))))
    total = carry_s[...] + jnp.sum(sel.astype(F32), axis=0, keepdims=True)
    carry_s[...] = total
    cnt_ref[...] = total.astype(jnp.int32)


def _to_slabs(ref, x, n):
    for c in range(SLAB):
        ref[pl.ds(c, n, stride=SLAB), :] = x[:, c * LANES:(c + 1) * LANES]


def _from_slabs(ref, n):
    return jnp.concatenate([ref[pl.ds(c, n, stride=SLAB), :] for c in range(SLAB)], axis=1)


def _slab(ref, t):
    return ref.at[pl.ds(pl.multiple_of(SLAB * t, SLAB), SLAB), :]


def _dispatch_kernel(dest_ref, h_ref, zero_ref, o_ref, sem, *, tm):
    del zero_ref
    base = pl.program_id(0) * tm

    def issue(r, carry):
        for s in range(2):
            d = dest_ref[2 * (base + r) + s]
            pltpu.make_async_copy(_slab(h_ref, r), _slab(o_ref, d), sem.at[s]).start()
        return carry

    lax.fori_loop(0, tm, issue, 0, unroll=4)
    for s in range(2):
        pltpu.make_async_copy(h_ref, o_ref.at[pl.ds(0, tm * SLAB), :], sem.at[s]).wait()


def _experts_kernel(te_ref, nu_ref, hs_ref, wg_ref, wu_ref, wd_ref, o_ref, acc_s, *, nf, tg):
    del te_ref
    i = pl.program_id(0)
    f = pl.program_id(1)

    def partial_out():
        h = _from_slabs(hs_ref, tg).astype(BF16)
        act = (_silu(_dot(h, wg_ref[0, 0])) * _dot(h, wu_ref[0, 0])).astype(BF16)
        return _dot(act, wd_ref[0, 0])

    used = i < nu_ref[0]

    @pl.when(used & (f == 0))
    def _():
        acc_s[...] = partial_out()

    @pl.when(used & (f > 0) & (f < nf - 1))
    def _():
        acc_s[...] += partial_out()

    @pl.when(used & (f == nf - 1))
    def _():
        _to_slabs(o_ref, acc_s[...] + partial_out(), tg)

    @pl.when((i >= nu_ref[0]) & (f == 0))
    def _():
        o_ref[...] = jnp.zeros_like(o_ref)


def _combine_kernel(dest_ref, x1_ref, gm_ref, gate_ref, gf_ref, y_ref, o_ref, b0_s, b1_s, sem,
                    *, tm, final_norm):
    base = pl.program_id(0) * tm
    bufs = (b0_s, b1_s)

    def issue(r, carry):
        for s in range(2):
            d = dest_ref[2 * (base + r) + s]
            pltpu.make_async_copy(_slab(y_ref, d), _slab(bufs[s], r), sem.at[s]).start()
        return carry

    lax.fori_loop(0, tm, issue, 0, unroll=4)
    for s in range(2):
        pltpu.make_async_copy(y_ref.at[pl.ds(0, tm * SLAB), :], bufs[s], sem.at[s]).wait()
    gate = gate_ref[...]
    y = gate[:, 0:1] * _from_slabs(b0_s, tm) + gate[:, 1:2] * _from_slabs(b1_s, tm)
    x2 = x1_ref[...] + gm_ref[...] * y
    if final_norm:
        ms = jnp.mean(x2 * x2, axis=-1, keepdims=True)
        x2 = x2 * lax.rsqrt(ms + EPS) * gf_ref[...]
    o_ref[...] = x2


def _post_ffn(x, a, mod, wo, g2, w_gu, w_down, layer, rt, pair_major):
    tm, tpg = rt.tm, rt.tpg
    nf = DFF // TF
    row = pl.BlockSpec((tm, D), lambda i, f: (i, 0))
    if pair_major:
        a_spec = pl.BlockSpec((1, NPAIR, tm, LANES), lambda i, f: (i // tpg, 0, i % tpg, 0))
    else:
        a_spec = row
    c2 = lambda shape: pl.BlockSpec(shape, lambda i, f: (0,) * len(shape))
    return pl.pallas_call(
        functools.partial(_ffn_kernel, nf=nf, pair_major=pair_major),
        grid=(rt.n_tiles, nf),
        in_specs=[row, a_spec, rt.mod_spec(2), c2((D, D)), c2((1, D)),
                  rt.mod_spec(4), rt.mod_spec(3), rt.mod_spec(5),
                  pl.BlockSpec((None, D, TF), lambda i, f: (layer, 0, f)),
                  pl.BlockSpec((None, D, TF), lambda i, f: (layer, 0, nf + f)),
                  pl.BlockSpec((None, TF, D), lambda i, f: (layer, f, 0))],
        out_specs=row,
        out_shape=jax.ShapeDtypeStruct((rt.rows, D), F32),
        scratch_shapes=[pltpu.VMEM((tm, D), F32), pltpu.VMEM((tm, D), BF16), pltpu.VMEM((tm, D), F32)],
        compiler_params=_cparams(2),
    )(x, a, mod, wo, g2, mod, mod, mod, w_gu, w_gu, w_down)


def _post_moe(x, a, mod, wo, g2, w_router, w_gu, w_down, layer, rt, final_g):
    tm, rows, n_tiles = rt.tm, rt.rows, rt.n_tiles
    nf = DFF // TF_MOE
    assert nf >= 2
    row = pl.BlockSpec((tm, D), lambda i, *_: (i, 0))
    lanes = pl.BlockSpec((tm, LANES), lambda i, *_: (i, 0))
    full = lambda shape: pl.BlockSpec(shape, lambda i, *_: (0,) * len(shape))
    rows_f32 = jax.ShapeDtypeStruct((rows, D), F32)
    slabs = lambda n: pl.BlockSpec((n * SLAB, LANES), lambda i, *_: (i, 0))
    tg = min(TG, rows // 2)
    x1, hf, route, gate, cnt = pl.pallas_call(
        functools.partial(_route_kernel, tm=tm),
        grid=(n_tiles,),
        in_specs=[row, row, rt.mod_spec(2), full((D, D)), full((1, D)), rt.mod_spec(4), rt.mod_spec(3),
                  full((D, LANES))],
        out_specs=[row, slabs(tm), lanes, lanes, full((1, LANES))],
        out_shape=[rows_f32, jax.ShapeDtypeStruct((rows * SLAB, LANES), F32),
                   jax.ShapeDtypeStruct((rows, LANES), jnp.int32),
                   jax.ShapeDtypeStruct((rows, LANES), F32), jax.ShapeDtypeStruct((1, LANES), jnp.int32)],
        scratch_shapes=[pltpu.VMEM((1, LANES), F32)],
        compiler_params=_cparams(1),
    )(x, a, mod, wo, g2, mod, mod, w_router)

    n_tiles_g = -(-(2 * rows + NEXP * (tg - 1)) // tg)
    tiles_e = (cnt[0, :NEXP] + (tg - 1)) // tg
    tile_end = jnp.cumsum(tiles_e)
    start = (tile_end - tiles_e) * tg
    is_e = route[:, 0:2, None] == jnp.arange(NEXP, dtype=jnp.int32)
    dest = (route[:, 2:4] + jnp.sum(jnp.where(is_e, start, 0), axis=-1)).reshape(-1).astype(jnp.int32)
    tile_ids = jnp.arange(n_tiles_g, dtype=jnp.int32)
    tile_expert = jnp.minimum(jnp.sum(tile_ids[:, None] >= tile_end[None, :], axis=1), NEXP - 1)
    tile_expert = tile_expert.astype(jnp.int32)
    n_used = tile_end[NEXP - 1:].astype(jnp.int32)

    sorted_h = pl.pallas_call(
        functools.partial(_dispatch_kernel, tm=tm),
        grid_spec=pltpu.PrefetchScalarGridSpec(
            num_scalar_prefetch=1, grid=(n_tiles,),
            in_specs=[slabs(tm), pl.BlockSpec(memory_space=pl.ANY)],
            out_specs=pl.BlockSpec(memory_space=pl.ANY),
            scratch_shapes=[pltpu.SemaphoreType.DMA((2,))]),
        out_shape=jax.ShapeDtypeStruct((n_tiles_g * tg * SLAB, LANES), F32),
        input_output_aliases={2: 0},
        compiler_params=_cparams(1),
    )(dest, hf, jnp.zeros((n_tiles_g * tg * SLAB, LANES), F32))

    def tile_of(i, nu):
        return jnp.maximum(jnp.minimum(i, nu[0] - 1), 0)

    def chunk_of(i, f, nu):
        return jnp.where(i < nu[0], f, nf - 1)

    y_sorted = pl.pallas_call(
        functools.partial(_experts_kernel, nf=nf, tg=tg),
        grid_spec=pltpu.PrefetchScalarGridSpec(
            num_scalar_prefetch=2, grid=(n_tiles_g, nf),
            in_specs=[
                pl.BlockSpec((tg * SLAB, LANES), lambda i, f, te, nu: (tile_of(i, nu), 0)),
                pl.BlockSpec((1, 1, D, TF_MOE),
                             lambda i, f, te, nu: (layer, te[tile_of(i, nu)], 0, chunk_of(i, f, nu))),
                pl.BlockSpec((1, 1, D, TF_MOE),
                             lambda i, f, te, nu: (layer, te[tile_of(i, nu)], 0, nf + chunk_of(i, f, nu))),
                pl.BlockSpec((1, 1, TF_MOE, D),
                             lambda i, f, te, nu: (layer, te[tile_of(i, nu)], chunk_of(i, f, nu), 0))],
            out_specs=pl.BlockSpec((tg * SLAB, LANES), lambda i, f, te, nu: (i, 0)),
            scratch_shapes=[pltpu.VMEM((tg, D), F32)]),
        out_shape=jax.ShapeDtypeStruct((n_tiles_g * tg * SLAB, LANES), F32),
        compiler_params=_cparams(2),
    )(tile_expert, n_used, sorted_h, w_gu, w_gu, w_down)

    gf = jnp.ones((1, D), F32) if final_g is None else final_g
    return pl.pallas_call(
        functools.partial(_combine_kernel, tm=tm, final_norm=final_g is not None),
        grid_spec=pltpu.PrefetchScalarGridSpec(
            num_scalar_prefetch=1, grid=(n_tiles,),
            in_specs=[row, rt.mod_spec(5), lanes, full((1, D)), pl.BlockSpec(memory_space=pl.ANY)],
            out_specs=row,
            scratch_shapes=[pltpu.VMEM((tm * SLAB, LANES), F32), pltpu.VMEM((tm * SLAB, LANES), F32),
                            pltpu.SemaphoreType.DMA((2,))]),
        out_shape=rows_f32,
        compiler_params=_cparams(1),
    )(dest, x1, mod, gate, gf, y_sorted)


def _final_norm_kernel(x_ref, g_ref, o_ref):
    x = x_ref[...]
    ms = jnp.mean(x * x, axis=-1, keepdims=True)
    o_ref[...] = x * lax.rsqrt(ms + EPS) * g_ref[...]


def _final_norm(x, g, tm):
    rows = x.shape[0]
    return pl.pallas_call(
        _final_norm_kernel,
        grid=(rows // tm,),
        in_specs=[pl.BlockSpec((tm, D), lambda i: (i, 0)), pl.BlockSpec((1, D), lambda i: (0, 0))],
        out_specs=pl.BlockSpec((tm, D), lambda i: (i, 0)),
        out_shape=jax.ShapeDtypeStruct((rows, D), F32),
        compiler_params=_cparams(1),
    )(x, g)


def _prep_weights(p):
    w = {}
    fw = p['fox_w_in']
    w['wq'] = fw[:, :, 0 * D:1 * D].astype(BF16)
    w['wk'] = fw[:, :, 1 * D:2 * D].astype(BF16)
    w['wv'] = fw[:, :, 2 * D:3 * D].astype(BF16)
    w['wf'] = jnp.pad(fw[:, :, 3 * D:], ((0, 0), (0, 0), (0, LANES - H))).astype(BF16)
    w['bf'] = jnp.pad(p['fox_b_f'], ((0, 0), (0, LANES - H)))[:, None, :]
    w['wkt'] = w['wk'].transpose(0, 2, 1)
    w['wvt'] = w['wv'].transpose(0, 2, 1)
    w['wft'] = fw[:, :, 3 * D:].transpose(0, 2, 1).astype(BF16)
    w['bft'] = p['fox_b_f'][:, :, None]
    w['fox_wo'] = p['fox_w_out'].astype(BF16)
    w['lru_wx'] = p['lru_w_in'][:, :, :D].astype(BF16)
    w['lru_wg'] = p['lru_w_in'][:, :, D:].astype(BF16)
    nblk = p['lru_w_a'].shape[1]
    eye = jnp.eye(nblk, dtype=F32)
    bd = lambda m: jnp.einsum('lnkj,nm->lnkmj', m, eye).reshape(m.shape[0], D, D).astype(BF16)
    w['lru_wa'] = bd(p['lru_w_a'])
    w['lru_wi'] = bd(p['lru_w_i'])
    w['lru_wo'] = p['lru_w_out'].astype(BF16)
    w['ffn_gu'] = p['ffn_w_gu'].astype(BF16)
    w['ffn_down'] = p['ffn_w_down'].astype(BF16)
    w['router'] = jnp.pad(p['moe_router'], ((0, 0), (0, 0), (0, LANES - NEXP)))
    w['moe_gu'] = p['moe_w_gu'].astype(BF16)
    w['moe_down'] = p['moe_w_down'].astype(BF16)
    return w


def _trunk(x, mods, p, w, rts, prompt, seq, fox_past, lru_past):
    rt, rt_ffn, rt_moe = rts
    depth = p['norm_mix_g'].shape[0]
    n_fox = w['wq'].shape[0]
    final_g = p['final_g'][None, :]
    fox_new, lru_new = [], []
    for l in range(depth):
        j = l // 2
        last = l == depth - 1
        mod = mods[l]
        g1 = p['norm_mix_g'][l][None, :]
        g2 = p['norm_ffn_g'][l][None, :]
        if l % 2 == 0:
            if prompt:
                kt, vt, lft, qa, kat, vp = _fox_in_prompt(
                    x, g1, mod, w['wq'][j], w['wkt'][j], w['wvt'][j], w['wv'][j], w['wf'][j], w['bf'][j],
                    w['wft'][j], w['bft'][j], rt, seq, j, n_fox, fox_new[0] if fox_new else None)
                a = _attn_prompt(qa, kat, vp, tq=kat.shape[4])
                fox_new = [(kt, vt, lft)]
            else:
                k, v, lf, q = _fox_in_sample(x, g1, mod, w['wq'][j], w['wk'][j], w['wv'][j], w['wf'][j],
                                             w['bf'][j], rt)
                nb = rt.rows // seq
                to_b = lambda t: t.reshape(seq, nb, -1).transpose(1, 0, 2)
                kb, vb, lfb = to_b(k), to_b(v), to_b(lf)
                ck, cv, clf = fox_past
                lfn_t = jnp.pad(lfb.transpose(0, 2, 1), ((0, 0), (0, 0), (0, LANES - seq)))
                ab = _attn_sample(to_b(q), kb, vb, lfn_t, ck, cv, clf, j)
                a = ab.transpose(1, 0, 2).reshape(rt.rows, D)
                fox_new.append((kb, vb, lfb))
            x = _post_ffn(x, a, mod, w['fox_wo'][j], g2, w['ffn_gu'], w['ffn_down'], j, rt_ffn,
                          pair_major=prompt)
            if last:
                x = _final_norm(x, final_g, rt.tm)
        else:
            h0, c0, stride = lru_past[j]
            y, hl, cn = _lru_mix(x, g1, mod, w['lru_wx'][j], w['lru_wg'][j], p['lru_conv_w'][j],
                                 p['lru_conv_b'][j][None, :], w['lru_wa'][j], p['lru_b_a'][j][None, :],
                                 w['lru_wi'][j], p['lru_b_i'][j][None, :], p['lru_lam'][j][None, :],
                                 h0, c0, rt, stride)
            lru_new.append((hl, cn))
            x = _post_moe(x, y, mod, w['lru_wo'][j], g2, w['router'][j], w['moe_gu'], w['moe_down'], j,
                          rt_moe, final_g if last else None)
    return x, fox_new, lru_new


def kernel(x_prompt, x_sample, c_prompt, c_sample, cache_k, cache_v, cache_logf, state_h, state_conv,
           norm_mix_g, norm_ffn_g, final_g, w_ada, b_ada, fox_w_in, fox_b_f, fox_w_out,
           lru_w_in, lru_conv_w, lru_conv_b, lru_w_a, lru_b_a, lru_w_i, lru_b_i, lru_lam, lru_w_out,
           ffn_w_gu, ffn_w_down, moe_router, moe_w_gu, moe_w_down):
    p = {
        'norm_mix_g': norm_mix_g, 'norm_ffn_g': norm_ffn_g, 'final_g': final_g,
        'fox_w_in': fox_w_in, 'fox_b_f': fox_b_f, 'fox_w_out': fox_w_out,
        'lru_w_in': lru_w_in, 'lru_conv_w': lru_conv_w, 'lru_conv_b': lru_conv_b,
        'lru_w_a': lru_w_a, 'lru_b_a': lru_b_a, 'lru_w_i': lru_w_i, 'lru_b_i': lru_b_i,
        'lru_lam': lru_lam, 'lru_w_out': lru_w_out,
        'ffn_w_gu': ffn_w_gu, 'ffn_w_down': ffn_w_down,
        'moe_router': moe_router, 'moe_w_gu': moe_w_gu, 'moe_w_down': moe_w_down,
    }
    bp, sp, _ = x_prompt.shape
    bs, ss, _ = x_sample.shape
    depth = w_ada.shape[0]
    n_fox, n_lru = cache_k.shape[0], state_h.shape[0]
    past = cache_k.shape[2]
    w = _prep_weights(p)

    nc = bp + bs
    ncp = -(-nc // 8) * 8
    c_all = jnp.pad(jnp.concatenate([c_prompt, c_sample], axis=0), ((0, ncp - nc), (0, 0)))
    mod_all = _ada_mod(c_all, w_ada, b_ada)
    mods_p = [mod_all[l, :bp].reshape(bp, 6, 1, D).transpose(1, 0, 2, 3) for l in range(depth)]
    mods_s = [jnp.tile(mod_all[l, bp:nc].reshape(bs, 6, D).transpose(1, 0, 2), (1, ss, 1))[:, None]
              for l in range(depth)]

    rts_p = tuple(_Rows(bp * sp, tm, bp, per_row_mod=False) for tm in (TM_MIX, TM_FFN, TM_MOE))
    zeros_h = jnp.zeros((bp, 1, D), F32)
    zeros_c = jnp.zeros((bp, CONV_W - 1, D), F32)
    y_p, fox_p, lru_p = _trunk(x_prompt.reshape(bp * sp, D), mods_p, p, w, rts_p, True, sp,
                               None, [(zeros_h, zeros_c, 1)] * n_lru)

    rt_s = _Rows(bs * ss, bs * ss, 1, per_row_mod=True)
    xs = x_sample.transpose(1, 0, 2).reshape(ss * bs, D)
    ckt = cache_k.transpose(0, 1, 3, 4, 2).reshape(n_fox, bs, D, past)
    cvt = cache_v.transpose(0, 1, 3, 4, 2).reshape(n_fox, bs, D, past)
    clft = cache_logf.transpose(0, 1, 3, 2)
    lru_past = [(state_h[j][None], state_conv[j].transpose(1, 0, 2).reshape(1, (CONV_W - 1) * bs, D), bs)
                for j in range(n_lru)]
    y_s, fox_s, lru_s = _trunk(xs, mods_s, p, w, (rt_s, rt_s, rt_s), False, ss, (ckt, cvt, clft), lru_past)

    y_prompt = y_p.reshape(bp, sp, D)
    y_sample = y_s.reshape(ss, bs, D).transpose(1, 0, 2)
    kt_all, vt_all, lft_all = fox_p[0]
    k_prompt = kt_all.reshape(n_fox, bp, H, HD, sp).transpose(0, 1, 4, 2, 3)
    v_prompt = vt_all.reshape(n_fox, bp, H, HD, sp).transpose(0, 1, 4, 2, 3)
    logf_prompt = lft_all.transpose(0, 1, 3, 2)
    h_prompt = jnp.stack([s[0][:, 0, :] for s in lru_p])
    conv_prompt = jnp.stack([s[1] for s in lru_p])
    k_sample = jnp.stack([f[0] for f in fox_s]).reshape(n_fox, bs, ss, H, HD)
    v_sample = jnp.stack([f[1] for f in fox_s]).reshape(n_fox, bs, ss, H, HD)
    logf_sample = jnp.stack([f[2] for f in fox_s])
    h_sample = jnp.stack([s[0][0] for s in lru_s])
    conv_sample = jnp.stack([s[1][0].reshape(CONV_W - 1, bs, D).transpose(1, 0, 2) for s in lru_s])
    return (y_prompt, y_sample, k_prompt, v_prompt, logf_prompt, h_prompt, conv_prompt,
            k_sample, v_sample, logf_sample, h_sample, conv_sample)
```

```python
import functools
import math

import jax
import jax.numpy as jnp
from jax import lax
from jax.experimental import pallas as pl
from jax.experimental.pallas import tpu as pltpu

F32 = jnp.float32
BF16 = jnp.bfloat16

D = 1024
H = 16
HD = 64
NPAIR = H // 2
LANES = 128
DFF = 3584
NEXP = 8
CONV_W = 4
LRU_C = 8.0
EPS = 1e-6
QK_SCALE = HD ** -0.5
LOG2E = math.log2(math.e)
NEG_INF = float("-inf")
VMEM_LIMIT = 56 * 1024 * 1024
TM_MIX = 512
TM_FFN = 512
TM_MOE = 512
TF = 1792
TF_MOE = 1792
TG = 512
SLAB = D // LANES
SCAN_ROWS = 8


def _cparams(n_axes):
    return pltpu.CompilerParams(dimension_semantics=("arbitrary",) * n_axes,
                                vmem_limit_bytes=VMEM_LIMIT)


def _dot(a, b):
    return jnp.dot(a, b, preferred_element_type=F32)


def _dot_nt(a, b):
    return lax.dot_general(a, b, (((1,), (1,)), ((), ())), preferred_element_type=F32)


def _norm_mod(x, g, sc, sh):
    ms = jnp.mean(x * x, axis=-1, keepdims=True)
    return (x * lax.rsqrt(ms + EPS) * g) * (1.0 + sc) + sh


def _log_sigmoid(x):
    return jnp.minimum(x, 0.0) - jnp.log1p(jnp.exp(-jnp.abs(x)))


def _sigmoid(x):
    return 0.5 * (1.0 + jnp.tanh(0.5 * x))


def _silu(x):
    return x * _sigmoid(x)


def _gelu_tanh(x):
    return 0.5 * x * (1.0 + jnp.tanh(math.sqrt(2.0 / math.pi) * (x + 0.044715 * (x * x * x))))


def _split3(x):
    hi = x.astype(BF16)
    r = x - hi.astype(F32)
    mid = r.astype(BF16)
    lo = (r - mid.astype(F32)).astype(BF16)
    return hi, mid, lo


def _cumsum_rows(x, n):
    r = lax.broadcasted_iota(jnp.int32, (n, n), 0)
    c = lax.broadcasted_iota(jnp.int32, (n, n), 1)
    tri = (c <= r).astype(BF16)
    hi, mid, lo = _split3(x)
    return _dot(tri, hi) + _dot(tri, mid) + _dot(tri, lo)


def _cumsum_lanes(x, n):
    r = lax.broadcasted_iota(jnp.int32, (n, n), 0)
    c = lax.broadcasted_iota(jnp.int32, (n, n), 1)
    tri = (r <= c).astype(BF16)
    hi, mid, lo = _split3(x)
    return _dot(hi, tri) + _dot(mid, tri) + _dot(lo, tri)


def _mod_kernel(c_ref, w_ref, b_ref, o_ref):
    c = c_ref[...]
    s = _silu(c).astype(BF16)
    o_ref[0] = _dot(s, w_ref[0].astype(BF16)) + b_ref[0]


def _ada_mod(c_all, w_ada, b_ada):
    nl, _, nmod = w_ada.shape
    rows = c_all.shape[0]
    tn = 1536
    return pl.pallas_call(
        _mod_kernel,
        grid=(nl, nmod // tn),
        in_specs=[
            pl.BlockSpec((rows, D), lambda l, j: (0, 0)),
            pl.BlockSpec((1, D, tn), lambda l, j: (l, 0, j)),
            pl.BlockSpec((1, 1, tn), lambda l, j: (l, 0, j)),
        ],
        out_specs=pl.BlockSpec((1, rows, tn), lambda l, j: (l, 0, j)),
        out_shape=jax.ShapeDtypeStruct((nl, rows, nmod), F32),
        compiler_params=_cparams(2),
    )(c_all, w_ada, b_ada.reshape(nl, 1, nmod))


class _Rows:
    def __init__(self, rows, tm, groups, per_row_mod):
        self.rows, self.tm, self.groups = rows, tm, groups
        self.n_tiles = rows // tm
        self.tpg = self.n_tiles // groups
        self.per_row_mod = per_row_mod

    def mod_spec(self, which):
        tpg, tm = self.tpg, self.tm
        if self.per_row_mod:
            return pl.BlockSpec((None, None, tm, D), lambda i, *_: (which, 0, i, 0))
        return pl.BlockSpec((None, None, 1, D), lambda i, *_: (which, i // tpg, 0, 0))


def _fox_in_sample_kernel(x_ref, g_ref, sc_ref, sh_ref, wq_ref, wk_ref, wv_ref, wf_ref, bf_ref,
                          k_ref, v_ref, lf_ref, q_ref):
    h = _norm_mod(x_ref[...], g_ref[...], sc_ref[...], sh_ref[...]).astype(BF16)
    q_ref[...] = _dot(h, wq_ref[...]) * QK_SCALE
    k_ref[...] = _dot(h, wk_ref[...])
    v_ref[...] = _dot(h, wv_ref[...])
    lf = _log_sigmoid(_dot(h, wf_ref[...]) + bf_ref[...])
    lf_ref[...] = lf[:, :H]


def _fox_in_sample(x, g, mod, wq, wk, wv, wf, bf, rt):
    tm = rt.tm
    full = lambda shape: pl.BlockSpec(shape, lambda i: (0,) * len(shape))
    row = pl.BlockSpec((tm, D), lambda i: (i, 0))
    rows_f32 = jax.ShapeDtypeStruct((rt.rows, D), F32)
    return pl.pallas_call(
        _fox_in_sample_kernel,
        grid=(rt.n_tiles,),
        in_specs=[row, full((1, D)), rt.mod_spec(1), rt.mod_spec(0),
                  full((D, D)), full((D, D)), full((D, D)), full((D, LANES)), full((1, LANES))],
        out_specs=[row, row, pl.BlockSpec((tm, H), lambda i: (i, 0)), row],
        out_shape=[rows_f32, rows_f32, jax.ShapeDtypeStruct((rt.rows, H), F32), rows_f32],
        compiler_params=_cparams(1),
    )(x, g, mod, mod, wq, wk, wv, wf, bf)


def _fox_in_prompt_kernel(x_ref, g_ref, sc_ref, sh_ref, wq_ref, wkt_ref, wvt_ref, wv_ref, wf_ref, bf_ref,
                          wft_ref, bft_ref, *rest, tm, tpg, n_prev):
    kt_ref, vt_ref, lft_ref, qa_ref, kat_ref, vp_ref, carry_r, carry_c = rest[n_prev:]
    i = pl.program_id(0)

    @pl.when(i % tpg == 0)
    def _():
        carry_r[...] = jnp.zeros_like(carry_r)
        carry_c[...] = jnp.zeros_like(carry_c)

    h = _norm_mod(x_ref[...], g_ref[...], sc_ref[...], sh_ref[...]).astype(BF16)
    q = _dot(h, wq_ref[...]) * (QK_SCALE * LOG2E)
    kt = _dot_nt(wkt_ref[...], h)
    kt_ref[0, 0] = kt
    vt_ref[0, 0] = _dot_nt(wvt_ref[...], h)
    vb = _dot(h, wv_ref[...]).astype(BF16)
    for hp in range(NPAIR):
        vp_ref[0, hp] = vb[:, hp * LANES:(hp + 1) * LANES]

    lf = _log_sigmoid(_dot(h, wf_ref[...]) + bf_ref[...])
    lft = _log_sigmoid(_dot_nt(wft_ref[...], h) + bft_ref[...])
    lft_ref[0, 0] = lft
    cum = _cumsum_rows(lf, tm) + carry_r[...]
    carry_r[...] = cum[tm - 1:tm, :]
    cumt = _cumsum_lanes(lft, tm) + carry_c[...]
    carry_c[...] = cumt[:, tm - 1:tm]
    hi, mid, lo = [p.astype(F32) for p in _split3(cum * LOG2E)]
    hit, midt, lot = [p.astype(F32) for p in _split3(cumt * LOG2E)]
    lane = lax.broadcasted_iota(jnp.int32, (tm, HD), 1)
    srow = lax.broadcasted_iota(jnp.int32, (HD, tm), 0)
    for hh in range(H):
        chi, cmid, clo = hi[:, hh:hh + 1], mid[:, hh:hh + 1], lo[:, hh:hh + 1]
        q_tail = jnp.where(lane == 0, chi, jnp.where(lane == 1, cmid, jnp.where(
            lane == 2, clo, jnp.where(lane < 6, 1.0, 0.0))))
        khi, kmid, klo = hit[hh:hh + 1, :], midt[hh:hh + 1, :], lot[hh:hh + 1, :]
        k_tail = jnp.where(srow < 3, 1.0, jnp.where(srow == 3, -khi, jnp.where(
            srow == 4, -kmid, jnp.where(srow == 5, -klo, 0.0))))
        qa_ref[0, hh, :, 0:HD] = q[:, hh * HD:(hh + 1) * HD].astype(BF16)
        qa_ref[0, hh, :, HD:2 * HD] = q_tail.astype(BF16)
        kat_ref[0, hh, 0, 0:HD, :] = kt[hh * HD:(hh + 1) * HD, :].astype(BF16)
        kat_ref[0, hh, 0, HD:2 * HD, :] = k_tail.astype(BF16)


def _fox_in_prompt(x, g, mod, wq, wkt, wvt, wv, wf, bf, wft, bft, rt, seq, layer, n_layers, prev):
    tm, tpg, b = rt.tm, rt.tpg, rt.groups
    full = lambda shape: pl.BlockSpec(shape, lambda i: (0,) * len(shape))
    row = pl.BlockSpec((tm, D), lambda i: (i, 0))
    if prev is None:
        prev = (jnp.zeros((n_layers, b, D, seq), F32), jnp.zeros((n_layers, b, D, seq), F32),
                jnp.zeros((n_layers, b, H, seq), F32))
    prev = tuple(prev)
    n_in = 12
    in_specs = [row, full((1, D)), rt.mod_spec(1), rt.mod_spec(0),
                full((D, D)), full((D, D)), full((D, D)), full((D, D)), full((D, LANES)),
                full((1, LANES)), full((H, D)), full((H, 1))]
    in_specs += [pl.BlockSpec(memory_space=pl.ANY)] * len(prev)
    state_spec = lambda rows: pl.BlockSpec((1, 1, rows, tm), lambda i: (layer, i // tpg, 0, i % tpg))
    out_specs = [state_spec(D), state_spec(D), state_spec(H),
                 pl.BlockSpec((1, H, tm, LANES), lambda i: (i // tpg, 0, i % tpg, 0)),
                 pl.BlockSpec((1, H, 1, LANES, tm), lambda i: (i // tpg, 0, i % tpg, 0, 0)),
                 pl.BlockSpec((1, NPAIR, tm, LANES), lambda i: (i // tpg, 0, i % tpg, 0))]
    out_shape = [jax.ShapeDtypeStruct((n_layers, b, D, seq), F32),
                 jax.ShapeDtypeStruct((n_layers, b, D, seq), F32),
                 jax.ShapeDtypeStruct((n_layers, b, H, seq), F32),
                 jax.ShapeDtypeStruct((b, H, seq, LANES), BF16),
                 jax.ShapeDtypeStruct((b, H, seq // tm, LANES, tm), BF16),
                 jax.ShapeDtypeStruct((b, NPAIR, seq, LANES), BF16)]
    return pl.pallas_call(
        functools.partial(_fox_in_prompt_kernel, tm=tm, tpg=tpg, n_prev=len(prev)),
        grid=(rt.n_tiles,),
        in_specs=in_specs, out_specs=out_specs, out_shape=out_shape,
        scratch_shapes=[pltpu.VMEM((1, LANES), F32), pltpu.VMEM((H, 1), F32)],
        input_output_aliases={n_in + t: t for t in range(len(prev))},
        compiler_params=_cparams(1),
    )(x, g, mod, mod, wq, wkt, wvt, wv, wf, bf, wft, bft, *prev)


def _attn_kernel(qa_ref, kat_ref, vp_ref, o_ref, m0, m1, l0, l1, a0, a1, sa0, sa1, sb0, sb1, *, tq, tk):
    qi = pl.program_id(2)
    m_s, l_s, acc_s, sa_s, sb_s = (m0, m1), (l0, l1), (a0, a1), (sa0, sa1), (sb0, sb1)
    q = [qa_ref[0, 0], qa_ref[0, 1]]
    for j in range(2):
        m_s[j][...] = jnp.full_like(m_s[j], NEG_INF)
        l_s[j][...] = jnp.zeros_like(l_s[j])
        acc_s[j][...] = jnp.zeros_like(acc_s[j])
        sa_s[j][...] = _dot(q[j], kat_ref[0, j, 0])

    def consume(j, kblk, s):
        v = vp_ref[0, 0, pl.ds(pl.multiple_of(kblk * tk, tk), tk), :]
        m_old = m_s[j][...]
        m_new = jnp.maximum(m_old, jnp.max(s, axis=-1, keepdims=True))
        alpha = jnp.exp2(m_old - m_new)
        p = jnp.exp2(s - jnp.concatenate([m_new] * (tk // LANES), axis=1))
        psum = p[:, 0:LANES]
        for t in range(1, tk // LANES):
            psum = psum + p[:, t * LANES:(t + 1) * LANES]
        l_s[j][...] = alpha * l_s[j][...] + psum
        acc_s[j][...] = alpha * acc_s[j][...] + _dot(p.astype(BF16), v)
        m_s[j][...] = m_new

    def scores(j, kblk):
        return _dot(q[j], kat_ref[0, j, kblk])

    def pair(k2, carry):
        kb = 2 * k2
        for j in range(2):
            sb_s[j][...] = scores(j, kb + 1)
            consume(j, kb, sa_s[j][...])
        for j in range(2):
            sa_s[j][...] = scores(j, kb + 2)
            consume(j, kb + 1, sb_s[j][...])
        return carry

    lax.fori_loop(0, qi // 2, pair, 0)

    def diagonal(j, s):
        r = lax.broadcasted_iota(jnp.int32, (tq, tk), 0)
        c = lax.broadcasted_iota(jnp.int32, (tq, tk), 1)
        consume(j, qi, jnp.where(c <= r, s, NEG_INF))

    @pl.when(qi % 2 == 1)
    def _():
        for j in range(2):
            sb_s[j][...] = scores(j, qi)
            consume(j, qi - 1, sa_s[j][...])
        for j in range(2):
            diagonal(j, sb_s[j][...])

    @pl.when(qi % 2 == 0)
    def _():
        for j in range(2):
            diagonal(j, sa_s[j][...])

    outs = [acc_s[j][...] / jnp.sum(l_s[j][...], axis=-1, keepdims=True) for j in range(2)]
    lane = lax.broadcasted_iota(jnp.int32, (tq, LANES), 1)
    o_ref[0, 0] = jnp.where(lane < HD, outs[0], outs[1]).astype(BF16)


def _attn_prompt(qa, kat, vp, tq):
    b, _, seq, _ = qa.shape
    nkb, tk = kat.shape[2], kat.shape[4]
    return pl.pallas_call(
        functools.partial(_attn_kernel, tq=tq, tk=tk),
        grid=(b, NPAIR, seq // tq),
        in_specs=[
            pl.BlockSpec((1, 2, tq, LANES), lambda bb, hp, qi: (bb, hp, qi, 0)),
            pl.BlockSpec((1, 2, nkb, LANES, tk), lambda bb, hp, qi: (bb, hp, 0, 0, 0)),
            pl.BlockSpec((1, 1, seq, LANES), lambda bb, hp, qi: (bb, hp, 0, 0)),
        ],
        out_specs=pl.BlockSpec((1, 1, tq, LANES), lambda bb, hp, qi: (bb, hp, qi, 0)),
        out_shape=jax.ShapeDtypeStruct((b, NPAIR, seq, LANES), BF16),
        scratch_shapes=[pltpu.VMEM((tq, LANES), F32)] * 6 + [pltpu.VMEM((tq, tk), F32)] * 4,
        compiler_params=_cparams(3),
    )(qa, kat, vp)


def _attn_sample_kernel(q_ref, kn_ref, vn_ref, lfn_ref, ck_ref, cv_ref, clf_ref, o_ref,
                        m_s, l_s, acc_s, carry_s, *, sq, tk, nkc):
    kc = pl.program_id(1)
    nrow = H * sq

    @pl.when(kc == 0)
    def _():
        m_s[...] = jnp.full_like(m_s, NEG_INF)
        l_s[...] = jnp.zeros_like(l_s)
        acc_s[...] = jnp.zeros_like(acc_s)
        carry_s[...] = jnp.zeros_like(carry_s)

    rowh = lax.broadcasted_iota(jnp.int32, (nrow, D), 0) // sq
    colh = lax.broadcasted_iota(jnp.int32, (nrow, D), 1) // HD
    q = q_ref[0]
    qbd = jnp.where(rowh == colh, jnp.concatenate([q] * H, axis=0), 0.0).astype(BF16)

    def expand_heads(ct):
        n = ct.shape[1]
        return jnp.concatenate([jnp.broadcast_to(ct[hh:hh + 1, :], (sq, n)) for hh in range(H)], axis=0)

    def update(s, vb, v_transposed):
        m_old = m_s[...]
        m_new = jnp.maximum(m_old, jnp.max(s, axis=-1, keepdims=True))
        alpha = jnp.exp(m_old - m_new)
        p = jnp.exp(s - m_new)
        l_s[...] = alpha * l_s[...] + jnp.sum(p, axis=-1, keepdims=True)
        pb = p.astype(BF16)
        pv = _dot_nt(pb, vb) if v_transposed else _dot(pb, vb)
        acc_s[...] = alpha * acc_s[...] + pv
        m_s[...] = m_new

    sub = 256
    cums = []
    carry = carry_s[...]
    for t in range(tk // sub):
        cpart = _cumsum_lanes(clf_ref[0, :, t * sub:(t + 1) * sub], sub) + carry
        carry = cpart[:, sub - 1:sub]
        cums.append(cpart)
    carry_s[...] = carry
    c_keys = jnp.concatenate(cums, axis=1)
    s = _dot(qbd, ck_ref[0].astype(BF16)) - expand_heads(c_keys)
    update(s, cv_ref[0].astype(BF16), True)

    @pl.when(kc == nkc - 1)
    def _():
        c_new = _cumsum_lanes(lfn_ref[0], LANES) + carry
        pad = jnp.zeros((LANES - sq, D), F32)
        kn = jnp.concatenate([kn_ref[0], pad], axis=0).astype(BF16)
        vn = jnp.concatenate([vn_ref[0], pad], axis=0).astype(BF16)
        c_rows = expand_heads(c_new)
        rq = lax.broadcasted_iota(jnp.int32, (nrow, LANES), 0) % sq
        kidx = lax.broadcasted_iota(jnp.int32, (nrow, LANES), 1)
        cq = jnp.sum(jnp.where(kidx == rq, c_rows, 0.0), axis=-1, keepdims=True)
        s_new = _dot_nt(qbd, kn) + (cq - c_rows)
        s_new = jnp.where(kidx <= rq, s_new, NEG_INF)
        m_s[...] = m_s[...] + cq
        update(s_new, vn, False)
        r = acc_s[...] / l_s[...]
        r = jnp.where(rowh == colh, r, 0.0)
        out = r[0:sq, :]
        for hh in range(1, H):
            out = out + r[hh * sq:(hh + 1) * sq, :]
        o_ref[0] = out


def _attn_sample(q, k_new, v_new, lf_new, cache_k, cache_v, cache_lf, layer):
    nb, sq, _ = q.shape
    past = cache_k.shape[3]
    tk = 1024
    nkc = past // tk
    cur = pl.BlockSpec((1, sq, D), lambda b, kc: (b, 0, 0))
    return pl.pallas_call(
        functools.partial(_attn_sample_kernel, sq=sq, tk=tk, nkc=nkc),
        grid=(nb, nkc),
        in_specs=[cur, cur, cur,
                  pl.BlockSpec((1, H, LANES), lambda b, kc: (b, 0, 0)),
                  pl.BlockSpec((None, 1, D, tk), lambda b, kc: (layer, b, 0, kc)),
                  pl.BlockSpec((None, 1, D, tk), lambda b, kc: (layer, b, 0, kc)),
                  pl.BlockSpec((None, 1, H, tk), lambda b, kc: (layer, b, 0, kc))],
        out_specs=cur,
        out_shape=jax.ShapeDtypeStruct((nb, sq, D), F32),
        scratch_shapes=[pltpu.VMEM((H * sq, 1), F32), pltpu.VMEM((H * sq, 1), F32),
                        pltpu.VMEM((H * sq, D), F32), pltpu.VMEM((H, 1), F32)],
        compiler_params=_cparams(2),
    )(q, k_new, v_new, lf_new, cache_k, cache_v, cache_lf)


def _lru_kernel(x_ref, g_ref, sc_ref, sh_ref, wx_ref, wg_ref, cw_ref, cb_ref, wa_ref, ba_ref,
                wi_ref, bi_ref, lam_ref, h0_ref, c0_ref, y_ref, hl_ref, cn_ref, xp_s, hc_s,
                *, tm, tpg, stride, hdr):
    i = pl.program_id(0)
    nst = (CONV_W - 1) * stride

    @pl.when(i % tpg == 0)
    def _():
        xp_s[hdr - nst:hdr, :] = c0_ref[0]
        hc_s[...] = h0_ref[0]

    h = _norm_mod(x_ref[...], g_ref[...], sc_ref[...], sh_ref[...]).astype(BF16)
    xb = _dot(h, wx_ref[...])
    gb = _dot(h, wg_ref[...])
    xp_s[hdr:hdr + tm, :] = xb
    xc = cb_ref[...] + xb * cw_ref[CONV_W - 1:CONV_W, :]
    for j in range(CONV_W - 1):
        d = (CONV_W - 1 - j) * stride
        shifted = jnp.concatenate([xp_s[hdr - d:2 * hdr - d, :], pltpu.roll(xb, d, 0)[hdr:, :]], axis=0)
        xc = xc + shifted * cw_ref[j:j + 1, :]
    tail = xp_s[hdr + tm - nst:hdr + tm, :]
    cn_ref[0] = tail
    xp_s[hdr - nst:hdr, :] = tail

    xcb = xc.astype(BF16)
    r = _sigmoid(_dot(xcb, wa_ref[...]) + ba_ref[...])
    gi = _sigmoid(_dot(xcb, wi_ref[...]) + bi_ref[...])
    log_a = LRU_C * r * _log_sigmoid(lam_ref[...])
    a = jnp.exp(log_a)
    u = jnp.sqrt(-jnp.tanh(log_a) * (1.0 + a * a)) * gi * xc

    cr = min(SCAN_ROWS, tm) if stride == 1 else tm
    row = lax.broadcasted_iota(jnp.int32, (tm, 1), 0) % cr
    sft = stride
    while sft < cr:
        keep = row >= sft
        a_sh = pltpu.roll(a, sft, 0)
        u_sh = pltpu.roll(u, sft, 0)
        u = jnp.where(keep, a * u_sh + u, u)
        a = jnp.where(keep, a * a_sh, a)
        sft *= 2
    hc = hc_s[...]
    if stride == 1:
        parts = []
        for c in range(tm // cr):
            part = a[c * cr:(c + 1) * cr, :] * hc + u[c * cr:(c + 1) * cr, :]
            hc = part[cr - 1:cr, :]
            parts.append(part)
        hs = jnp.concatenate(parts, axis=0)
    else:
        hs = a * jnp.concatenate([hc] * (tm // stride), axis=0) + u
    last = hs[tm - stride:tm, :]
    hc_s[...] = last
    hl_ref[0] = last
    y_ref[...] = (hs * _gelu_tanh(gb)).astype(BF16)


def _lru_mix(x, g, mod, wx, wg, cw, cb, wa, ba, wi, bi, lam, h0, c0, rt, stride):
    tm, tpg = rt.tm, rt.tpg
    hdr = -(-(CONV_W - 1) * stride // 8) * 8
    nst = (CONV_W - 1) * stride
    full = lambda shape: pl.BlockSpec(shape, lambda i: (0,) * len(shape))
    row = pl.BlockSpec((tm, D), lambda i: (i, 0))
    grp = rt.groups
    return pl.pallas_call(
        functools.partial(_lru_kernel, tm=tm, tpg=tpg, stride=stride, hdr=hdr),
        grid=(rt.n_tiles,),
        in_specs=[row, full((1, D)), rt.mod_spec(1), rt.mod_spec(0),
                  full((D, D)), full((D, D)), full((CONV_W, D)), full((1, D)),
                  full((D, D)), full((1, D)), full((D, D)), full((1, D)), full((1, D)),
                  pl.BlockSpec((1, stride, D), lambda i: (i // tpg, 0, 0)),
                  pl.BlockSpec((1, nst, D), lambda i: (i // tpg, 0, 0))],
        out_specs=[row,
                   pl.BlockSpec((1, stride, D), lambda i: (i // tpg, 0, 0)),
                   pl.BlockSpec((1, nst, D), lambda i: (i // tpg, 0, 0))],
        out_shape=[jax.ShapeDtypeStruct((rt.rows, D), BF16),
                   jax.ShapeDtypeStruct((grp, stride, D), F32),
                   jax.ShapeDtypeStruct((grp, nst, D), F32)],
        scratch_shapes=[pltpu.VMEM((hdr + tm, D), F32), pltpu.VMEM((stride, D), F32)],
        compiler_params=_cparams(1),
    )(x, g, mod, mod, wx, wg, cw, cb, wa, ba, wi, bi, lam, h0, c0)


def _mix_in(a_ref, pair_major):
    if pair_major:
        return jnp.concatenate([a_ref[0, hp] for hp in range(NPAIR)], axis=1)
    return a_ref[...]


def _ffn_kernel(x_ref, a_ref, ga_ref, wo_ref, g2_ref, sc_ref, sh_ref, gm_ref, wg_ref, wu_ref, wd_ref,
                o_ref, x1_s, h_s, acc_s, *, nf, pair_major):
    f = pl.program_id(1)

    @pl.when(f == 0)
    def _():
        a = _mix_in(a_ref, pair_major).astype(BF16)
        x1 = x_ref[...] + ga_ref[...] * _dot(a, wo_ref[...])
        x1_s[...] = x1
        h_s[...] = _norm_mod(x1, g2_ref[...], sc_ref[...], sh_ref[...]).astype(BF16)
        acc_s[...] = jnp.zeros_like(acc_s)

    h = h_s[...]
    act = (_silu(_dot(h, wg_ref[...])) * _dot(h, wu_ref[...])).astype(BF16)
    acc_s[...] += _dot(act, wd_ref[...])

    @pl.when(f == nf - 1)
    def _():
        o_ref[...] = x1_s[...] + gm_ref[...] * acc_s[...]


def _route_kernel(x_ref, a_ref, ga_ref, wo_ref, g2_ref, sc_ref, sh_ref, wr_ref,
                  x1_ref, h_ref, route_ref, gate_ref, cnt_ref, carry_s, *, tm):
    i = pl.program_id(0)

    @pl.when(i == 0)
    def _():
        carry_s[...] = jnp.zeros_like(carry_s)

    lane = lax.broadcasted_iota(jnp.int32, (tm, LANES), 1)
    x1 = x_ref[...] + ga_ref[...] * _dot(a_ref[...].astype(BF16), wo_ref[...])
    x1_ref[...] = x1
    hf = _norm_mod(x1, g2_ref[...], sc_ref[...], sh_ref[...])
    _to_slabs(h_ref, hf, tm)
    h_hi, h_lo, _ = _split3(hf)
    w_hi, w_lo, _ = _split3(wr_ref[...])
    logits = _dot(h_hi, w_hi) + (_dot(h_hi, w_lo) + _dot(h_lo, w_hi))
    logits = jnp.where(lane < NEXP, logits, NEG_INF)
    m1 = jnp.max(logits, axis=-1, keepdims=True)
    i1 = jnp.min(jnp.where(logits == m1, lane, LANES), axis=-1, keepdims=True)
    rest = jnp.where(lane == i1, NEG_INF, logits)
    m2 = jnp.max(rest, axis=-1, keepdims=True)
    i2 = jnp.min(jnp.where(rest == m2, lane, LANES), axis=-1, keepdims=True)
    t = jnp.exp(m2 - m1)
    gate_ref[...] = jnp.where(lane == 0, 1.0 / (1.0 + t), jnp.where(lane == 1, t / (1.0 + t), 0.0))
    sel = (lane == i1) | (lane == i2)
    r = lax.broadcasted_iota(jnp.int32, (tm, tm), 0)
    c = lax.broadcasted_iota(jnp.int32, (tm, tm), 1)
    before = _dot((c < r).astype(BF16), sel.astype(BF16)) + carry_s[...]
    rank1 = jnp.sum(jnp.where(lane == i1, before, 0.0), axis=-1, keepdims=True).astype(jnp.int32)
    rank2 = jnp.sum(jnp.where(lane == i2, before, 0.0), axis=-1, keepdims=True).astype(jnp.int32)
    route_ref[...] = jnp.where(lane == 0, i1, jnp.where(lane == 1, i2, jnp.where(
        lane == 2, rank1, jnp.where(lane == 3, rank2, 0))))
    total = carry_s[...] + jnp.sum(sel.astype(F32), axis=0, keepdims=True)
    carry_s[...] = total
    cnt_ref[...] = total.astype(jnp.int32)


def _to_slabs(ref, x, n):
    for c in range(SLAB):
        ref[pl.ds(c, n, stride=SLAB), :] = x[:, c * LANES:(c + 1) * LANES]


def _from_slabs(ref, n):
    return jnp.concatenate([ref[pl.ds(c, n, stride=SLAB), :] for c in range(SLAB)], axis=1)


def _slab(ref, t):
    return ref.at[pl.ds(pl.multiple_of(SLAB * t, SLAB), SLAB), :]


def _dispatch_kernel(dest_ref, h_ref, zero_ref, o_ref, sem, *, tm):
    del zero_ref
    base = pl.program_id(0) * tm

    def issue(r, carry):
        for s in range(2):
            d = dest_ref[2 * (base + r) + s]
            pltpu.make_async_copy(_slab(h_ref, r), _slab(o_ref, d), sem.at[s]).start()
        return carry

    lax.fori_loop(0, tm, issue, 0, unroll=4)
    for s in range(2):
        pltpu.make_async_copy(h_ref, o_ref.at[pl.ds(0, tm * SLAB), :], sem.at[s]).wait()


def _experts_kernel(te_ref, nu_ref, hs_ref, wg_ref, wu_ref, wd_ref, o_ref, acc_s, *, nf, tg):
    del te_ref
    i = pl.program_id(0)
    f = pl.program_id(1)

    def partial_out():
        h = _from_slabs(hs_ref, tg).astype(BF16)
        act = (_silu(_dot(h, wg_ref[0, 0])) * _dot(h, wu_ref[0, 0])).astype(BF16)
        return _dot(act, wd_ref[0, 0])

    used = i < nu_ref[0]

    @pl.when(used & (f == 0))
    def _():
        acc_s[...] = partial_out()

    @pl.when(used & (f > 0) & (f < nf - 1))
    def _():
        acc_s[...] += partial_out()

    @pl.when(used & (f == nf - 1))
    def _():
        _to_slabs(o_ref, acc_s[...] + partial_out(), tg)

    @pl.when((i >= nu_ref[0]) & (f == 0))
    def _():
        o_ref[...] = jnp.zeros_like(o_ref)


def _combine_kernel(dest_ref, x1_ref, gm_ref, gate_ref, gf_ref, y_ref, o_ref, buf_s, sem,
                    *, tm, n_tiles, final_norm):
    i = pl.program_id(0)

    def start_tile(t, st):
        def issue(r, carry):
            for s in range(2):
                d = dest_ref[2 * (t * tm + r) + s]
                pltpu.make_async_copy(_slab(y_ref, d), _slab(buf_s.at[st, s], r), sem.at[st, s]).start()
            return carry

        lax.fori_loop(0, tm, issue, 0, unroll=4)

    @pl.when(i == 0)
    def _():
        start_tile(0, 0)

    @pl.when(i + 1 < n_tiles)
    def _():
        start_tile(i + 1, (i + 1) % 2)

    st = i % 2
    for s in range(2):
        pltpu.make_async_copy(y_ref.at[pl.ds(0, tm * SLAB), :], buf_s.at[st, s], sem.at[st, s]).wait()
    gate = gate_ref[...]
    y = gate[:, 0:1] * _from_slabs(buf_s.at[st, 0], tm) + gate[:, 1:2] * _from_slabs(buf_s.at[st, 1], tm)
    x2 = x1_ref[...] + gm_ref[...] * y
    if final_norm:
        ms = jnp.mean(x2 * x2, axis=-1, keepdims=True)
        x2 = x2 * lax.rsqrt(ms + EPS) * gf_ref[...]
    o_ref[...] = x2


def _post_ffn(x, a, mod, wo, g2, w_gu, w_down, layer, rt, pair_major):
    tm, tpg = rt.tm, rt.tpg
    nf = DFF // TF
    row = pl.BlockSpec((tm, D), lambda i, f: (i, 0))
    if pair_major:
        a_spec = pl.BlockSpec((1, NPAIR, tm, LANES), lambda i, f: (i // tpg, 0, i % tpg, 0))
    else:
        a_spec = row
    c2 = lambda shape: pl.BlockSpec(shape, lambda i, f: (0,) * len(shape))
    return pl.pallas_call(
        functools.partial(_ffn_kernel, nf=nf, pair_major=pair_major),
        grid=(rt.n_tiles, nf),
        in_specs=[row, a_spec, rt.mod_spec(2), c2((D, D)), c2((1, D)),
                  rt.mod_spec(4), rt.mod_spec(3), rt.mod_spec(5),
                  pl.BlockSpec((None, D, TF), lambda i, f: (layer, 0, f)),
                  pl.BlockSpec((None, D, TF), lambda i, f: (layer, 0, nf + f)),
                  pl.BlockSpec((None, TF, D), lambda i, f: (layer, f, 0))],
        out_specs=row,
        out_shape=jax.ShapeDtypeStruct((rt.rows, D), F32),
        scratch_shapes=[pltpu.VMEM((tm, D), F32), pltpu.VMEM((tm, D), BF16), pltpu.VMEM((tm, D), F32)],
        compiler_params=_cparams(2),
    )(x, a, mod, wo, g2, mod, mod, mod, w_gu, w_gu, w_down)


def _post_moe(x, a, mod, wo, g2, w_router, w_gu, w_down, layer, rt, final_g):
    tm, rows, n_tiles = rt.tm, rt.rows, rt.n_tiles
    nf = DFF // TF_MOE
    assert nf >= 2
    row = pl.BlockSpec((tm, D), lambda i, *_: (i, 0))
    lanes = pl.BlockSpec((tm, LANES), lambda i, *_: (i, 0))
    full = lambda shape: pl.BlockSpec(shape, lambda i, *_: (0,) * len(shape))
    rows_f32 = jax.ShapeDtypeStruct((rows, D), F32)
    slabs = lambda n: pl.BlockSpec((n * SLAB, LANES), lambda i, *_: (i, 0))
    tg = min(TG, rows // 2)
    x1, hf, route, gate, cnt = pl.pallas_call(
        functools.partial(_route_kernel, tm=tm),
        grid=(n_tiles,),
        in_specs=[row, row, rt.mod_spec(2), full((D, D)), full((1, D)), rt.mod_spec(4), rt.mod_spec(3),
                  full((D, LANES))],
        out_specs=[row, slabs(tm), lanes, lanes, full((1, LANES))],
        out_shape=[rows_f32, jax.ShapeDtypeStruct((rows * SLAB, LANES), F32),
                   jax.ShapeDtypeStruct((rows, LANES), jnp.int32),
                   jax.ShapeDtypeStruct((rows, LANES), F32), jax.ShapeDtypeStruct((1, LANES), jnp.int32)],
        scratch_shapes=[pltpu.VMEM((1, LANES), F32)],
        compiler_params=_cparams(1),
    )(x, a, mod, wo, g2, mod, mod, w_router)

    n_tiles_g = -(-(2 * rows + NEXP * (tg - 1)) // tg)
    tiles_e = (cnt[0, :NEXP] + (tg - 1)) // tg
    tile_end = jnp.cumsum(tiles_e)
    start = (tile_end - tiles_e) * tg
    is_e = route[:, 0:2, None] == jnp.arange(NEXP, dtype=jnp.int32)
    dest = (route[:, 2:4] + jnp.sum(jnp.where(is_e, start, 0), axis=-1)).reshape(-1).astype(jnp.int32)
    tile_ids = jnp.arange(n_tiles_g, dtype=jnp.int32)
    tile_expert = jnp.minimum(jnp.sum(tile_ids[:, None] >= tile_end[None, :], axis=1), NEXP - 1)
    tile_expert = tile_expert.astype(jnp.int32)
    n_used = tile_end[NEXP - 1:].astype(jnp.int32)

    sorted_h = pl.pallas_call(
        functools.partial(_dispatch_kernel, tm=tm),
        grid_spec=pltpu.PrefetchScalarGridSpec(
            num_scalar_prefetch=1, grid=(n_tiles,),
            in_specs=[slabs(tm), pl.BlockSpec(memory_space=pl.ANY)],
            out_specs=pl.BlockSpec(memory_space=pl.ANY),
            scratch_shapes=[pltpu.SemaphoreType.DMA((2,))]),
        out_shape=jax.ShapeDtypeStruct((n_tiles_g * tg * SLAB, LANES), F32),
        input_output_aliases={2: 0},
        compiler_params=_cparams(1),
    )(dest, hf, jnp.zeros((n_tiles_g * tg * SLAB, LANES), F32))

    def tile_of(i, nu):
        return jnp.maximum(jnp.minimum(i, nu[0] - 1), 0)

    def chunk_of(i, f, nu):
        return jnp.where(i < nu[0], f, nf - 1)

    y_sorted = pl.pallas_call(
        functools.partial(_experts_kernel, nf=nf, tg=tg),
        grid_spec=pltpu.PrefetchScalarGridSpec(
            num_scalar_prefetch=2, grid=(n_tiles_g, nf),
            in_specs=[
                pl.BlockSpec((tg * SLAB, LANES), lambda i, f, te, nu: (tile_of(i, nu), 0)),
                pl.BlockSpec((1, 1, D, TF_MOE),
                             lambda i, f, te, nu: (layer, te[tile_of(i, nu)], 0, chunk_of(i, f, nu))),
                pl.BlockSpec((1, 1, D, TF_MOE),
                             lambda i, f, te, nu: (layer, te[tile_of(i, nu)], 0, nf + chunk_of(i, f, nu))),
                pl.BlockSpec((1, 1, TF_MOE, D),
                             lambda i, f, te, nu: (layer, te[tile_of(i, nu)], chunk_of(i, f, nu), 0))],
            out_specs=pl.BlockSpec((tg * SLAB, LANES), lambda i, f, te, nu: (i, 0)),
            scratch_shapes=[pltpu.VMEM((tg, D), F32)]),
        out_shape=jax.ShapeDtypeStruct((n_tiles_g * tg * SLAB, LANES), F32),
        compiler_params=_cparams(2),
    )(tile_expert, n_used, sorted_h, w_gu, w_gu, w_down)

    gf = jnp.ones((1, D), F32) if final_g is None else final_g
    return pl.pallas_call(
        functools.partial(_combine_kernel, tm=tm, n_tiles=n_tiles, final_norm=final_g is not None),
        grid_spec=pltpu.PrefetchScalarGridSpec(
            num_scalar_prefetch=1, grid=(n_tiles,),
            in_specs=[row, rt.mod_spec(5), lanes, full((1, D)), pl.BlockSpec(memory_space=pl.ANY)],
            out_specs=row,
            scratch_shapes=[pltpu.VMEM((2, 2, tm * SLAB, LANES), F32),
                            pltpu.SemaphoreType.DMA((2, 2))]),
        out_shape=rows_f32,
        compiler_params=_cparams(1),
    )(dest, x1, mod, gate, gf, y_sorted)


def _final_norm_kernel(x_ref, g_ref, o_ref):
    x = x_ref[...]
    ms = jnp.mean(x * x, axis=-1, keepdims=True)
    o_ref[...] = x * lax.rsqrt(ms + EPS) * g_ref[...]


def _final_norm(x, g, tm):
    rows = x.shape[0]
    return pl.pallas_call(
        _final_norm_kernel,
        grid=(rows // tm,),
        in_specs=[pl.BlockSpec((tm, D), lambda i: (i, 0)), pl.BlockSpec((1, D), lambda i: (0, 0))],
        out_specs=pl.BlockSpec((tm, D), lambda i: (i, 0)),
        out_shape=jax.ShapeDtypeStruct((rows, D), F32),
        compiler_params=_cparams(1),
    )(x, g)


def _prep_weights(p):
    w = {}
    fw = p['fox_w_in']
    w['wq'] = fw[:, :, 0 * D:1 * D].astype(BF16)
    w['wk'] = fw[:, :, 1 * D:2 * D].astype(BF16)
    w['wv'] = fw[:, :, 2 * D:3 * D].astype(BF16)
    w['wf'] = jnp.pad(fw[:, :, 3 * D:], ((0, 0), (0, 0), (0, LANES - H))).astype(BF16)
    w['bf'] = jnp.pad(p['fox_b_f'], ((0, 0), (0, LANES - H)))[:, None, :]
    w['wkt'] = w['wk'].transpose(0, 2, 1)
    w['wvt'] = w['wv'].transpose(0, 2, 1)
    w['wft'] = fw[:, :, 3 * D:].transpose(0, 2, 1).astype(BF16)
    w['bft'] = p['fox_b_f'][:, :, None]
    w['fox_wo'] = p['fox_w_out'].astype(BF16)
    w['lru_wx'] = p['lru_w_in'][:, :, :D].astype(BF16)
    w['lru_wg'] = p['lru_w_in'][:, :, D:].astype(BF16)
    nblk = p['lru_w_a'].shape[1]
    eye = jnp.eye(nblk, dtype=F32)
    bd = lambda m: jnp.einsum('lnkj,nm->lnkmj', m, eye).reshape(m.shape[0], D, D).astype(BF16)
    w['lru_wa'] = bd(p['lru_w_a'])
    w['lru_wi'] = bd(p['lru_w_i'])
    w['lru_wo'] = p['lru_w_out'].astype(BF16)
    w['ffn_gu'] = p['ffn_w_gu'].astype(BF16)
    w['ffn_down'] = p['ffn_w_down'].astype(BF16)
    w['router'] = jnp.pad(p['moe_router'], ((0, 0), (0, 0), (0, LANES - NEXP)))
    w['moe_gu'] = p['moe_w_gu'].astype(BF16)
    w['moe_down'] = p['moe_w_down'].astype(BF16)
    return w


def _trunk(x, mods, p, w, rts, prompt, seq, fox_past, lru_past):
    rt, rt_ffn, rt_moe = rts
    depth = p['norm_mix_g'].shape[0]
    n_fox = w['wq'].shape[0]
    final_g = p['final_g'][None, :]
    fox_new, lru_new = [], []
    for l in range(depth):
        j = l // 2
        last = l == depth - 1
        mod = mods[l]
        g1 = p['norm_mix_g'][l][None, :]
        g2 = p['norm_ffn_g'][l][None, :]
        if l % 2 == 0:
            if prompt:
                kt, vt, lft, qa, kat, vp = _fox_in_prompt(
                    x, g1, mod, w['wq'][j], w['wkt'][j], w['wvt'][j], w['wv'][j], w['wf'][j], w['bf'][j],
                    w['wft'][j], w['bft'][j], rt, seq, j, n_fox, fox_new[0] if fox_new else None)
                a = _attn_prompt(qa, kat, vp, tq=kat.shape[4])
                fox_new = [(kt, vt, lft)]
            else:
                k, v, lf, q = _fox_in_sample(x, g1, mod, w['wq'][j], w['wk'][j], w['wv'][j], w['wf'][j],
                                             w['bf'][j], rt)
                nb = rt.rows // seq
                to_b = lambda t: t.reshape(seq, nb, -1).transpose(1, 0, 2)
                kb, vb, lfb = to_b(k), to_b(v), to_b(lf)
                ck, cv, clf = fox_past
                lfn_t = jnp.pad(lfb.transpose(0, 2, 1), ((0, 0), (0, 0), (0, LANES - seq)))
                ab = _attn_sample(to_b(q), kb, vb, lfn_t, ck, cv, clf, j)
                a = ab.transpose(1, 0, 2).reshape(rt.rows, D)
                fox_new.append((kb, vb, lfb))
            x = _post_ffn(x, a, mod, w['fox_wo'][j], g2, w['ffn_gu'], w['ffn_down'], j, rt_ffn,
                          pair_major=prompt)
            if last:
                x = _final_norm(x, final_g, rt.tm)
        else:
            h0, c0, stride = lru_past[j]
            y, hl, cn = _lru_mix(x, g1, mod, w['lru_wx'][j], w['lru_wg'][j], p['lru_conv_w'][j],
                                 p['lru_conv_b'][j][None, :], w['lru_wa'][j], p['lru_b_a'][j][None, :],
                                 w['lru_wi'][j], p['lru_b_i'][j][None, :], p['lru_lam'][j][None, :],
                                 h0, c0, rt, stride)
            lru_new.append((hl, cn))
            x = _post_moe(x, y, mod, w['lru_wo'][j], g2, w['router'][j], w['moe_gu'], w['moe_down'], j,
                          rt_moe, final_g if last else None)
    return x, fox_new, lru_new


def kernel(x_prompt, x_sample, c_prompt, c_sample, cache_k, cache_v, cache_logf, state_h, state_conv,
           norm_mix_g, norm_ffn_g, final_g, w_ada, b_ada, fox_w_in, fox_b_f, fox_w_out,
           lru_w_in, lru_conv_w, lru_conv_b, lru_w_a, lru_b_a, lru_w_i, lru_b_i, lru_lam, lru_w_out,
           ffn_w_gu, ffn_w_down, moe_router, moe_w_gu, moe_w_down):
    p = {
        'norm_mix_g': norm_mix_g, 'norm_ffn_g': norm_ffn_g, 'final_g': final_g,
        'fox_w_in': fox_w_in, 'fox_b_f': fox_b_f, 'fox_w_out': fox_w_out,
        'lru_w_in': lru_w_in, 'lru_conv_w': lru_conv_w, 'lru_conv_b': lru_conv_b,
        'lru_w_a': lru_w_a, 'lru_b_a': lru_b_a, 'lru_w_i': lru_w_i, 'lru_b_i': lru_b_i,
        'lru_lam': lru_lam, 'lru_w_out': lru_w_out,
        'ffn_w_gu': ffn_w_gu, 'ffn_w_down': ffn_w_down,
        'moe_router': moe_router, 'moe_w_gu': moe_w_gu, 'moe_w_down': moe_w_down,
    }
    bp, sp, _ = x_prompt.shape
    bs, ss, _ = x_sample.shape
    depth = w_ada.shape[0]
    n_fox, n_lru = cache_k.shape[0], state_h.shape[0]
    past = cache_k.shape[2]
    w = _prep_weights(p)

    nc = bp + bs
    ncp = -(-nc // 8) * 8
    c_all = jnp.pad(jnp.concatenate([c_prompt, c_sample], axis=0), ((0, ncp - nc), (0, 0)))
    mod_all = _ada_mod(c_all, w_ada, b_ada)
    mods_p = [mod_all[l, :bp].reshape(bp, 6, 1, D).transpose(1, 0, 2, 3) for l in range(depth)]
    mods_s = [jnp.tile(mod_all[l, bp:nc].reshape(bs, 6, D).transpose(1, 0, 2), (1, ss, 1))[:, None]
              for l in range(depth)]

    rts_p = tuple(_Rows(bp * sp, tm, bp, per_row_mod=False) for tm in (TM_MIX, TM_FFN, TM_MOE))
    zeros_h = jnp.zeros((bp, 1, D), F32)
    zeros_c = jnp.zeros((bp, CONV_W - 1, D), F32)
    y_p, fox_p, lru_p = _trunk(x_prompt.reshape(bp * sp, D), mods_p, p, w, rts_p, True, sp,
                               None, [(zeros_h, zeros_c, 1)] * n_lru)

    rt_s = _Rows(bs * ss, bs * ss, 1, per_row_mod=True)
    xs = x_sample.transpose(1, 0, 2).reshape(ss * bs, D)
    ckt = cache_k.transpose(0, 1, 3, 4, 2).reshape(n_fox, bs, D, past)
    cvt = cache_v.transpose(0, 1, 3, 4, 2).reshape(n_fox, bs, D, past)
    clft = cache_logf.transpose(0, 1, 3, 2)
    lru_past = [(state_h[j][None], state_conv[j].transpose(1, 0, 2).reshape(1, (CONV_W - 1) * bs, D), bs)
                for j in range(n_lru)]
    y_s, fox_s, lru_s = _trunk(xs, mods_s, p, w, (rt_s, rt_s, rt_s), False, ss, (ckt, cvt, clft), lru_past)

    y_prompt = y_p.reshape(bp, sp, D)
    y_sample = y_s.reshape(ss, bs, D).transpose(1, 0, 2)
    kt_all, vt_all, lft_all = fox_p[0]
    k_prompt = kt_all.reshape(n_fox, bp, H, HD, sp).transpose(0, 1, 4, 2, 3)
    v_prompt = vt_all.reshape(n_fox, bp, H, HD, sp).transpose(0, 1, 4, 2, 3)
    logf_prompt = lft_all.transpose(0, 1, 3, 2)
    h_prompt = jnp.stack([s[0][:, 0, :] for s in lru_p])
    conv_prompt = jnp.stack([s[1] for s in lru_p])
    k_sample = jnp.stack([f[0] for f in fox_s]).reshape(n_fox, bs, ss, H, HD)
    v_sample = jnp.stack([f[1] for f in fox_s]).reshape(n_fox, bs, ss, H, HD)
    logf_sample = jnp.stack([f[2] for f in fox_s])
    h_sample = jnp.stack([s[0][0] for s in lru_s])
    conv_sample = jnp.stack([s[1][0].reshape(CONV_W - 1, bs, D).transpose(1, 0, 2) for s in lru_s])
    return (y_prompt, y_sample, k_prompt, v_prompt, logf_prompt, h_prompt, conv_prompt,
            k_sample, v_sample, logf_sample, h_sample, conv_sample)
```

```python
import functools
import math

import jax
import jax.numpy as jnp
from jax import lax
from jax.experimental import pallas as pl
from jax.experimental.pallas import tpu as pltpu

F32 = jnp.float32
BF16 = jnp.bfloat16

D = 1024
H = 16
HD = 64
NPAIR = H // 2
LANES = 128
DFF = 3584
NEXP = 8
CONV_W = 4
LRU_C = 8.0
EPS = 1e-6
QK_SCALE = HD ** -0.5
LOG2E = math.log2(math.e)
NEG_INF = float("-inf")
VMEM_LIMIT = 56 * 1024 * 1024
TM_MIX = 512
TM_FFN = 512
TM_MOE = 512
TF = 1792
TF_MOE = 1792
TG = 512
SLAB = D // LANES
SCAN_ROWS = 8


def _cparams(n_axes):
    return pltpu.CompilerParams(dimension_semantics=("arbitrary",) * n_axes,
                                vmem_limit_bytes=VMEM_LIMIT)


def _dot(a, b):
    return jnp.dot(a, b, preferred_element_type=F32)


def _dot_nt(a, b):
    return lax.dot_general(a, b, (((1,), (1,)), ((), ())), preferred_element_type=F32)


def _norm_mod(x, g, sc, sh):
    ms = jnp.mean(x * x, axis=-1, keepdims=True)
    return (x * lax.rsqrt(ms + EPS) * g) * (1.0 + sc) + sh


def _log_sigmoid(x):
    return jnp.minimum(x, 0.0) - jnp.log1p(jnp.exp(-jnp.abs(x)))


def _sigmoid(x):
    return 0.5 * (1.0 + jnp.tanh(0.5 * x))


def _silu(x):
    return x * _sigmoid(x)


def _gelu_tanh(x):
    return 0.5 * x * (1.0 + jnp.tanh(math.sqrt(2.0 / math.pi) * (x + 0.044715 * (x * x * x))))


def _split3(x):
    hi = x.astype(BF16)
    r = x - hi.astype(F32)
    mid = r.astype(BF16)
    lo = (r - mid.astype(F32)).astype(BF16)
    return hi, mid, lo


def _cumsum_rows(x, n):
    r = lax.broadcasted_iota(jnp.int32, (n, n), 0)
    c = lax.broadcasted_iota(jnp.int32, (n, n), 1)
    tri = (c <= r).astype(BF16)
    hi, mid, lo = _split3(x)
    return _dot(tri, hi) + _dot(tri, mid) + _dot(tri, lo)


def _cumsum_lanes(x, n):
    r = lax.broadcasted_iota(jnp.int32, (n, n), 0)
    c = lax.broadcasted_iota(jnp.int32, (n, n), 1)
    tri = (r <= c).astype(BF16)
    hi, mid, lo = _split3(x)
    return _dot(hi, tri) + _dot(mid, tri) + _dot(lo, tri)


def _mod_kernel(c_ref, w_ref, b_ref, o_ref):
    c = c_ref[...]
    s = _silu(c).astype(BF16)
    o_ref[0] = _dot(s, w_ref[0].astype(BF16)) + b_ref[0]


def _ada_mod(c_all, w_ada, b_ada):
    nl, _, nmod = w_ada.shape
    rows = c_all.shape[0]
    tn = 1536
    return pl.pallas_call(
        _mod_kernel,
        grid=(nl, nmod // tn),
        in_specs=[
            pl.BlockSpec((rows, D), lambda l, j: (0, 0)),
            pl.BlockSpec((1, D, tn), lambda l, j: (l, 0, j)),
            pl.BlockSpec((1, 1, tn), lambda l, j: (l, 0, j)),
        ],
        out_specs=pl.BlockSpec((1, rows, tn), lambda l, j: (l, 0, j)),
        out_shape=jax.ShapeDtypeStruct((nl, rows, nmod), F32),
        compiler_params=_cparams(2),
    )(c_all, w_ada, b_ada.reshape(nl, 1, nmod))


class _Rows:
    def __init__(self, rows, tm, groups, per_row_mod):
        self.rows, self.tm, self.groups = rows, tm, groups
        self.n_tiles = rows // tm
        self.tpg = self.n_tiles // groups
        self.per_row_mod = per_row_mod

    def mod_spec(self, which):
        tpg, tm = self.tpg, self.tm
        if self.per_row_mod:
            return pl.BlockSpec((None, None, tm, D), lambda i, *_: (which, 0, i, 0))
        return pl.BlockSpec((None, None, 1, D), lambda i, *_: (which, i // tpg, 0, 0))


def _fox_in_sample_kernel(x_ref, g_ref, sc_ref, sh_ref, wq_ref, wk_ref, wv_ref, wf_ref, bf_ref,
                          k_ref, v_ref, lf_ref, q_ref):
    h = _norm_mod(x_ref[...], g_ref[...], sc_ref[...], sh_ref[...]).astype(BF16)
    q_ref[...] = _dot(h, wq_ref[...]) * QK_SCALE
    k_ref[...] = _dot(h, wk_ref[...])
    v_ref[...] = _dot(h, wv_ref[...])
    lf = _log_sigmoid(_dot(h, wf_ref[...]) + bf_ref[...])
    lf_ref[...] = lf[:, :H]


def _fox_in_sample(x, g, mod, wq, wk, wv, wf, bf, rt):
    tm = rt.tm
    full = lambda shape: pl.BlockSpec(shape, lambda i: (0,) * len(shape))
    row = pl.BlockSpec((tm, D), lambda i: (i, 0))
    rows_f32 = jax.ShapeDtypeStruct((rt.rows, D), F32)
    return pl.pallas_call(
        _fox_in_sample_kernel,
        grid=(rt.n_tiles,),
        in_specs=[row, full((1, D)), rt.mod_spec(1), rt.mod_spec(0),
                  full((D, D)), full((D, D)), full((D, D)), full((D, LANES)), full((1, LANES))],
        out_specs=[row, row, pl.BlockSpec((tm, H), lambda i: (i, 0)), row],
        out_shape=[rows_f32, rows_f32, jax.ShapeDtypeStruct((rt.rows, H), F32), rows_f32],
        compiler_params=_cparams(1),
    )(x, g, mod, mod, wq, wk, wv, wf, bf)


def _fox_in_prompt_kernel(x_ref, g_ref, sc_ref, sh_ref, wq_ref, wkt_ref, wvt_ref, wv_ref, wf_ref, bf_ref,
                          wft_ref, bft_ref, *rest, tm, tpg, n_prev):
    kt_ref, vt_ref, lft_ref, qa_ref, kat_ref, vp_ref, carry_r, carry_c = rest[n_prev:]
    i = pl.program_id(0)

    @pl.when(i % tpg == 0)
    def _():
        carry_r[...] = jnp.zeros_like(carry_r)
        carry_c[...] = jnp.zeros_like(carry_c)

    h = _norm_mod(x_ref[...], g_ref[...], sc_ref[...], sh_ref[...]).astype(BF16)
    q = _dot(h, wq_ref[...]) * (QK_SCALE * LOG2E)
    kt = _dot_nt(wkt_ref[...], h)
    kt_ref[0, 0] = kt
    vt_ref[0, 0] = _dot_nt(wvt_ref[...], h)
    vb = _dot(h, wv_ref[...]).astype(BF16)
    for hp in range(NPAIR):
        vp_ref[0, hp] = vb[:, hp * LANES:(hp + 1) * LANES]

    lf = _log_sigmoid(_dot(h, wf_ref[...]) + bf_ref[...])
    lft = _log_sigmoid(_dot_nt(wft_ref[...], h) + bft_ref[...])
    lft_ref[0, 0] = lft
    cum = _cumsum_rows(lf, tm) + carry_r[...]
    carry_r[...] = cum[tm - 1:tm, :]
    cumt = _cumsum_lanes(lft, tm) + carry_c[...]
    carry_c[...] = cumt[:, tm - 1:tm]
    hi, mid, lo = [p.astype(F32) for p in _split3(cum * LOG2E)]
    hit, midt, lot = [p.astype(F32) for p in _split3(cumt * LOG2E)]
    lane = lax.broadcasted_iota(jnp.int32, (tm, HD), 1)
    srow = lax.broadcasted_iota(jnp.int32, (HD, tm), 0)
    for hh in range(H):
        chi, cmid, clo = hi[:, hh:hh + 1], mid[:, hh:hh + 1], lo[:, hh:hh + 1]
        q_tail = jnp.where(lane == 0, chi, jnp.where(lane == 1, cmid, jnp.where(
            lane == 2, clo, jnp.where(lane < 6, 1.0, 0.0))))
        khi, kmid, klo = hit[hh:hh + 1, :], midt[hh:hh + 1, :], lot[hh:hh + 1, :]
        k_tail = jnp.where(srow < 3, 1.0, jnp.where(srow == 3, -khi, jnp.where(
            srow == 4, -kmid, jnp.where(srow == 5, -klo, 0.0))))
        qa_ref[0, hh, :, 0:HD] = q[:, hh * HD:(hh + 1) * HD].astype(BF16)
        qa_ref[0, hh, :, HD:2 * HD] = q_tail.astype(BF16)
        kat_ref[0, hh, 0, 0:HD, :] = kt[hh * HD:(hh + 1) * HD, :].astype(BF16)
        kat_ref[0, hh, 0, HD:2 * HD, :] = k_tail.astype(BF16)


def _fox_in_prompt(x, g, mod, wq, wkt, wvt, wv, wf, bf, wft, bft, rt, seq, layer, n_layers, prev):
    tm, tpg, b = rt.tm, rt.tpg, rt.groups
    full = lambda shape: pl.BlockSpec(shape, lambda i: (0,) * len(shape))
    row = pl.BlockSpec((tm, D), lambda i: (i, 0))
    if prev is None:
        prev = (jnp.zeros((n_layers, b, D, seq), F32), jnp.zeros((n_layers, b, D, seq), F32),
                jnp.zeros((n_layers, b, H, seq), F32))
    prev = tuple(prev)
    n_in = 12
    in_specs = [row, full((1, D)), rt.mod_spec(1), rt.mod_spec(0),
                full((D, D)), full((D, D)), full((D, D)), full((D, D)), full((D, LANES)),
                full((1, LANES)), full((H, D)), full((H, 1))]
    in_specs += [pl.BlockSpec(memory_space=pl.ANY)] * len(prev)
    state_spec = lambda rows: pl.BlockSpec((1, 1, rows, tm), lambda i: (layer, i // tpg, 0, i % tpg))
    out_specs = [state_spec(D), state_spec(D), state_spec(H),
                 pl.BlockSpec((1, H, tm, LANES), lambda i: (i // tpg, 0, i % tpg, 0)),
                 pl.BlockSpec((1, H, 1, LANES, tm), lambda i: (i // tpg, 0, i % tpg, 0, 0)),
                 pl.BlockSpec((1, NPAIR, tm, LANES), lambda i: (i // tpg, 0, i % tpg, 0))]
    out_shape = [jax.ShapeDtypeStruct((n_layers, b, D, seq), F32),
                 jax.ShapeDtypeStruct((n_layers, b, D, seq), F32),
                 jax.ShapeDtypeStruct((n_layers, b, H, seq), F32),
                 jax.ShapeDtypeStruct((b, H, seq, LANES), BF16),
                 jax.ShapeDtypeStruct((b, H, seq // tm, LANES, tm), BF16),
                 jax.ShapeDtypeStruct((b, NPAIR, seq, LANES), BF16)]
    return pl.pallas_call(
        functools.partial(_fox_in_prompt_kernel, tm=tm, tpg=tpg, n_prev=len(prev)),
        grid=(rt.n_tiles,),
        in_specs=in_specs, out_specs=out_specs, out_shape=out_shape,
        scratch_shapes=[pltpu.VMEM((1, LANES), F32), pltpu.VMEM((H, 1), F32)],
        input_output_aliases={n_in + t: t for t in range(len(prev))},
        compiler_params=_cparams(1),
    )(x, g, mod, mod, wq, wkt, wvt, wv, wf, bf, wft, bft, *prev)


def _attn_kernel(qa_ref, kat_ref, vp_ref, o_ref, m0, m1, l0, l1, a0, a1, sa0, sa1, sb0, sb1, *, tq, tk):
    qi = pl.program_id(2)
    m_s, l_s, acc_s, sa_s, sb_s = (m0, m1), (l0, l1), (a0, a1), (sa0, sa1), (sb0, sb1)
    q = [qa_ref[0, 0], qa_ref[0, 1]]
    for j in range(2):
        m_s[j][...] = jnp.full_like(m_s[j], NEG_INF)
        l_s[j][...] = jnp.zeros_like(l_s[j])
        acc_s[j][...] = jnp.zeros_like(acc_s[j])
        sa_s[j][...] = _dot(q[j], kat_ref[0, j, 0])

    def consume(j, kblk, s):
        v = vp_ref[0, 0, pl.ds(pl.multiple_of(kblk * tk, tk), tk), :]
        m_old = m_s[j][...]
        m_new = jnp.maximum(m_old, jnp.max(s, axis=-1, keepdims=True))
        alpha = jnp.exp2(m_old - m_new)
        p = jnp.exp2(s - jnp.concatenate([m_new] * (tk // LANES), axis=1))
        psum = p[:, 0:LANES]
        for t in range(1, tk // LANES):
            psum = psum + p[:, t * LANES:(t + 1) * LANES]
        l_s[j][...] = alpha * l_s[j][...] + psum
        acc_s[j][...] = alpha * acc_s[j][...] + _dot(p.astype(BF16), v)
        m_s[j][...] = m_new

    def scores(j, kblk):
        return _dot(q[j], kat_ref[0, j, kblk])

    def pair(k2, carry):
        kb = 2 * k2
        for j in range(2):
            sb_s[j][...] = scores(j, kb + 1)
            consume(j, kb, sa_s[j][...])
        for j in range(2):
            sa_s[j][...] = scores(j, kb + 2)
            consume(j, kb + 1, sb_s[j][...])
        return carry

    lax.fori_loop(0, qi // 2, pair, 0)

    def diagonal(j, s):
        r = lax.broadcasted_iota(jnp.int32, (tq, tk), 0)
        c = lax.broadcasted_iota(jnp.int32, (tq, tk), 1)
        consume(j, qi, jnp.where(c <= r, s, NEG_INF))

    @pl.when(qi % 2 == 1)
    def _():
        for j in range(2):
            sb_s[j][...] = scores(j, qi)
            consume(j, qi - 1, sa_s[j][...])
        for j in range(2):
            diagonal(j, sb_s[j][...])

    @pl.when(qi % 2 == 0)
    def _():
        for j in range(2):
            diagonal(j, sa_s[j][...])

    outs = [acc_s[j][...] / jnp.sum(l_s[j][...], axis=-1, keepdims=True) for j in range(2)]
    lane = lax.broadcasted_iota(jnp.int32, (tq, LANES), 1)
    o_ref[0, 0] = jnp.where(lane < HD, outs[0], outs[1]).astype(BF16)


def _attn_prompt(qa, kat, vp, tq):
    b, _, seq, _ = qa.shape
    nkb, tk = kat.shape[2], kat.shape[4]
    return pl.pallas_call(
        functools.partial(_attn_kernel, tq=tq, tk=tk),
        grid=(b, NPAIR, seq // tq),
        in_specs=[
            pl.BlockSpec((1, 2, tq, LANES), lambda bb, hp, qi: (bb, hp, qi, 0)),
            pl.BlockSpec((1, 2, nkb, LANES, tk), lambda bb, hp, qi: (bb, hp, 0, 0, 0)),
            pl.BlockSpec((1, 1, seq, LANES), lambda bb, hp, qi: (bb, hp, 0, 0)),
        ],
        out_specs=pl.BlockSpec((1, 1, tq, LANES), lambda bb, hp, qi: (bb, hp, qi, 0)),
        out_shape=jax.ShapeDtypeStruct((b, NPAIR, seq, LANES), BF16),
        scratch_shapes=[pltpu.VMEM((tq, LANES), F32)] * 6 + [pltpu.VMEM((tq, tk), F32)] * 4,
        compiler_params=_cparams(3),
    )(qa, kat, vp)


def _attn_sample_kernel(q_ref, kn_ref, vn_ref, lfn_ref, ck_ref, cv_ref, clf_ref, o_ref,
                        m_s, l_s, acc_s, carry_s, *, sq, tk, nkc):
    kc = pl.program_id(1)
    nrow = H * sq

    @pl.when(kc == 0)
    def _():
        m_s[...] = jnp.full_like(m_s, NEG_INF)
        l_s[...] = jnp.zeros_like(l_s)
        acc_s[...] = jnp.zeros_like(acc_s)
        carry_s[...] = jnp.zeros_like(carry_s)

    rowh = lax.broadcasted_iota(jnp.int32, (nrow, D), 0) // sq
    colh = lax.broadcasted_iota(jnp.int32, (nrow, D), 1) // HD
    q = q_ref[0]
    qbd = jnp.where(rowh == colh, jnp.concatenate([q] * H, axis=0), 0.0).astype(BF16)

    def expand_heads(ct):
        n = ct.shape[1]
        return jnp.concatenate([jnp.broadcast_to(ct[hh:hh + 1, :], (sq, n)) for hh in range(H)], axis=0)

    def update(s, vb, v_transposed):
        m_old = m_s[...]
        m_new = jnp.maximum(m_old, jnp.max(s, axis=-1, keepdims=True))
        alpha = jnp.exp(m_old - m_new)
        p = jnp.exp(s - m_new)
        l_s[...] = alpha * l_s[...] + jnp.sum(p, axis=-1, keepdims=True)
        pb = p.astype(BF16)
        pv = _dot_nt(pb, vb) if v_transposed else _dot(pb, vb)
        acc_s[...] = alpha * acc_s[...] + pv
        m_s[...] = m_new

    sub = 256
    cums = []
    carry = carry_s[...]
    for t in range(tk // sub):
        cpart = _cumsum_lanes(clf_ref[0, :, t * sub:(t + 1) * sub], sub) + carry
        carry = cpart[:, sub - 1:sub]
        cums.append(cpart)
    carry_s[...] = carry
    c_keys = jnp.concatenate(cums, axis=1)
    s = _dot(qbd, ck_ref[0].astype(BF16)) - expand_heads(c_keys)
    update(s, cv_ref[0].astype(BF16), True)

    @pl.when(kc == nkc - 1)
    def _():
        c_new = _cumsum_lanes(lfn_ref[0], LANES) + carry
        pad = jnp.zeros((LANES - sq, D), F32)
        kn = jnp.concatenate([kn_ref[0], pad], axis=0).astype(BF16)
        vn = jnp.concatenate([vn_ref[0], pad], axis=0).astype(BF16)
        c_rows = expand_heads(c_new)
        rq = lax.broadcasted_iota(jnp.int32, (nrow, LANES), 0) % sq
        kidx = lax.broadcasted_iota(jnp.int32, (nrow, LANES), 1)
        cq = jnp.sum(jnp.where(kidx == rq, c_rows, 0.0), axis=-1, keepdims=True)
        s_new = _dot_nt(qbd, kn) + (cq - c_rows)
        s_new = jnp.where(kidx <= rq, s_new, NEG_INF)
        m_s[...] = m_s[...] + cq
        update(s_new, vn, False)
        r = acc_s[...] / l_s[...]
        r = jnp.where(rowh == colh, r, 0.0)
        out = r[0:sq, :]
        for hh in range(1, H):
            out = out + r[hh * sq:(hh + 1) * sq, :]
        o_ref[0] = out


def _attn_sample(q, k_new, v_new, lf_new, cache_k, cache_v, cache_lf, layer):
    nb, sq, _ = q.shape
    past = cache_k.shape[3]
    tk = 1024
    nkc = past // tk
    cur = pl.BlockSpec((1, sq, D), lambda b, kc: (b, 0, 0))
    return pl.pallas_call(
        functools.partial(_attn_sample_kernel, sq=sq, tk=tk, nkc=nkc),
        grid=(nb, nkc),
        in_specs=[cur, cur, cur,
                  pl.BlockSpec((1, H, LANES), lambda b, kc: (b, 0, 0)),
                  pl.BlockSpec((None, 1, D, tk), lambda b, kc: (layer, b, 0, kc)),
                  pl.BlockSpec((None, 1, D, tk), lambda b, kc: (layer, b, 0, kc)),
                  pl.BlockSpec((None, 1, H, tk), lambda b, kc: (layer, b, 0, kc))],
        out_specs=cur,
        out_shape=jax.ShapeDtypeStruct((nb, sq, D), F32),
        scratch_shapes=[pltpu.VMEM((H * sq, 1), F32), pltpu.VMEM((H * sq, 1), F32),
                        pltpu.VMEM((H * sq, D), F32), pltpu.VMEM((H, 1), F32)],
        compiler_params=_cparams(2),
    )(q, k_new, v_new, lf_new, cache_k, cache_v, cache_lf)


def _lru_kernel(x_ref, g_ref, sc_ref, sh_ref, wx_ref, wg_ref, cw_ref, cb_ref, wa_ref, ba_ref,
                wi_ref, bi_ref, lam_ref, h0_ref, c0_ref, y_ref, hl_ref, cn_ref, xp_s, hc_s,
                *, tm, tpg, stride, hdr):
    i = pl.program_id(0)
    nst = (CONV_W - 1) * stride

    @pl.when(i % tpg == 0)
    def _():
        xp_s[hdr - nst:hdr, :] = c0_ref[0]
        hc_s[...] = h0_ref[0]

    h = _norm_mod(x_ref[...], g_ref[...], sc_ref[...], sh_ref[...]).astype(BF16)
    xb = _dot(h, wx_ref[...])
    gb = _dot(h, wg_ref[...])
    xp_s[hdr:hdr + tm, :] = xb
    xc = cb_ref[...] + xb * cw_ref[CONV_W - 1:CONV_W, :]
    for j in range(CONV_W - 1):
        d = (CONV_W - 1 - j) * stride
        shifted = jnp.concatenate([xp_s[hdr - d:2 * hdr - d, :], pltpu.roll(xb, d, 0)[hdr:, :]], axis=0)
        xc = xc + shifted * cw_ref[j:j + 1, :]
    tail = xp_s[hdr + tm - nst:hdr + tm, :]
    cn_ref[0] = tail
    xp_s[hdr - nst:hdr, :] = tail

    xcb = xc.astype(BF16)
    r = _sigmoid(_dot(xcb, wa_ref[...]) + ba_ref[...])
    gi = _sigmoid(_dot(xcb, wi_ref[...]) + bi_ref[...])
    log_a = LRU_C * r * _log_sigmoid(lam_ref[...])
    a = jnp.exp(log_a)
    u = jnp.sqrt(-jnp.tanh(log_a) * (1.0 + a * a)) * gi * xc

    cr = min(SCAN_ROWS, tm) if stride == 1 else tm
    row = lax.broadcasted_iota(jnp.int32, (tm, 1), 0) % cr
    sft = stride
    while sft < cr:
        keep = row >= sft
        a_sh = pltpu.roll(a, sft, 0)
        u_sh = pltpu.roll(u, sft, 0)
        u = jnp.where(keep, a * u_sh + u, u)
        a = jnp.where(keep, a * a_sh, a)
        sft *= 2
    hc = hc_s[...]
    if stride == 1:
        parts = []
        for c in range(tm // cr):
            part = a[c * cr:(c + 1) * cr, :] * hc + u[c * cr:(c + 1) * cr, :]
            hc = part[cr - 1:cr, :]
            parts.append(part)
        hs = jnp.concatenate(parts, axis=0)
    else:
        hs = a * jnp.concatenate([hc] * (tm // stride), axis=0) + u
    last = hs[tm - stride:tm, :]
    hc_s[...] = last
    hl_ref[0] = last
    y_ref[...] = (hs * _gelu_tanh(gb)).astype(BF16)


def _lru_mix(x, g, mod, wx, wg, cw, cb, wa, ba, wi, bi, lam, h0, c0, rt, stride):
    tm, tpg = rt.tm, rt.tpg
    hdr = -(-(CONV_W - 1) * stride // 8) * 8
    nst = (CONV_W - 1) * stride
    full = lambda shape: pl.BlockSpec(shape, lambda i: (0,) * len(shape))
    row = pl.BlockSpec((tm, D), lambda i: (i, 0))
    grp = rt.groups
    return pl.pallas_call(
        functools.partial(_lru_kernel, tm=tm, tpg=tpg, stride=stride, hdr=hdr),
        grid=(rt.n_tiles,),
        in_specs=[row, full((1, D)), rt.mod_spec(1), rt.mod_spec(0),
                  full((D, D)), full((D, D)), full((CONV_W, D)), full((1, D)),
                  full((D, D)), full((1, D)), full((D, D)), full((1, D)), full((1, D)),
                  pl.BlockSpec((1, stride, D), lambda i: (i // tpg, 0, 0)),
                  pl.BlockSpec((1, nst, D), lambda i: (i // tpg, 0, 0))],
        out_specs=[row,
                   pl.BlockSpec((1, stride, D), lambda i: (i // tpg, 0, 0)),
                   pl.BlockSpec((1, nst, D), lambda i: (i // tpg, 0, 0))],
        out_shape=[jax.ShapeDtypeStruct((rt.rows, D), BF16),
                   jax.ShapeDtypeStruct((grp, stride, D), F32),
                   jax.ShapeDtypeStruct((grp, nst, D), F32)],
        scratch_shapes=[pltpu.VMEM((hdr + tm, D), F32), pltpu.VMEM((stride, D), F32)],
        compiler_params=_cparams(1),
    )(x, g, mod, mod, wx, wg, cw, cb, wa, ba, wi, bi, lam, h0, c0)


def _mix_in(a_ref, pair_major):
    if pair_major:
        return jnp.concatenate([a_ref[0, hp] for hp in range(NPAIR)], axis=1)
    return a_ref[...]


def _ffn_kernel(x_ref, a_ref, ga_ref, wo_ref, g2_ref, sc_ref, sh_ref, gm_ref, wg_ref, wu_ref, wd_ref,
                o_ref, x1_s, h_s, acc_s, *, nf, pair_major):
    f = pl.program_id(1)

    def partial_out(h):
        act = (_silu(_dot(h, wg_ref[...])) * _dot(h, wu_ref[...])).astype(BF16)
        return _dot(act, wd_ref[...])

    @pl.when(f == 0)
    def _():
        a = _mix_in(a_ref, pair_major).astype(BF16)
        x1 = x_ref[...] + ga_ref[...] * _dot(a, wo_ref[...])
        x1_s[...] = x1
        h = _norm_mod(x1, g2_ref[...], sc_ref[...], sh_ref[...]).astype(BF16)
        h_s[...] = h
        acc_s[...] = partial_out(h)

    @pl.when((f > 0) & (f < nf - 1))
    def _():
        acc_s[...] += partial_out(h_s[...])

    @pl.when(f == nf - 1)
    def _():
        o_ref[...] = x1_s[...] + gm_ref[...] * (acc_s[...] + partial_out(h_s[...]))


def _route_kernel(x_ref, a_ref, ga_ref, wo_ref, g2_ref, sc_ref, sh_ref, wr_ref,
                  x1_ref, h_ref, route_ref, gate_ref, cnt_ref, carry_s, *, tm):
    i = pl.program_id(0)

    @pl.when(i == 0)
    def _():
        carry_s[...] = jnp.zeros_like(carry_s)

    lane = lax.broadcasted_iota(jnp.int32, (tm, LANES), 1)
    x1 = x_ref[...] + ga_ref[...] * _dot(a_ref[...].astype(BF16), wo_ref[...])
    x1_ref[...] = x1
    hf = _norm_mod(x1, g2_ref[...], sc_ref[...], sh_ref[...])
    _to_slabs(h_ref, hf, tm)
    h_hi, h_lo, _ = _split3(hf)
    w_hi, w_lo, _ = _split3(wr_ref[...])
    logits = _dot(h_hi, w_hi) + (_dot(h_hi, w_lo) + _dot(h_lo, w_hi))
    logits = jnp.where(lane < NEXP, logits, NEG_INF)
    m1 = jnp.max(logits, axis=-1, keepdims=True)
    i1 = jnp.min(jnp.where(logits == m1, lane, LANES), axis=-1, keepdims=True)
    rest = jnp.where(lane == i1, NEG_INF, logits)
    m2 = jnp.max(rest, axis=-1, keepdims=True)
    i2 = jnp.min(jnp.where(rest == m2, lane, LANES), axis=-1, keepdims=True)
    t = jnp.exp(m2 - m1)
    gate_ref[...] = jnp.where(lane == 0, 1.0 / (1.0 + t), jnp.where(lane == 1, t / (1.0 + t), 0.0))
    sel = (lane == i1) | (lane == i2)
    r = lax.broadcasted_iota(jnp.int32, (tm, tm), 0)
    c = lax.broadcasted_iota(jnp.int32, (tm, tm), 1)
    before = _dot((c < r).astype(BF16), sel.astype(BF16)) + carry_s[...]
    rank1 = jnp.sum(jnp.where(lane == i1, before, 0.0), axis=-1, keepdims=True).astype(jnp.int32)
    rank2 = jnp.sum(jnp.where(lane == i2, before, 0.0), axis=-1, keepdims=True).astype(jnp.int32)
    route_ref[...] = jnp.where(lane == 0, i1, jnp.where(lane == 1, i2, jnp.where(
        lane == 2, rank1, jnp.where(lane == 3, rank2, 0))))
    total = carry_s[...] + jnp.sum(sel.astype(F32), axis=0, keepdims=True)
    carry_s[...] = total
    cnt_ref[...] = total.astype(jnp.int32)


def _to_slabs(ref, x, n):
    for c in range(SLAB):
        ref[pl.ds(c, n, stride=SLAB), :] = x[:, c * LANES:(c + 1) * LANES]


def _from_slabs(ref, n):
    return jnp.concatenate([ref[pl.ds(c, n, stride=SLAB), :] for c in range(SLAB)], axis=1)


def _slab(ref, t):
    return ref.at[pl.ds(pl.multiple_of(SLAB * t, SLAB), SLAB), :]


def _dispatch_kernel(dest_ref, h_ref, zero_ref, o_ref, sem, *, tm):
    del zero_ref
    base = pl.program_id(0) * tm

    def issue(r, carry):
        for s in range(2):
            d = dest_ref[2 * (base + r) + s]
            pltpu.make_async_copy(_slab(h_ref, r), _slab(o_ref, d), sem.at[s]).start()
        return carry

    lax.fori_loop(0, tm, issue, 0, unroll=4)
    for s in range(2):
        pltpu.make_async_copy(h_ref, o_ref.at[pl.ds(0, tm * SLAB), :], sem.at[s]).wait()


def _experts_kernel(te_ref, nu_ref, hs_ref, wg_ref, wu_ref, wd_ref, o_ref, acc_s, *, nf, tg):
    del te_ref
    i = pl.program_id(0)
    f = pl.program_id(1)

    def partial_out():
        h = _from_slabs(hs_ref, tg).astype(BF16)
        act = (_silu(_dot(h, wg_ref[0, 0])) * _dot(h, wu_ref[0, 0])).astype(BF16)
        return _dot(act, wd_ref[0, 0])

    used = i < nu_ref[0]

    @pl.when(used & (f == 0))
    def _():
        acc_s[...] = partial_out()

    @pl.when(used & (f > 0) & (f < nf - 1))
    def _():
        acc_s[...] += partial_out()

    @pl.when(used & (f == nf - 1))
    def _():
        _to_slabs(o_ref, acc_s[...] + partial_out(), tg)

    @pl.when((i >= nu_ref[0]) & (f == 0))
    def _():
        o_ref[...] = jnp.zeros_like(o_ref)


def _combine_kernel(dest_ref, x1_ref, gm_ref, gate_ref, gf_ref, y_ref, o_ref, buf_s, sem,
                    *, tm, n_tiles, final_norm):
    i = pl.program_id(0)

    def start_tile(t, st):
        def issue(r, carry):
            for s in range(2):
                d = dest_ref[2 * (t * tm + r) + s]
                pltpu.make_async_copy(_slab(y_ref, d), _slab(buf_s.at[st, s], r), sem.at[st, s]).start()
            return carry

        lax.fori_loop(0, tm, issue, 0, unroll=4)

    @pl.when(i == 0)
    def _():
        start_tile(0, 0)

    @pl.when(i + 1 < n_tiles)
    def _():
        start_tile(i + 1, (i + 1) % 2)

    st = i % 2
    for s in range(2):
        pltpu.make_async_copy(y_ref.at[pl.ds(0, tm * SLAB), :], buf_s.at[st, s], sem.at[st, s]).wait()
    gate = gate_ref[...]
    y = gate[:, 0:1] * _from_slabs(buf_s.at[st, 0], tm) + gate[:, 1:2] * _from_slabs(buf_s.at[st, 1], tm)
    x2 = x1_ref[...] + gm_ref[...] * y
    if final_norm:
        ms = jnp.mean(x2 * x2, axis=-1, keepdims=True)
        x2 = x2 * lax.rsqrt(ms + EPS) * gf_ref[...]
    o_ref[...] = x2


def _post_ffn(x, a, mod, wo, g2, w_gu, w_down, layer, rt, pair_major):
    tm, tpg = rt.tm, rt.tpg
    nf = DFF // TF
    assert nf >= 2
    row = pl.BlockSpec((tm, D), lambda i, f: (i, 0))
    if pair_major:
        a_spec = pl.BlockSpec((1, NPAIR, tm, LANES), lambda i, f: (i // tpg, 0, i % tpg, 0))
    else:
        a_spec = row
    c2 = lambda shape: pl.BlockSpec(shape, lambda i, f: (0,) * len(shape))
    return pl.pallas_call(
        functools.partial(_ffn_kernel, nf=nf, pair_major=pair_major),
        grid=(rt.n_tiles, nf),
        in_specs=[row, a_spec, rt.mod_spec(2), c2((D, D)), c2((1, D)),
                  rt.mod_spec(4), rt.mod_spec(3), rt.mod_spec(5),
                  pl.BlockSpec((None, D, TF), lambda i, f: (layer, 0, f)),
                  pl.BlockSpec((None, D, TF), lambda i, f: (layer, 0, nf + f)),
                  pl.BlockSpec((None, TF, D), lambda i, f: (layer, f, 0))],
        out_specs=row,
        out_shape=jax.ShapeDtypeStruct((rt.rows, D), F32),
        scratch_shapes=[pltpu.VMEM((tm, D), F32), pltpu.VMEM((tm, D), BF16), pltpu.VMEM((tm, D), F32)],
        compiler_params=_cparams(2),
    )(x, a, mod, wo, g2, mod, mod, mod, w_gu, w_gu, w_down)


def _post_moe(x, a, mod, wo, g2, w_router, w_gu, w_down, layer, rt, final_g):
    tm, rows, n_tiles = rt.tm, rt.rows, rt.n_tiles
    nf = DFF // TF_MOE
    assert nf >= 2
    row = pl.BlockSpec((tm, D), lambda i, *_: (i, 0))
    lanes = pl.BlockSpec((tm, LANES), lambda i, *_: (i, 0))
    full = lambda shape: pl.BlockSpec(shape, lambda i, *_: (0,) * len(shape))
    rows_f32 = jax.ShapeDtypeStruct((rows, D), F32)
    slabs = lambda n: pl.BlockSpec((n * SLAB, LANES), lambda i, *_: (i, 0))
    tg = min(TG, rows // 2)
    x1, hf, route, gate, cnt = pl.pallas_call(
        functools.partial(_route_kernel, tm=tm),
        grid=(n_tiles,),
        in_specs=[row, row, rt.mod_spec(2), full((D, D)), full((1, D)), rt.mod_spec(4), rt.mod_spec(3),
                  full((D, LANES))],
        out_specs=[row, slabs(tm), lanes, lanes, full((1, LANES))],
        out_shape=[rows_f32, jax.ShapeDtypeStruct((rows * SLAB, LANES), F32),
                   jax.ShapeDtypeStruct((rows, LANES), jnp.int32),
                   jax.ShapeDtypeStruct((rows, LANES), F32), jax.ShapeDtypeStruct((1, LANES), jnp.int32)],
        scratch_shapes=[pltpu.VMEM((1, LANES), F32)],
        compiler_params=_cparams(1),
    )(x, a, mod, wo, g2, mod, mod, w_router)

    n_tiles_g = -(-(2 * rows + NEXP * (tg - 1)) // tg)
    tiles_e = (cnt[0, :NEXP] + (tg - 1)) // tg
    tile_end = jnp.cumsum(tiles_e)
    start = (tile_end - tiles_e) * tg
    is_e = route[:, 0:2, None] == jnp.arange(NEXP, dtype=jnp.int32)
    dest = (route[:, 2:4] + jnp.sum(jnp.where(is_e, start, 0), axis=-1)).reshape(-1).astype(jnp.int32)
    tile_ids = jnp.arange(n_tiles_g, dtype=jnp.int32)
    tile_expert = jnp.minimum(jnp.sum(tile_ids[:, None] >= tile_end[None, :], axis=1), NEXP - 1)
    tile_expert = tile_expert.astype(jnp.int32)
    n_used = tile_end[NEXP - 1:].astype(jnp.int32)

    sorted_h = pl.pallas_call(
        functools.partial(_dispatch_kernel, tm=tm),
        grid_spec=pltpu.PrefetchScalarGridSpec(
            num_scalar_prefetch=1, grid=(n_tiles,),
            in_specs=[slabs(tm), pl.BlockSpec(memory_space=pl.ANY)],
            out_specs=pl.BlockSpec(memory_space=pl.ANY),
            scratch_shapes=[pltpu.SemaphoreType.DMA((2,))]),
        out_shape=jax.ShapeDtypeStruct((n_tiles_g * tg * SLAB, LANES), F32),
        input_output_aliases={2: 0},
        compiler_params=_cparams(1),
    )(dest, hf, jnp.zeros((n_tiles_g * tg * SLAB, LANES), F32))

    def tile_of(i, nu):
        return jnp.maximum(jnp.minimum(i, nu[0] - 1), 0)

    def chunk_of(i, f, nu):
        return jnp.where(i < nu[0], f, nf - 1)

    y_sorted = pl.pallas_call(
        functools.partial(_experts_kernel, nf=nf, tg=tg),
        grid_spec=pltpu.PrefetchScalarGridSpec(
            num_scalar_prefetch=2, grid=(n_tiles_g, nf),
            in_specs=[
                pl.BlockSpec((tg * SLAB, LANES), lambda i, f, te, nu: (tile_of(i, nu), 0)),
                pl.BlockSpec((1, 1, D, TF_MOE),
                             lambda i, f, te, nu: (layer, te[tile_of(i, nu)], 0, chunk_of(i, f, nu))),
                pl.BlockSpec((1, 1, D, TF_MOE),
                             lambda i, f, te, nu: (layer, te[tile_of(i, nu)], 0, nf + chunk_of(i, f, nu))),
                pl.BlockSpec((1, 1, TF_MOE, D),
                             lambda i, f, te, nu: (layer, te[tile_of(i, nu)], chunk_of(i, f, nu), 0))],
            out_specs=pl.BlockSpec((tg * SLAB, LANES), lambda i, f, te, nu: (i, 0)),
            scratch_shapes=[pltpu.VMEM((tg, D), F32)]),
        out_shape=jax.ShapeDtypeStruct((n_tiles_g * tg * SLAB, LANES), F32),
        compiler_params=_cparams(2),
    )(tile_expert, n_used, sorted_h, w_gu, w_gu, w_down)

    gf = jnp.ones((1, D), F32) if final_g is None else final_g
    return pl.pallas_call(
        functools.partial(_combine_kernel, tm=tm, n_tiles=n_tiles, final_norm=final_g is not None),
        grid_spec=pltpu.PrefetchScalarGridSpec(
            num_scalar_prefetch=1, grid=(n_tiles,),
            in_specs=[row, rt.mod_spec(5), lanes, full((1, D)), pl.BlockSpec(memory_space=pl.ANY)],
            out_specs=row,
            scratch_shapes=[pltpu.VMEM((2, 2, tm * SLAB, LANES), F32),
                            pltpu.SemaphoreType.DMA((2, 2))]),
        out_shape=rows_f32,
        compiler_params=_cparams(1),
    )(dest, x1, mod, gate, gf, y_sorted)


def _final_norm_kernel(x_ref, g_ref, o_ref):
    x = x_ref[...]
    ms = jnp.mean(x * x, axis=-1, keepdims=True)
    o_ref[...] = x * lax.rsqrt(ms + EPS) * g_ref[...]


def _final_norm(x, g, tm):
    rows = x.shape[0]
    return pl.pallas_call(
        _final_norm_kernel,
        grid=(rows // tm,),
        in_specs=[pl.BlockSpec((tm, D), lambda i: (i, 0)), pl.BlockSpec((1, D), lambda i: (0, 0))],
        out_specs=pl.BlockSpec((tm, D), lambda i: (i, 0)),
        out_shape=jax.ShapeDtypeStruct((rows, D), F32),
        compiler_params=_cparams(1),
    )(x, g)


def _prep_weights(p):
    w = {}
    fw = p['fox_w_in']
    w['wq'] = fw[:, :, 0 * D:1 * D].astype(BF16)
    w['wk'] = fw[:, :, 1 * D:2 * D].astype(BF16)
    w['wv'] = fw[:, :, 2 * D:3 * D].astype(BF16)
    w['wf'] = jnp.pad(fw[:, :, 3 * D:], ((0, 0), (0, 0), (0, LANES - H))).astype(BF16)
    w['bf'] = jnp.pad(p['fox_b_f'], ((0, 0), (0, LANES - H)))[:, None, :]
    w['wkt'] = w['wk'].transpose(0, 2, 1)
    w['wvt'] = w['wv'].transpose(0, 2, 1)
    w['wft'] = fw[:, :, 3 * D:].transpose(0, 2, 1).astype(BF16)
    w['bft'] = p['fox_b_f'][:, :, None]
    w['fox_wo'] = p['fox_w_out'].astype(BF16)
    w['lru_wx'] = p['lru_w_in'][:, :, :D].astype(BF16)
    w['lru_wg'] = p['lru_w_in'][:, :, D:].astype(BF16)
    nblk = p['lru_w_a'].shape[1]
    eye = jnp.eye(nblk, dtype=F32)
    bd = lambda m: jnp.einsum('lnkj,nm->lnkmj', m, eye).reshape(m.shape[0], D, D).astype(BF16)
    w['lru_wa'] = bd(p['lru_w_a'])
    w['lru_wi'] = bd(p['lru_w_i'])
    w['lru_wo'] = p['lru_w_out'].astype(BF16)
    w['ffn_gu'] = p['ffn_w_gu'].astype(BF16)
    w['ffn_down'] = p['ffn_w_down'].astype(BF16)
    w['router'] = jnp.pad(p['moe_router'], ((0, 0), (0, 0), (0, LANES - NEXP)))
    w['moe_gu'] = p['moe_w_gu'].astype(BF16)
    w['moe_down'] = p['moe_w_down'].astype(BF16)
    return w


def _trunk(x, mods, p, w, rts, prompt, seq, fox_past, lru_past):
    rt, rt_ffn, rt_moe = rts
    depth = p['norm_mix_g'].shape[0]
    n_fox = w['wq'].shape[0]
    final_g = p['final_g'][None, :]
    fox_new, lru_new = [], []
    for l in range(depth):
        j = l // 2
        last = l == depth - 1
        mod = mods[l]
        g1 = p['norm_mix_g'][l][None, :]
        g2 = p['norm_ffn_g'][l][None, :]
        if l % 2 == 0:
            if prompt:
                kt, vt, lft, qa, kat, vp = _fox_in_prompt(
                    x, g1, mod, w['wq'][j], w['wkt'][j], w['wvt'][j], w['wv'][j], w['wf'][j], w['bf'][j],
                    w['wft'][j], w['bft'][j], rt, seq, j, n_fox, fox_new[0] if fox_new else None)
                a = _attn_prompt(qa, kat, vp, tq=kat.shape[4])
                fox_new = [(kt, vt, lft)]
            else:
                k, v, lf, q = _fox_in_sample(x, g1, mod, w['wq'][j], w['wk'][j], w['wv'][j], w['wf'][j],
                                             w['bf'][j], rt)
                nb = rt.rows // seq
                to_b = lambda t: t.reshape(seq, nb, -1).transpose(1, 0, 2)
                kb, vb, lfb = to_b(k), to_b(v), to_b(lf)
                ck, cv, clf = fox_past
                lfn_t = jnp.pad(lfb.transpose(0, 2, 1), ((0, 0), (0, 0), (0, LANES - seq)))
                ab = _attn_sample(to_b(q), kb, vb, lfn_t, ck, cv, clf, j)
                a = ab.transpose(1, 0, 2).reshape(rt.rows, D)
                fox_new.append((kb, vb, lfb))
            x = _post_ffn(x, a, mod, w['fox_wo'][j], g2, w['ffn_gu'], w['ffn_down'], j, rt_ffn,
                          pair_major=prompt)
            if last:
                x = _final_norm(x, final_g, rt.tm)
        else:
            h0, c0, stride = lru_past[j]
            y, hl, cn = _lru_mix(x, g1, mod, w['lru_wx'][j], w['lru_wg'][j], p['lru_conv_w'][j],
                                 p['lru_conv_b'][j][None, :], w['lru_wa'][j], p['lru_b_a'][j][None, :],
                                 w['lru_wi'][j], p['lru_b_i'][j][None, :], p['lru_lam'][j][None, :],
                                 h0, c0, rt, stride)
            lru_new.append((hl, cn))
            x = _post_moe(x, y, mod, w['lru_wo'][j], g2, w['router'][j], w['moe_gu'], w['moe_down'], j,
                          rt_moe, final_g if last else None)
    return x, fox_new, lru_new


def kernel(x_prompt, x_sample, c_prompt, c_sample, cache_k, cache_v, cache_logf, state_h, state_conv,
           norm_mix_g, norm_ffn_g, final_g, w_ada, b_ada, fox_w_in, fox_b_f, fox_w_out,
           lru_w_in, lru_conv_w, lru_conv_b, lru_w_a, lru_b_a, lru_w_i, lru_b_i, lru_lam, lru_w_out,
           ffn_w_gu, ffn_w_down, moe_router, moe_w_gu, moe_w_down):
    p = {
        'norm_mix_g': norm_mix_g, 'norm_ffn_g': norm_ffn_g, 'final_g': final_g,
        'fox_w_in': fox_w_in, 'fox_b_f': fox_b_f, 'fox_w_out': fox_w_out,
        'lru_w_in': lru_w_in, 'lru_conv_w': lru_conv_w, 'lru_conv_b': lru_conv_b,
        'lru_w_a': lru_w_a, 'lru_b_a': lru_b_a, 'lru_w_i': lru_w_i, 'lru_b_i': lru_b_i,
        'lru_lam': lru_lam, 'lru_w_out': lru_w_out,
        'ffn_w_gu': ffn_w_gu, 'ffn_w_down': ffn_w_down,
        'moe_router': moe_router, 'moe_w_gu': moe_w_gu, 'moe_w_down': moe_w_down,
    }
    bp, sp, _ = x_prompt.shape
    bs, ss, _ = x_sample.shape
    depth = w_ada.shape[0]
    n_fox, n_lru = cache_k.shape[0], state_h.shape[0]
    past = cache_k.shape[2]
    w = _prep_weights(p)

    nc = bp + bs
    ncp = -(-nc // 8) * 8
    c_all = jnp.pad(jnp.concatenate([c_prompt, c_sample], axis=0), ((0, ncp - nc), (0, 0)))
    mod_all = _ada_mod(c_all, w_ada, b_ada)
    mods_p = [mod_all[l, :bp].reshape(bp, 6, 1, D).transpose(1, 0, 2, 3) for l in range(depth)]
    mods_s = [jnp.tile(mod_all[l, bp:nc].reshape(bs, 6, D).transpose(1, 0, 2), (1, ss, 1))[:, None]
              for l in range(depth)]

    rts_p = tuple(_Rows(bp * sp, tm, bp, per_row_mod=False) for tm in (TM_MIX, TM_FFN, TM_MOE))
    zeros_h = jnp.zeros((bp, 1, D), F32)
    zeros_c = jnp.zeros((bp, CONV_W - 1, D), F32)
    y_p, fox_p, lru_p = _trunk(x_prompt.reshape(bp * sp, D), mods_p, p, w, rts_p, True, sp,
                               None, [(zeros_h, zeros_c, 1)] * n_lru)

    rt_s = _Rows(bs * ss, bs * ss, 1, per_row_mod=True)
    xs = x_sample.transpose(1, 0, 2).reshape(ss * bs, D)
    ckt = cache_k.transpose(0, 1, 3, 4, 2).reshape(n_fox, bs, D, past)
    cvt = cache_v.transpose(0, 1, 3, 4, 2).reshape(n_fox, bs, D, past)
    clft = cache_logf.transpose(0, 1, 3, 2)
    lru_past = [(state_h[j][None], state_conv[j].transpose(1, 0, 2).reshape(1, (CONV_W - 1) * bs, D), bs)
                for j in range(n_lru)]
    y_s, fox_s, lru_s = _trunk(xs, mods_s, p, w, (rt_s, rt_s, rt_s), False, ss, (ckt, cvt, clft), lru_past)

    y_prompt = y_p.reshape(bp, sp, D)
    y_sample = y_s.reshape(ss, bs, D).transpose(1, 0, 2)
    kt_all, vt_all, lft_all = fox_p[0]
    k_prompt = kt_all.reshape(n_fox, bp, H, HD, sp).transpose(0, 1, 4, 2, 3)
    v_prompt = vt_all.reshape(n_fox, bp, H, HD, sp).transpose(0, 1, 4, 2, 3)
    logf_prompt = lft_all.transpose(0, 1, 3, 2)
    h_prompt = jnp.stack([s[0][:, 0, :] for s in lru_p])
    conv_prompt = jnp.stack([s[1] for s in lru_p])
    k_sample = jnp.stack([f[0] for f in fox_s]).reshape(n_fox, bs, ss, H, HD)
    v_sample = jnp.stack([f[1] for f in fox_s]).reshape(n_fox, bs, ss, H, HD)
    logf_sample = jnp.stack([f[2] for f in fox_s])
    h_sample = jnp.stack([s[0][0] for s in lru_s])
    conv_sample = jnp.stack([s[1][0].reshape(CONV_W - 1, bs, D).transpose(1, 0, 2) for s in lru_s])
    return (y_prompt, y_sample, k_prompt, v_prompt, logf_prompt, h_prompt, conv_prompt,
            k_sample, v_sample, logf_sample, h_sample, conv_sample)
```

```python
import functools
import math

import jax
import jax.numpy as jnp
from jax import lax
from jax.experimental import pallas as pl
from jax.experimental.pallas import tpu as pltpu

F32 = jnp.float32
BF16 = jnp.bfloat16

D = 1024
H = 16
HD = 64
NPAIR = H // 2
LANES = 128
DFF = 3584
NEXP = 8
CONV_W = 4
LRU_C = 8.0
EPS = 1e-6
QK_SCALE = HD ** -0.5
LOG2E = math.log2(math.e)
NEG_INF = float("-inf")
VMEM_LIMIT = 56 * 1024 * 1024
TM_MIX = 512
TM_FFN = 512
TM_MOE = 512
TF = 1792
TF_MOE = 1792
TG = 512
SLAB = D // LANES
SCAN_ROWS = 8


def _cparams(n_axes):
    return pltpu.CompilerParams(dimension_semantics=("arbitrary",) * n_axes,
                                vmem_limit_bytes=VMEM_LIMIT)


def _dot(a, b):
    return jnp.dot(a, b, preferred_element_type=F32)


def _dot_nt(a, b):
    return lax.dot_general(a, b, (((1,), (1,)), ((), ())), preferred_element_type=F32)


def _norm_mod(x, g, sc, sh):
    ms = jnp.mean(x * x, axis=-1, keepdims=True)
    return (x * lax.rsqrt(ms + EPS) * g) * (1.0 + sc) + sh


def _log_sigmoid(x):
    return jnp.minimum(x, 0.0) - jnp.log1p(jnp.exp(-jnp.abs(x)))


def _sigmoid(x):
    return 0.5 * (1.0 + jnp.tanh(0.5 * x))


def _silu(x):
    return x * _sigmoid(x)


def _gelu_tanh(x):
    return 0.5 * x * (1.0 + jnp.tanh(math.sqrt(2.0 / math.pi) * (x + 0.044715 * (x * x * x))))


def _split3(x):
    hi = x.astype(BF16)
    r = x - hi.astype(F32)
    mid = r.astype(BF16)
    lo = (r - mid.astype(F32)).astype(BF16)
    return hi, mid, lo


def _cumsum_rows(x, n):
    r = lax.broadcasted_iota(jnp.int32, (n, n), 0)
    c = lax.broadcasted_iota(jnp.int32, (n, n), 1)
    tri = (c <= r).astype(BF16)
    hi, mid, lo = _split3(x)
    return _dot(tri, hi) + _dot(tri, mid) + _dot(tri, lo)


def _cumsum_lanes(x, n):
    r = lax.broadcasted_iota(jnp.int32, (n, n), 0)
    c = lax.broadcasted_iota(jnp.int32, (n, n), 1)
    tri = (r <= c).astype(BF16)
    hi, mid, lo = _split3(x)
    return _dot(hi, tri) + _dot(mid, tri) + _dot(lo, tri)


def _mod_kernel(c_ref, w_ref, b_ref, o_ref):
    c = c_ref[...]
    s = _silu(c).astype(BF16)
    o_ref[0] = _dot(s, w_ref[0].astype(BF16)) + b_ref[0]


def _ada_mod(c_all, w_ada, b_ada):
    nl, _, nmod = w_ada.shape
    rows = c_all.shape[0]
    tn = 1536
    return pl.pallas_call(
        _mod_kernel,
        grid=(nl, nmod // tn),
        in_specs=[
            pl.BlockSpec((rows, D), lambda l, j: (0, 0)),
            pl.BlockSpec((1, D, tn), lambda l, j: (l, 0, j)),
            pl.BlockSpec((1, 1, tn), lambda l, j: (l, 0, j)),
        ],
        out_specs=pl.BlockSpec((1, rows, tn), lambda l, j: (l, 0, j)),
        out_shape=jax.ShapeDtypeStruct((nl, rows, nmod), F32),
        compiler_params=_cparams(2),
    )(c_all, w_ada, b_ada.reshape(nl, 1, nmod))


class _Rows:
    def __init__(self, rows, tm, groups, per_row_mod):
        self.rows, self.tm, self.groups = rows, tm, groups
        self.n_tiles = rows // tm
        self.tpg = self.n_tiles // groups
        self.per_row_mod = per_row_mod

    def mod_spec(self, which):
        tpg, tm = self.tpg, self.tm
        if self.per_row_mod:
            return pl.BlockSpec((None, None, tm, D), lambda i, *_: (which, 0, i, 0))
        return pl.BlockSpec((None, None, 1, D), lambda i, *_: (which, i // tpg, 0, 0))


def _fox_in_sample_kernel(x_ref, g_ref, sc_ref, sh_ref, wq_ref, wk_ref, wv_ref, wf_ref, bf_ref,
                          k_ref, v_ref, lf_ref, q_ref):
    h = _norm_mod(x_ref[...], g_ref[...], sc_ref[...], sh_ref[...]).astype(BF16)
    q_ref[...] = _dot(h, wq_ref[...]) * QK_SCALE
    k_ref[...] = _dot(h, wk_ref[...])
    v_ref[...] = _dot(h, wv_ref[...])
    lf = _log_sigmoid(_dot(h, wf_ref[...]) + bf_ref[...])
    lf_ref[...] = lf[:, :H]


def _fox_in_sample(x, g, mod, wq, wk, wv, wf, bf, rt):
    tm = rt.tm
    full = lambda shape: pl.BlockSpec(shape, lambda i: (0,) * len(shape))
    row = pl.BlockSpec((tm, D), lambda i: (i, 0))
    rows_f32 = jax.ShapeDtypeStruct((rt.rows, D), F32)
    return pl.pallas_call(
        _fox_in_sample_kernel,
        grid=(rt.n_tiles,),
        in_specs=[row, full((1, D)), rt.mod_spec(1), rt.mod_spec(0),
                  full((D, D)), full((D, D)), full((D, D)), full((D, LANES)), full((1, LANES))],
        out_specs=[row, row, pl.BlockSpec((tm, H), lambda i: (i, 0)), row],
        out_shape=[rows_f32, rows_f32, jax.ShapeDtypeStruct((rt.rows, H), F32), rows_f32],
        compiler_params=_cparams(1),
    )(x, g, mod, mod, wq, wk, wv, wf, bf)


def _fox_in_prompt_kernel(x_ref, g_ref, sc_ref, sh_ref, wq_ref, wkt_ref, wvt_ref, wv_ref, wf_ref, bf_ref,
                          wft_ref, bft_ref, *rest, tm, tpg, n_prev):
    kt_ref, vt_ref, lft_ref, qa_ref, kat_ref, vp_ref, carry_r, carry_c = rest[n_prev:]
    i = pl.program_id(0)

    @pl.when(i % tpg == 0)
    def _():
        carry_r[...] = jnp.zeros_like(carry_r)
        carry_c[...] = jnp.zeros_like(carry_c)

    h = _norm_mod(x_ref[...], g_ref[...], sc_ref[...], sh_ref[...]).astype(BF16)
    q = _dot(h, wq_ref[...]) * (QK_SCALE * LOG2E)
    kt = _dot_nt(wkt_ref[...], h)
    kt_ref[0, 0] = kt
    vt_ref[0, 0] = _dot_nt(wvt_ref[...], h)
    vb = _dot(h, wv_ref[...]).astype(BF16)
    for hp in range(NPAIR):
        vp_ref[0, hp] = vb[:, hp * LANES:(hp + 1) * LANES]

    lf = _log_sigmoid(_dot(h, wf_ref[...]) + bf_ref[...])
    lft = _log_sigmoid(_dot_nt(wft_ref[...], h) + bft_ref[...])
    lft_ref[0, 0] = lft
    cum = _cumsum_rows(lf, tm) + carry_r[...]
    carry_r[...] = cum[tm - 1:tm, :]
    cumt = _cumsum_lanes(lft, tm) + carry_c[...]
    carry_c[...] = cumt[:, tm - 1:tm]
    hi, mid, lo = [p.astype(F32) for p in _split3(cum * LOG2E)]
    hit, midt, lot = [p.astype(F32) for p in _split3(cumt * LOG2E)]
    lane = lax.broadcasted_iota(jnp.int32, (tm, HD), 1)
    srow = lax.broadcasted_iota(jnp.int32, (HD, tm), 0)
    for hh in range(H):
        chi, cmid, clo = hi[:, hh:hh + 1], mid[:, hh:hh + 1], lo[:, hh:hh + 1]
        q_tail = jnp.where(lane == 0, chi, jnp.where(lane == 1, cmid, jnp.where(
            lane == 2, clo, jnp.where(lane < 6, 1.0, 0.0))))
        khi, kmid, klo = hit[hh:hh + 1, :], midt[hh:hh + 1, :], lot[hh:hh + 1, :]
        k_tail = jnp.where(srow < 3, 1.0, jnp.where(srow == 3, -khi, jnp.where(
            srow == 4, -kmid, jnp.where(srow == 5, -klo, 0.0))))
        qa_ref[0, hh, :, 0:HD] = q[:, hh * HD:(hh + 1) * HD].astype(BF16)
        qa_ref[0, hh, :, HD:2 * HD] = q_tail.astype(BF16)
        kat_ref[0, hh, 0, 0:HD, :] = kt[hh * HD:(hh + 1) * HD, :].astype(BF16)
        kat_ref[0, hh, 0, HD:2 * HD, :] = k_tail.astype(BF16)


def _fox_in_prompt(x, g, mod, wq, wkt, wvt, wv, wf, bf, wft, bft, rt, seq, layer, n_layers, prev):
    tm, tpg, b = rt.tm, rt.tpg, rt.groups
    full = lambda shape: pl.BlockSpec(shape, lambda i: (0,) * len(shape))
    row = pl.BlockSpec((tm, D), lambda i: (i, 0))
    if prev is None:
        prev = (jnp.zeros((n_layers, b, D, seq), F32), jnp.zeros((n_layers, b, D, seq), F32),
                jnp.zeros((n_layers, b, H, seq), F32))
    prev = tuple(prev)
    n_in = 12
    in_specs = [row, full((1, D)), rt.mod_spec(1), rt.mod_spec(0),
                full((D, D)), full((D, D)), full((D, D)), full((D, D)), full((D, LANES)),
                full((1, LANES)), full((H, D)), full((H, 1))]
    in_specs += [pl.BlockSpec(memory_space=pl.ANY)] * len(prev)
    state_spec = lambda rows: pl.BlockSpec((1, 1, rows, tm), lambda i: (layer, i // tpg, 0, i % tpg))
    out_specs = [state_spec(D), state_spec(D), state_spec(H),
                 pl.BlockSpec((1, H, tm, LANES), lambda i: (i // tpg, 0, i % tpg, 0)),
                 pl.BlockSpec((1, H, 1, LANES, tm), lambda i: (i // tpg, 0, i % tpg, 0, 0)),
                 pl.BlockSpec((1, NPAIR, tm, LANES), lambda i: (i // tpg, 0, i % tpg, 0))]
    out_shape = [jax.ShapeDtypeStruct((n_layers, b, D, seq), F32),
                 jax.ShapeDtypeStruct((n_layers, b, D, seq), F32),
                 jax.ShapeDtypeStruct((n_layers, b, H, seq), F32),
                 jax.ShapeDtypeStruct((b, H, seq, LANES), BF16),
                 jax.ShapeDtypeStruct((b, H, seq // tm, LANES, tm), BF16),
                 jax.ShapeDtypeStruct((b, NPAIR, seq, LANES), BF16)]
    return pl.pallas_call(
        functools.partial(_fox_in_prompt_kernel, tm=tm, tpg=tpg, n_prev=len(prev)),
        grid=(rt.n_tiles,),
        in_specs=in_specs, out_specs=out_specs, out_shape=out_shape,
        scratch_shapes=[pltpu.VMEM((1, LANES), F32), pltpu.VMEM((H, 1), F32)],
        input_output_aliases={n_in + t: t for t in range(len(prev))},
        compiler_params=_cparams(1),
    )(x, g, mod, mod, wq, wkt, wvt, wv, wf, bf, wft, bft, *prev)


def _attn_kernel(qa_ref, kat_ref, vp_ref, o_ref, m0, m1, l0, l1, a0, a1, sa0, sa1, sb0, sb1, *, tq, tk):
    qi = pl.program_id(2)
    m_s, l_s, acc_s, sa_s, sb_s = (m0, m1), (l0, l1), (a0, a1), (sa0, sa1), (sb0, sb1)
    q = [qa_ref[0, 0], qa_ref[0, 1]]
    for j in range(2):
        m_s[j][...] = jnp.full_like(m_s[j], NEG_INF)
        l_s[j][...] = jnp.zeros_like(l_s[j])
        acc_s[j][...] = jnp.zeros_like(acc_s[j])
        sa_s[j][...] = _dot(q[j], kat_ref[0, j, 0])

    def consume(j, kblk, s):
        v = vp_ref[0, 0, pl.ds(pl.multiple_of(kblk * tk, tk), tk), :]
        m_old = m_s[j][...]
        m_new = jnp.maximum(m_old, jnp.max(s, axis=-1, keepdims=True))
        alpha = jnp.exp2(m_old - m_new)
        p = jnp.exp2(s - jnp.concatenate([m_new] * (tk // LANES), axis=1))
        psum = p[:, 0:LANES]
        for t in range(1, tk // LANES):
            psum = psum + p[:, t * LANES:(t + 1) * LANES]
        l_s[j][...] = alpha * l_s[j][...] + psum
        acc_s[j][...] = alpha * acc_s[j][...] + _dot(p.astype(BF16), v)
        m_s[j][...] = m_new

    def scores(j, kblk):
        return _dot(q[j], kat_ref[0, j, kblk])

    def pair(k2, carry):
        kb = 2 * k2
        for j in range(2):
            sb_s[j][...] = scores(j, kb + 1)
            consume(j, kb, sa_s[j][...])
        for j in range(2):
            sa_s[j][...] = scores(j, kb + 2)
            consume(j, kb + 1, sb_s[j][...])
        return carry

    lax.fori_loop(0, qi // 2, pair, 0)

    def diagonal(j, s):
        r = lax.broadcasted_iota(jnp.int32, (tq, tk), 0)
        c = lax.broadcasted_iota(jnp.int32, (tq, tk), 1)
        consume(j, qi, jnp.where(c <= r, s, NEG_INF))

    @pl.when(qi % 2 == 1)
    def _():
        for j in range(2):
            sb_s[j][...] = scores(j, qi)
            consume(j, qi - 1, sa_s[j][...])
        for j in range(2):
            diagonal(j, sb_s[j][...])

    @pl.when(qi % 2 == 0)
    def _():
        for j in range(2):
            diagonal(j, sa_s[j][...])

    outs = [acc_s[j][...] / jnp.sum(l_s[j][...], axis=-1, keepdims=True) for j in range(2)]
    lane = lax.broadcasted_iota(jnp.int32, (tq, LANES), 1)
    o_ref[0, 0] = jnp.where(lane < HD, outs[0], outs[1]).astype(BF16)


def _attn_prompt(qa, kat, vp, tq):
    b, _, seq, _ = qa.shape
    nkb, tk = kat.shape[2], kat.shape[4]
    return pl.pallas_call(
        functools.partial(_attn_kernel, tq=tq, tk=tk),
        grid=(b, NPAIR, seq // tq),
        in_specs=[
            pl.BlockSpec((1, 2, tq, LANES), lambda bb, hp, qi: (bb, hp, qi, 0)),
            pl.BlockSpec((1, 2, nkb, LANES, tk), lambda bb, hp, qi: (bb, hp, 0, 0, 0)),
            pl.BlockSpec((1, 1, seq, LANES), lambda bb, hp, qi: (bb, hp, 0, 0)),
        ],
        out_specs=pl.BlockSpec((1, 1, tq, LANES), lambda bb, hp, qi: (bb, hp, qi, 0)),
        out_shape=jax.ShapeDtypeStruct((b, NPAIR, seq, LANES), BF16),
        scratch_shapes=[pltpu.VMEM((tq, LANES), F32)] * 6 + [pltpu.VMEM((tq, tk), F32)] * 4,
        compiler_params=_cparams(3),
    )(qa, kat, vp)


def _attn_sample_kernel(q_ref, kn_ref, vn_ref, lfn_ref, ck_ref, cv_ref, clf_ref, o_ref,
                        m_s, l_s, acc_s, carry_s, *, sq, tk, nkc):
    kc = pl.program_id(1)
    nrow = H * sq

    @pl.when(kc == 0)
    def _():
        m_s[...] = jnp.full_like(m_s, NEG_INF)
        l_s[...] = jnp.zeros_like(l_s)
        acc_s[...] = jnp.zeros_like(acc_s)
        carry_s[...] = jnp.zeros_like(carry_s)

    rowh = lax.broadcasted_iota(jnp.int32, (nrow, D), 0) // sq
    colh = lax.broadcasted_iota(jnp.int32, (nrow, D), 1) // HD
    q = q_ref[0]
    qbd = jnp.where(rowh == colh, jnp.concatenate([q] * H, axis=0), 0.0).astype(BF16)

    def expand_heads(ct):
        n = ct.shape[1]
        return jnp.concatenate([jnp.broadcast_to(ct[hh:hh + 1, :], (sq, n)) for hh in range(H)], axis=0)

    def update(s, vb, v_transposed):
        m_old = m_s[...]
        m_new = jnp.maximum(m_old, jnp.max(s, axis=-1, keepdims=True))
        alpha = jnp.exp(m_old - m_new)
        p = jnp.exp(s - m_new)
        l_s[...] = alpha * l_s[...] + jnp.sum(p, axis=-1, keepdims=True)
        pb = p.astype(BF16)
        pv = _dot_nt(pb, vb) if v_transposed else _dot(pb, vb)
        acc_s[...] = alpha * acc_s[...] + pv
        m_s[...] = m_new

    sub = 256
    cums = []
    carry = carry_s[...]
    for t in range(tk // sub):
        cpart = _cumsum_lanes(clf_ref[0, :, t * sub:(t + 1) * sub], sub) + carry
        carry = cpart[:, sub - 1:sub]
        cums.append(cpart)
    carry_s[...] = carry
    c_keys = jnp.concatenate(cums, axis=1)
    s = _dot(qbd, ck_ref[0].astype(BF16)) - expand_heads(c_keys)
    update(s, cv_ref[0].astype(BF16), True)

    @pl.when(kc == nkc - 1)
    def _():
        c_new = _cumsum_lanes(lfn_ref[0], LANES) + carry
        pad = jnp.zeros((LANES - sq, D), F32)
        kn = jnp.concatenate([kn_ref[0], pad], axis=0).astype(BF16)
        vn = jnp.concatenate([vn_ref[0], pad], axis=0).astype(BF16)
        c_rows = expand_heads(c_new)
        rq = lax.broadcasted_iota(jnp.int32, (nrow, LANES), 0) % sq
        kidx = lax.broadcasted_iota(jnp.int32, (nrow, LANES), 1)
        cq = jnp.sum(jnp.where(kidx == rq, c_rows, 0.0), axis=-1, keepdims=True)
        s_new = _dot_nt(qbd, kn) + (cq - c_rows)
        s_new = jnp.where(kidx <= rq, s_new, NEG_INF)
        m_s[...] = m_s[...] + cq
        update(s_new, vn, False)
        r = acc_s[...] / l_s[...]
        r = jnp.where(rowh == colh, r, 0.0)
        out = r[0:sq, :]
        for hh in range(1, H):
            out = out + r[hh * sq:(hh + 1) * sq, :]
        o_ref[0] = out


def _attn_sample(q, k_new, v_new, lf_new, cache_k, cache_v, cache_lf, layer):
    nb, sq, _ = q.shape
    past = cache_k.shape[3]
    tk = 1024
    nkc = past // tk
    cur = pl.BlockSpec((1, sq, D), lambda b, kc: (b, 0, 0))
    return pl.pallas_call(
        functools.partial(_attn_sample_kernel, sq=sq, tk=tk, nkc=nkc),
        grid=(nb, nkc),
        in_specs=[cur, cur, cur,
                  pl.BlockSpec((1, H, LANES), lambda b, kc: (b, 0, 0)),
                  pl.BlockSpec((None, 1, D, tk), lambda b, kc: (layer, b, 0, kc)),
                  pl.BlockSpec((None, 1, D, tk), lambda b, kc: (layer, b, 0, kc)),
                  pl.BlockSpec((None, 1, H, tk), lambda b, kc: (layer, b, 0, kc))],
        out_specs=cur,
        out_shape=jax.ShapeDtypeStruct((nb, sq, D), F32),
        scratch_shapes=[pltpu.VMEM((H * sq, 1), F32), pltpu.VMEM((H * sq, 1), F32),
                        pltpu.VMEM((H * sq, D), F32), pltpu.VMEM((H, 1), F32)],
        compiler_params=_cparams(2),
    )(q, k_new, v_new, lf_new, cache_k, cache_v, cache_lf)


def _lru_kernel(x_ref, g_ref, sc_ref, sh_ref, wx_ref, wg_ref, cw_ref, cb_ref, wa_ref, ba_ref,
                wi_ref, bi_ref, lam_ref, h0_ref, c0_ref, y_ref, hl_ref, cn_ref, xp_s, hc_s,
                *, tm, tpg, stride, hdr):
    i = pl.program_id(0)
    nst = (CONV_W - 1) * stride

    @pl.when(i % tpg == 0)
    def _():
        xp_s[hdr - nst:hdr, :] = c0_ref[0]
        hc_s[...] = h0_ref[0]

    h = _norm_mod(x_ref[...], g_ref[...], sc_ref[...], sh_ref[...]).astype(BF16)
    xb = _dot(h, wx_ref[...])
    gb = _dot(h, wg_ref[...])
    xp_s[hdr:hdr + tm, :] = xb
    xc = cb_ref[...] + xb * cw_ref[CONV_W - 1:CONV_W, :]
    for j in range(CONV_W - 1):
        d = (CONV_W - 1 - j) * stride
        shifted = jnp.concatenate([xp_s[hdr - d:2 * hdr - d, :], pltpu.roll(xb, d, 0)[hdr:, :]], axis=0)
        xc = xc + shifted * cw_ref[j:j + 1, :]
    tail = xp_s[hdr + tm - nst:hdr + tm, :]
    cn_ref[0] = tail
    xp_s[hdr - nst:hdr, :] = tail

    xcb = xc.astype(BF16)
    r = _sigmoid(_dot(xcb, wa_ref[...]) + ba_ref[...])
    gi = _sigmoid(_dot(xcb, wi_ref[...]) + bi_ref[...])
    log_a = LRU_C * r * _log_sigmoid(lam_ref[...])
    a = jnp.exp(log_a)
    u = jnp.sqrt(-jnp.tanh(log_a) * (1.0 + a * a)) * gi * xc

    cr = min(SCAN_ROWS, tm) if stride == 1 else tm
    row = lax.broadcasted_iota(jnp.int32, (tm, 1), 0) % cr
    sft = stride
    while sft < cr:
        keep = row >= sft
        a_sh = pltpu.roll(a, sft, 0)
        u_sh = pltpu.roll(u, sft, 0)
        u = jnp.where(keep, a * u_sh + u, u)
        a = jnp.where(keep, a * a_sh, a)
        sft *= 2
    hc = hc_s[...]
    if stride == 1:
        parts = []
        for c in range(tm // cr):
            part = a[c * cr:(c + 1) * cr, :] * hc + u[c * cr:(c + 1) * cr, :]
            hc = part[cr - 1:cr, :]
            parts.append(part)
        hs = jnp.concatenate(parts, axis=0)
    else:
        hs = a * jnp.concatenate([hc] * (tm // stride), axis=0) + u
    last = hs[tm - stride:tm, :]
    hc_s[...] = last
    hl_ref[0] = last
    y_ref[...] = (hs * _gelu_tanh(gb)).astype(BF16)


def _lru_mix(x, g, mod, wx, wg, cw, cb, wa, ba, wi, bi, lam, h0, c0, rt, stride):
    tm, tpg = rt.tm, rt.tpg
    hdr = -(-(CONV_W - 1) * stride // 8) * 8
    nst = (CONV_W - 1) * stride
    full = lambda shape: pl.BlockSpec(shape, lambda i: (0,) * len(shape))
    row = pl.BlockSpec((tm, D), lambda i: (i, 0))
    grp = rt.groups
    return pl.pallas_call(
        functools.partial(_lru_kernel, tm=tm, tpg=tpg, stride=stride, hdr=hdr),
        grid=(rt.n_tiles,),
        in_specs=[row, full((1, D)), rt.mod_spec(1), rt.mod_spec(0),
                  full((D, D)), full((D, D)), full((CONV_W, D)), full((1, D)),
                  full((D, D)), full((1, D)), full((D, D)), full((1, D)), full((1, D)),
                  pl.BlockSpec((1, stride, D), lambda i: (i // tpg, 0, 0)),
                  pl.BlockSpec((1, nst, D), lambda i: (i // tpg, 0, 0))],
        out_specs=[row,
                   pl.BlockSpec((1, stride, D), lambda i: (i // tpg, 0, 0)),
                   pl.BlockSpec((1, nst, D), lambda i: (i // tpg, 0, 0))],
        out_shape=[jax.ShapeDtypeStruct((rt.rows, D), BF16),
                   jax.ShapeDtypeStruct((grp, stride, D), F32),
                   jax.ShapeDtypeStruct((grp, nst, D), F32)],
        scratch_shapes=[pltpu.VMEM((hdr + tm, D), F32), pltpu.VMEM((stride, D), F32)],
        compiler_params=_cparams(1),
    )(x, g, mod, mod, wx, wg, cw, cb, wa, ba, wi, bi, lam, h0, c0)


def _mix_in(a_ref, pair_major):
    if pair_major:
        return jnp.concatenate([a_ref[0, hp] for hp in range(NPAIR)], axis=1)
    return a_ref[...]


def _ffn_kernel(x_ref, a_ref, ga_ref, wo_ref, g2_ref, sc_ref, sh_ref, gm_ref, wg_ref, wu_ref, wd_ref,
                o_ref, x1_s, h_s, acc_s, *, nf, pair_major):
    f = pl.program_id(1)

    @pl.when(f == 0)
    def _():
        a = _mix_in(a_ref, pair_major).astype(BF16)
        x1 = x_ref[...] + ga_ref[...] * _dot(a, wo_ref[...])
        x1_s[...] = x1
        h_s[...] = _norm_mod(x1, g2_ref[...], sc_ref[...], sh_ref[...]).astype(BF16)
        acc_s[...] = jnp.zeros_like(acc_s)

    h = h_s[...]
    act = (_silu(_dot(h, wg_ref[...])) * _dot(h, wu_ref[...])).astype(BF16)
    acc_s[...] += _dot(act, wd_ref[...])

    @pl.when(f == nf - 1)
    def _():
        o_ref[...] = x1_s[...] + gm_ref[...] * acc_s[...]


def _route_kernel(x_ref, a_ref, ga_ref, wo_ref, g2_ref, sc_ref, sh_ref, wr_ref,
                  x1_ref, h_ref, route_ref, gate_ref, cnt_ref, carry_s, *, tm):
    i = pl.program_id(0)

    @pl.when(i == 0)
    def _():
        carry_s[...] = jnp.zeros_like(carry_s)

    lane = lax.broadcasted_iota(jnp.int32, (tm, LANES), 1)
    x1 = x_ref[...] + ga_ref[...] * _dot(a_ref[...].astype(BF16), wo_ref[...])
    x1_ref[...] = x1
    hf = _norm_mod(x1, g2_ref[...], sc_ref[...], sh_ref[...])
    _to_slabs(h_ref, hf, tm)
    h_hi, h_lo, _ = _split3(hf)
    w_hi, w_lo, _ = _split3(wr_ref[...])
    logits = _dot(h_hi, w_hi) + (_dot(h_hi, w_lo) + _dot(h_lo, w_hi))
    logits = jnp.where(lane < NEXP, logits, NEG_INF)
    m1 = jnp.max(logits, axis=-1, keepdims=True)
    i1 = jnp.min(jnp.where(logits == m1, lane, LANES), axis=-1, keepdims=True)
    rest = jnp.where(lane == i1, NEG_INF, logits)
    m2 = jnp.max(rest, axis=-1, keepdims=True)
    i2 = jnp.min(jnp.where(rest == m2, lane, LANES), axis=-1, keepdims=True)
    t = jnp.exp(m2 - m1)
    gate_ref[...] = jnp.where(lane == 0, 1.0 / (1.0 + t), jnp.where(lane == 1, t / (1.0 + t), 0.0))
    sel = (lane == i1) | (lane == i2)
    r = lax.broadcasted_iota(jnp.int32, (tm, tm), 0)
    c = lax.broadcasted_iota(jnp.int32, (tm, tm), 1)
    before = _dot((c < r).astype(BF16), sel.astype(BF16)) + carry_s[...]
    rank1 = jnp.sum(jnp.where(lane == i1, before, 0.0), axis=-1, keepdims=True).astype(jnp.int32)
    rank2 = jnp.sum(jnp.where(lane == i2, before, 0.0), axis=-1, keepdims=True).astype(jnp.int32)
    route_ref[...] = jnp.where(lane == 0, i1, jnp.where(lane == 1, i2, jnp.where(
        lane == 2, rank1, jnp.where(lane == 3, rank2, 0))))
    total = carry_s[...] + jnp.sum(sel.astype(F32), axis=0, keepdims=True)
    carry_s[...] = total
    cnt_ref[...] = total.astype(jnp.int32)


def _to_slabs(ref, x, n):
    for c in range(SLAB):
        ref[pl.ds(c, n, stride=SLAB), :] = x[:, c * LANES:(c + 1) * LANES]


def _from_slabs(ref, n):
    return jnp.concatenate([ref[pl.ds(c, n, stride=SLAB), :] for c in range(SLAB)], axis=1)


def _slab(ref, t):
    return ref.at[pl.ds(pl.multiple_of(SLAB * t, SLAB), SLAB), :]


def _dispatch_kernel(dest_ref, h_ref, zero_ref, o_ref, sem, *, tm):
    del zero_ref
    base = pl.program_id(0) * tm

    def issue(r, carry):
        for s in range(2):
            d = dest_ref[2 * (base + r) + s]
            pltpu.make_async_copy(_slab(h_ref, r), _slab(o_ref, d), sem.at[s]).start(priority=s)
        return carry

    lax.fori_loop(0, tm, issue, 0, unroll=4)
    for s in range(2):
        pltpu.make_async_copy(h_ref, o_ref.at[pl.ds(0, tm * SLAB), :], sem.at[s]).wait()


def _experts_kernel(te_ref, nu_ref, hs_ref, wg_ref, wu_ref, wd_ref, o_ref, acc_s, *, nf, tg):
    del te_ref
    i = pl.program_id(0)
    f = pl.program_id(1)

    def partial_out():
        h = _from_slabs(hs_ref, tg).astype(BF16)
        act = (_silu(_dot(h, wg_ref[0, 0])) * _dot(h, wu_ref[0, 0])).astype(BF16)
        return _dot(act, wd_ref[0, 0])

    used = i < nu_ref[0]

    @pl.when(used & (f == 0))
    def _():
        acc_s[...] = partial_out()

    @pl.when(used & (f > 0) & (f < nf - 1))
    def _():
        acc_s[...] += partial_out()

    @pl.when(used & (f == nf - 1))
    def _():
        _to_slabs(o_ref, acc_s[...] + partial_out(), tg)

    @pl.when((i >= nu_ref[0]) & (f == 0))
    def _():
        o_ref[...] = jnp.zeros_like(o_ref)


def _combine_kernel(dest_ref, x1_ref, gm_ref, gate_ref, gf_ref, y_ref, o_ref, buf_s, sem,
                    *, tm, n_tiles, final_norm):
    i = pl.program_id(0)

    def start_tile(t, st):
        def issue(r, carry):
            for s in range(2):
                d = dest_ref[2 * (t * tm + r) + s]
                pltpu.make_async_copy(_slab(y_ref, d), _slab(buf_s.at[st, s], r),
                                      sem.at[st, s]).start(priority=s)
            return carry

        lax.fori_loop(0, tm, issue, 0, unroll=4)

    @pl.when(i == 0)
    def _():
        start_tile(0, 0)

    @pl.when(i + 1 < n_tiles)
    def _():
        start_tile(i + 1, (i + 1) % 2)

    st = i % 2
    for s in range(2):
        pltpu.make_async_copy(y_ref.at[pl.ds(0, tm * SLAB), :], buf_s.at[st, s], sem.at[st, s]).wait()
    gate = gate_ref[...]
    y = gate[:, 0:1] * _from_slabs(buf_s.at[st, 0], tm) + gate[:, 1:2] * _from_slabs(buf_s.at[st, 1], tm)
    x2 = x1_ref[...] + gm_ref[...] * y
    if final_norm:
        ms = jnp.mean(x2 * x2, axis=-1, keepdims=True)
        x2 = x2 * lax.rsqrt(ms + EPS) * gf_ref[...]
    o_ref[...] = x2


def _post_ffn(x, a, mod, wo, g2, w_gu, w_down, layer, rt, pair_major):
    tm, tpg = rt.tm, rt.tpg
    nf = DFF // TF
    row = pl.BlockSpec((tm, D), lambda i, f: (i, 0))
    if pair_major:
        a_spec = pl.BlockSpec((1, NPAIR, tm, LANES), lambda i, f: (i // tpg, 0, i % tpg, 0))
    else:
        a_spec = row
    c2 = lambda shape: pl.BlockSpec(shape, lambda i, f: (0,) * len(shape))
    return pl.pallas_call(
        functools.partial(_ffn_kernel, nf=nf, pair_major=pair_major),
        grid=(rt.n_tiles, nf),
        in_specs=[row, a_spec, rt.mod_spec(2), c2((D, D)), c2((1, D)),
                  rt.mod_spec(4), rt.mod_spec(3), rt.mod_spec(5),
                  pl.BlockSpec((None, D, TF), lambda i, f: (layer, 0, f)),
                  pl.BlockSpec((None, D, TF), lambda i, f: (layer, 0, nf + f)),
                  pl.BlockSpec((None, TF, D), lambda i, f: (layer, f, 0))],
        out_specs=row,
        out_shape=jax.ShapeDtypeStruct((rt.rows, D), F32),
        scratch_shapes=[pltpu.VMEM((tm, D), F32), pltpu.VMEM((tm, D), BF16), pltpu.VMEM((tm, D), F32)],
        compiler_params=_cparams(2),
    )(x, a, mod, wo, g2, mod, mod, mod, w_gu, w_gu, w_down)


def _post_moe(x, a, mod, wo, g2, w_router, w_gu, w_down, layer, rt, final_g):
    tm, rows, n_tiles = rt.tm, rt.rows, rt.n_tiles
    nf = DFF // TF_MOE
    assert nf >= 2
    row = pl.BlockSpec((tm, D), lambda i, *_: (i, 0))
    lanes = pl.BlockSpec((tm, LANES), lambda i, *_: (i, 0))
    full = lambda shape: pl.BlockSpec(shape, lambda i, *_: (0,) * len(shape))
    rows_f32 = jax.ShapeDtypeStruct((rows, D), F32)
    slabs = lambda n: pl.BlockSpec((n * SLAB, LANES), lambda i, *_: (i, 0))
    tg = min(TG, rows // 2)
    x1, hf, route, gate, cnt = pl.pallas_call(
        functools.partial(_route_kernel, tm=tm),
        grid=(n_tiles,),
        in_specs=[row, row, rt.mod_spec(2), full((D, D)), full((1, D)), rt.mod_spec(4), rt.mod_spec(3),
                  full((D, LANES))],
        out_specs=[row, slabs(tm), lanes, lanes, full((1, LANES))],
        out_shape=[rows_f32, jax.ShapeDtypeStruct((rows * SLAB, LANES), F32),
                   jax.ShapeDtypeStruct((rows, LANES), jnp.int32),
                   jax.ShapeDtypeStruct((rows, LANES), F32), jax.ShapeDtypeStruct((1, LANES), jnp.int32)],
        scratch_shapes=[pltpu.VMEM((1, LANES), F32)],
        compiler_params=_cparams(1),
    )(x, a, mod, wo, g2, mod, mod, w_router)

    n_tiles_g = -(-(2 * rows + NEXP * (tg - 1)) // tg)
    tiles_e = (cnt[0, :NEXP] + (tg - 1)) // tg
    tile_end = jnp.cumsum(tiles_e)
    start = (tile_end - tiles_e) * tg
    is_e = route[:, 0:2, None] == jnp.arange(NEXP, dtype=jnp.int32)
    dest = (route[:, 2:4] + jnp.sum(jnp.where(is_e, start, 0), axis=-1)).reshape(-1).astype(jnp.int32)
    tile_ids = jnp.arange(n_tiles_g, dtype=jnp.int32)
    tile_expert = jnp.minimum(jnp.sum(tile_ids[:, None] >= tile_end[None, :], axis=1), NEXP - 1)
    tile_expert = tile_expert.astype(jnp.int32)
    n_used = tile_end[NEXP - 1:].astype(jnp.int32)

    sorted_h = pl.pallas_call(
        functools.partial(_dispatch_kernel, tm=tm),
        grid_spec=pltpu.PrefetchScalarGridSpec(
            num_scalar_prefetch=1, grid=(n_tiles,),
            in_specs=[slabs(tm), pl.BlockSpec(memory_space=pl.ANY)],
            out_specs=pl.BlockSpec(memory_space=pl.ANY),
            scratch_shapes=[pltpu.SemaphoreType.DMA((2,))]),
        out_shape=jax.ShapeDtypeStruct((n_tiles_g * tg * SLAB, LANES), F32),
        input_output_aliases={2: 0},
        compiler_params=_cparams(1),
    )(dest, hf, jnp.zeros((n_tiles_g * tg * SLAB, LANES), F32))

    def tile_of(i, nu):
        return jnp.maximum(jnp.minimum(i, nu[0] - 1), 0)

    def chunk_of(i, f, nu):
        return jnp.where(i < nu[0], f, nf - 1)

    y_sorted = pl.pallas_call(
        functools.partial(_experts_kernel, nf=nf, tg=tg),
        grid_spec=pltpu.PrefetchScalarGridSpec(
            num_scalar_prefetch=2, grid=(n_tiles_g, nf),
            in_specs=[
                pl.BlockSpec((tg * SLAB, LANES), lambda i, f, te, nu: (tile_of(i, nu), 0)),
                pl.BlockSpec((1, 1, D, TF_MOE),
                             lambda i, f, te, nu: (layer, te[tile_of(i, nu)], 0, chunk_of(i, f, nu))),
                pl.BlockSpec((1, 1, D, TF_MOE),
                             lambda i, f, te, nu: (layer, te[tile_of(i, nu)], 0, nf + chunk_of(i, f, nu))),
                pl.BlockSpec((1, 1, TF_MOE, D),
                             lambda i, f, te, nu: (layer, te[tile_of(i, nu)], chunk_of(i, f, nu), 0))],
            out_specs=pl.BlockSpec((tg * SLAB, LANES), lambda i, f, te, nu: (i, 0)),
            scratch_shapes=[pltpu.VMEM((tg, D), F32)]),
        out_shape=jax.ShapeDtypeStruct((n_tiles_g * tg * SLAB, LANES), F32),
        compiler_params=_cparams(2),
    )(tile_expert, n_used, sorted_h, w_gu, w_gu, w_down)

    gf = jnp.ones((1, D), F32) if final_g is None else final_g
    return pl.pallas_call(
        functools.partial(_combine_kernel, tm=tm, n_tiles=n_tiles, final_norm=final_g is not None),
        grid_spec=pltpu.PrefetchScalarGridSpec(
            num_scalar_prefetch=1, grid=(n_tiles,),
            in_specs=[row, rt.mod_spec(5), lanes, full((1, D)), pl.BlockSpec(memory_space=pl.ANY)],
            out_specs=row,
            scratch_shapes=[pltpu.VMEM((2, 2, tm * SLAB, LANES), F32),
                            pltpu.SemaphoreType.DMA((2, 2))]),
        out_shape=rows_f32,
        compiler_params=_cparams(1),
    )(dest, x1, mod, gate, gf, y_sorted)


def _final_norm_kernel(x_ref, g_ref, o_ref):
    x = x_ref[...]
    ms = jnp.mean(x * x, axis=-1, keepdims=True)
    o_ref[...] = x * lax.rsqrt(ms + EPS) * g_ref[...]


def _final_norm(x, g, tm):
    rows = x.shape[0]
    return pl.pallas_call(
        _final_norm_kernel,
        grid=(rows // tm,),
        in_specs=[pl.BlockSpec((tm, D), lambda i: (i, 0)), pl.BlockSpec((1, D), lambda i: (0, 0))],
        out_specs=pl.BlockSpec((tm, D), lambda i: (i, 0)),
        out_shape=jax.ShapeDtypeStruct((rows, D), F32),
        compiler_params=_cparams(1),
    )(x, g)


def _prep_weights(p):
    w = {}
    fw = p['fox_w_in']
    w['wq'] = fw[:, :, 0 * D:1 * D].astype(BF16)
    w['wk'] = fw[:, :, 1 * D:2 * D].astype(BF16)
    w['wv'] = fw[:, :, 2 * D:3 * D].astype(BF16)
    w['wf'] = jnp.pad(fw[:, :, 3 * D:], ((0, 0), (0, 0), (0, LANES - H))).astype(BF16)
    w['bf'] = jnp.pad(p['fox_b_f'], ((0, 0), (0, LANES - H)))[:, None, :]
    w['wkt'] = w['wk'].transpose(0, 2, 1)
    w['wvt'] = w['wv'].transpose(0, 2, 1)
    w['wft'] = fw[:, :, 3 * D:].transpose(0, 2, 1).astype(BF16)
    w['bft'] = p['fox_b_f'][:, :, None]
    w['fox_wo'] = p['fox_w_out'].astype(BF16)
    w['lru_wx'] = p['lru_w_in'][:, :, :D].astype(BF16)
    w['lru_wg'] = p['lru_w_in'][:, :, D:].astype(BF16)
    nblk = p['lru_w_a'].shape[1]
    eye = jnp.eye(nblk, dtype=F32)
    bd = lambda m: jnp.einsum('lnkj,nm->lnkmj', m, eye).reshape(m.shape[0], D, D).astype(BF16)
    w['lru_wa'] = bd(p['lru_w_a'])
    w['lru_wi'] = bd(p['lru_w_i'])
    w['lru_wo'] = p['lru_w_out'].astype(BF16)
    w['ffn_gu'] = p['ffn_w_gu'].astype(BF16)
    w['ffn_down'] = p['ffn_w_down'].astype(BF16)
    w['router'] = jnp.pad(p['moe_router'], ((0, 0), (0, 0), (0, LANES - NEXP)))
    w['moe_gu'] = p['moe_w_gu'].astype(BF16)
    w['moe_down'] = p['moe_w_down'].astype(BF16)
    return w


def _trunk(x, mods, p, w, rts, prompt, seq, fox_past, lru_past):
    rt, rt_ffn, rt_moe = rts
    depth = p['norm_mix_g'].shape[0]
    n_fox = w['wq'].shape[0]
    final_g = p['final_g'][None, :]
    fox_new, lru_new = [], []
    for l in range(depth):
        j = l // 2
        last = l == depth - 1
        mod = mods[l]
        g1 = p['norm_mix_g'][l][None, :]
        g2 = p['norm_ffn_g'][l][None, :]
        if l % 2 == 0:
            if prompt:
                kt, vt, lft, qa, kat, vp = _fox_in_prompt(
                    x, g1, mod, w['wq'][j], w['wkt'][j], w['wvt'][j], w['wv'][j], w['wf'][j], w['bf'][j],
                    w['wft'][j], w['bft'][j], rt, seq, j, n_fox, fox_new[0] if fox_new else None)
                a = _attn_prompt(qa, kat, vp, tq=kat.shape[4])
                fox_new = [(kt, vt, lft)]
            else:
                k, v, lf, q = _fox_in_sample(x, g1, mod, w['wq'][j], w['wk'][j], w['wv'][j], w['wf'][j],
                                             w['bf'][j], rt)
                nb = rt.rows // seq
                to_b = lambda t: t.reshape(seq, nb, -1).transpose(1, 0, 2)
                kb, vb, lfb = to_b(k), to_b(v), to_b(lf)
                ck, cv, clf = fox_past
                lfn_t = jnp.pad(lfb.transpose(0, 2, 1), ((0, 0), (0, 0), (0, LANES - seq)))
                ab = _attn_sample(to_b(q), kb, vb, lfn_t, ck, cv, clf, j)
                a = ab.transpose(1, 0, 2).reshape(rt.rows, D)
                fox_new.append((kb, vb, lfb))
            x = _post_ffn(x, a, mod, w['fox_wo'][j], g2, w['ffn_gu'], w['ffn_down'], j, rt_ffn,
                          pair_major=prompt)
            if last:
                x = _final_norm(x, final_g, rt.tm)
        else:
            h0, c0, stride = lru_past[j]
            y, hl, cn = _lru_mix(x, g1, mod, w['lru_wx'][j], w['lru_wg'][j], p['lru_conv_w'][j],
                                 p['lru_conv_b'][j][None, :], w['lru_wa'][j], p['lru_b_a'][j][None, :],
                                 w['lru_wi'][j], p['lru_b_i'][j][None, :], p['lru_lam'][j][None, :],
                                 h0, c0, rt, stride)
            lru_new.append((hl, cn))
            x = _post_moe(x, y, mod, w['lru_wo'][j], g2, w['router'][j], w['moe_gu'], w['moe_down'], j,
                          rt_moe, final_g if last else None)
    return x, fox_new, lru_new


def kernel(x_prompt, x_sample, c_prompt, c_sample, cache_k, cache_v, cache_logf, state_h, state_conv,
           norm_mix_g, norm_ffn_g, final_g, w_ada, b_ada, fox_w_in, fox_b_f, fox_w_out,
           lru_w_in, lru_conv_w, lru_conv_b, lru_w_a, lru_b_a, lru_w_i, lru_b_i, lru_lam, lru_w_out,
           ffn_w_gu, ffn_w_down, moe_router, moe_w_gu, moe_w_down):
    p = {
        'norm_mix_g': norm_mix_g, 'norm_ffn_g': norm_ffn_g, 'final_g': final_g,
        'fox_w_in': fox_w_in, 'fox_b_f': fox_b_f, 'fox_w_out': fox_w_out,
        'lru_w_in': lru_w_in, 'lru_conv_w': lru_conv_w, 'lru_conv_b': lru_conv_b,
        'lru_w_a': lru_w_a, 'lru_b_a': lru_b_a, 'lru_w_i': lru_w_i, 'lru_b_i': lru_b_i,
        'lru_lam': lru_lam, 'lru_w_out': lru_w_out,
        'ffn_w_gu': ffn_w_gu, 'ffn_w_down': ffn_w_down,
        'moe_router': moe_router, 'moe_w_gu': moe_w_gu, 'moe_w_down': moe_w_down,
    }
    bp, sp, _ = x_prompt.shape
    bs, ss, _ = x_sample.shape
    depth = w_ada.shape[0]
    n_fox, n_lru = cache_k.shape[0], state_h.shape[0]
    past = cache_k.shape[2]
    w = _prep_weights(p)

    nc = bp + bs
    ncp = -(-nc // 8) * 8
    c_all = jnp.pad(jnp.concatenate([c_prompt, c_sample], axis=0), ((0, ncp - nc), (0, 0)))
    mod_all = _ada_mod(c_all, w_ada, b_ada)
    mods_p = [mod_all[l, :bp].reshape(bp, 6, 1, D).transpose(1, 0, 2, 3) for l in range(depth)]
    mods_s = [jnp.tile(mod_all[l, bp:nc].reshape(bs, 6, D).transpose(1, 0, 2), (1, ss, 1))[:, None]
              for l in range(depth)]

    rts_p = tuple(_Rows(bp * sp, tm, bp, per_row_mod=False) for tm in (TM_MIX, TM_FFN, TM_MOE))
    zeros_h = jnp.zeros((bp, 1, D), F32)
    zeros_c = jnp.zeros((bp, CONV_W - 1, D), F32)
    y_p, fox_p, lru_p = _trunk(x_prompt.reshape(bp * sp, D), mods_p, p, w, rts_p, True, sp,
                               None, [(zeros_h, zeros_c, 1)] * n_lru)

    rt_s = _Rows(bs * ss, bs * ss, 1, per_row_mod=True)
    xs = x_sample.transpose(1, 0, 2).reshape(ss * bs, D)
    ckt = cache_k.transpose(0, 1, 3, 4, 2).reshape(n_fox, bs, D, past)
    cvt = cache_v.transpose(0, 1, 3, 4, 2).reshape(n_fox, bs, D, past)
    clft = cache_logf.transpose(0, 1, 3, 2)
    lru_past = [(state_h[j][None], state_conv[j].transpose(1, 0, 2).reshape(1, (CONV_W - 1) * bs, D), bs)
                for j in range(n_lru)]
    y_s, fox_s, lru_s = _trunk(xs, mods_s, p, w, (rt_s, rt_s, rt_s), False, ss, (ckt, cvt, clft), lru_past)

    y_prompt = y_p.reshape(bp, sp, D)
    y_sample = y_s.reshape(ss, bs, D).transpose(1, 0, 2)
    kt_all, vt_all, lft_all = fox_p[0]
    k_prompt = kt_all.reshape(n_fox, bp, H, HD, sp).transpose(0, 1, 4, 2, 3)
    v_prompt = vt_all.reshape(n_fox, bp, H, HD, sp).transpose(0, 1, 4, 2, 3)
    logf_prompt = lft_all.transpose(0, 1, 3, 2)
    h_prompt = jnp.stack([s[0][:, 0, :] for s in lru_p])
    conv_prompt = jnp.stack([s[1] for s in lru_p])
    k_sample = jnp.stack([f[0] for f in fox_s]).reshape(n_fox, bs, ss, H, HD)
    v_sample = jnp.stack([f[1] for f in fox_s]).reshape(n_fox, bs, ss, H, HD)
    logf_sample = jnp.stack([f[2] for f in fox_s])
    h_sample = jnp.stack([s[0][0] for s in lru_s])
    conv_sample = jnp.stack([s[1][0].reshape(CONV_W - 1, bs, D).transpose(1, 0, 2) for s in lru_s])
    return (y_prompt, y_sample, k_prompt, v_prompt, logf_prompt, h_prompt, conv_prompt,
            k_sample, v_sample, logf_sample, h_sample, conv_sample)
```
